```python
import math
import jax
import jax.numpy as jnp
from jax import lax
import numpy as np

D_MODEL = 1024
BATCH = 16
SEQ = 256
DEPTH = 2
DEC_BATCH = 4
DEC_SEQ = 1024
PAST_LEN = 512

GRID_W = 64
N_EVEN = (DEPTH + 1) // 2
N_ODD = DEPTH // 2
ALPHA = (2 * DEPTH) ** 0.25
BETA = (8 * DEPTH) ** -0.25
LN_EPS = 1e-5
RMS_EPS = 1e-6
A_HEADS = 4
A_GROUP = 128
A_WIDTH = A_HEADS * A_GROUP
CHUNK = 128
POOL_WINDOWS = (2, 4, 8, 16)
B_GROUP = 128
B_WIDTH = len(POOL_WINDOWS) * B_GROUP
EVEN_IN = 2 * A_WIDTH + B_WIDTH
EVEN_MIX = A_WIDTH + B_WIDTH
W_C = D_MODEL // 2
C_HEADS = 8
C_BLOCK = W_C // C_HEADS
RG_CONV = 4
RG_C = 8.0
D_HEADS = 8
Q_LORA = 384
KV_LORA = 256
QK_NOPE = 64
QK_ROPE = 32
V_DIM = 64
ROPE_BASE = 10000.0
Q_BLOCK = 128
ATTN_SCALE = 1.0 / math.sqrt(QK_NOPE + QK_ROPE)
ODD_IN = 2 * W_C + Q_LORA + KV_LORA + QK_ROPE
ODD_MIX = W_C + D_HEADS * V_DIM
D_FF = 2816
FFN_CONV = 3

kernel_name = "hybrid_diffusion_prefix_trunk_step"

F32 = jnp.float32


def _standardize(x, eps):
    xf = x.astype(F32)
    mu = jnp.mean(xf, axis=-1, keepdims=True)
    var = jnp.mean(jnp.square(xf - mu), axis=-1, keepdims=True)
    return (xf - mu) * lax.rsqrt(var + eps)


def _layernorm(x, g, b):
    return (_standardize(x, LN_EPS) * g.astype(F32) + b.astype(F32)).astype(x.dtype)


def _rmsnorm(x, g):
    xf = x.astype(F32)
    y = xf * lax.rsqrt(jnp.mean(jnp.square(xf), axis=-1, keepdims=True) + RMS_EPS)
    return (y * g.astype(F32)).astype(x.dtype)


def _dwconv(x, w, b, pad_lo, pad_hi):
    C = x.shape[-1]
    y = lax.conv_general_dilated(
        x, w[:, None, :].astype(x.dtype), window_strides=(1,),
        padding=[(pad_lo, pad_hi)], dimension_numbers=('NWC', 'WIO', 'NWC'),
        feature_group_count=C)
    return y + b.astype(x.dtype)


def _axial_rope(rows):
    half = QK_ROPE // 2
    inv = ROPE_BASE ** (-jnp.arange(0, half, 2, dtype=F32) / half)
    r = jnp.repeat(jnp.arange(rows, dtype=F32), GRID_W)
    col = jnp.tile(jnp.arange(GRID_W, dtype=F32), rows)
    ang = jnp.concatenate([r[:, None] * inv, col[:, None] * inv], axis=-1)
    return jnp.cos(ang), jnp.sin(ang)


def _rope(x, cos, sin):
    xf = x.astype(F32)
    x1, x2 = xf[..., 0::2], xf[..., 1::2]
    out = jnp.stack([x1 * cos - x2 * sin, x1 * sin + x2 * cos], axis=-1)
    return out.reshape(x.shape).astype(x.dtype)


def _chunk_spatial_gate(u, v, w_s, b_s):
    B, L, _ = u.shape
    n = L // CHUNK
    vn = _standardize(v, LN_EPS).astype(v.dtype)
    vh = vn.reshape(B, n, CHUNK, A_HEADS, A_GROUP)
    s = jnp.einsum('hpq,bnqhc->bnphc', w_s, vh) + b_s.T[None, None, :, :, None]
    return u * s.reshape(B, L, A_WIDTH).astype(u.dtype)


def _multiscale_pool(p, w_pool, pool_scale):
    B, L, _ = p.shape
    pg = p.reshape(B, L, len(POOL_WINDOWS), B_GROUP)
    pgf = pg.astype(F32)
    cs = jnp.concatenate([jnp.zeros((B, 1, len(POOL_WINDOWS), B_GROUP), F32),
                          jnp.cumsum(pgf, axis=1)], axis=1)
    t = jnp.arange(L)
    outs = []
    for gi, w in enumerate(POOL_WINDOWS):
        lo = jnp.clip(t - w // 2, 0, L)
        hi = jnp.clip(t + w - w // 2, 0, L)
        csg = cs[:, :, gi]
        cnt = (hi - lo).astype(F32)[:, None]
        outs.append((csg[:, hi] - csg[:, lo]) / cnt - pgf[:, :, gi])
    pooled = jnp.stack(outs, axis=2).astype(p.dtype)
    y = jnp.einsum('blgc,gcd->blgd', pooled, w_pool).reshape(B, L, B_WIDTH)
    return y * pool_scale


def _even_mixer(h, w_in, w_s, b_s, w_pool, pool_scale, w_out):
    z = h @ w_in
    u = jax.nn.gelu(z[..., :A_WIDTH])
    v = jax.nn.gelu(z[..., A_WIDTH:2 * A_WIDTH])
    p = z[..., 2 * A_WIDTH:]
    y_a = _chunk_spatial_gate(u, v, w_s, b_s)
    y_b = _multiscale_pool(p, w_pool, pool_scale)
    return jnp.concatenate([y_a, y_b], axis=-1) @ w_out


def _rglru_coeffs(xc, w_a, b_a, w_x, b_x, lam):
    B, L, _ = xc.shape
    xh = xc.reshape(B, L, C_HEADS, C_BLOCK)
    r = jax.nn.sigmoid((jnp.einsum('blhi,hij->blhj', xh, w_a).reshape(B, L, W_C) + b_a).astype(F32))
    i = jax.nn.sigmoid((jnp.einsum('blhi,hij->blhj', xh, w_x).reshape(B, L, W_C) + b_x).astype(F32))
    log_a = -RG_C * r * jax.nn.softplus(-lam.astype(F32))
    a = jnp.exp(log_a)
    bx = jnp.sqrt(-jnp.expm1(2.0 * log_a)) * i * xc.astype(F32)
    return a, bx


def _linear_scan(a, bx, h0, reverse):
    def step(h, ab):
        h = ab[0] * h + ab[1]
        return h, h
    hT, hs = lax.scan(step, h0, (jnp.swapaxes(a, 0, 1), jnp.swapaxes(bx, 0, 1)), reverse=reverse)
    return jnp.swapaxes(hs, 0, 1), hT


def _rglru(xr, gr, conv_w, conv_b, w_a, b_a, w_x, b_x, lam, h0_fwd, h0_bwd):
    xc = _dwconv(xr, conv_w, conv_b, 1, 2)
    a_f, b_f = _rglru_coeffs(xc, w_a[0], b_a[0], w_x[0], b_x[0], lam[0])
    h_f, hT_f = _linear_scan(a_f, b_f, h0_fwd, False)
    a_b, b_b = _rglru_coeffs(xc, w_a[1], b_a[1], w_x[1], b_x[1], lam[1])
    h_b, hT_b = _linear_scan(a_b, b_b, h0_bwd, True)
    y = (h_f + h_b).astype(xr.dtype) * jax.nn.gelu(gr)
    return y, hT_f, hT_b


def _mla_queries(q_lat, q_g, w_uq):
    q = jnp.einsum('blr,rhe->blhe', _rmsnorm(q_lat, q_g), w_uq)
    return q[..., :QK_NOPE], q[..., QK_NOPE:]


def _mla_kv(ckv, w_uk, w_uv):
    return (jnp.einsum('blr,rhe->blhe', ckv, w_uk), jnp.einsum('blr,rhe->blhe', ckv, w_uv))


def _attend(q_nope, q_rope, k_nope, k_rope, v):
    B, Lq, H, _ = q_nope.shape
    nb = Lq // Q_BLOCK
    qn = jnp.moveaxis(q_nope.reshape(B, nb, Q_BLOCK, H, QK_NOPE), 1, 0)
    qr = jnp.moveaxis(q_rope.reshape(B, nb, Q_BLOCK, H, QK_ROPE), 1, 0)

    def block(args):
        qn_b, qr_b = args
        s = (jnp.einsum('bqhd,bkhd->bhqk', qn_b, k_nope, preferred_element_type=F32)
             + jnp.einsum('bqhd,bkd->bhqk', qr_b, k_rope, preferred_element_type=F32)) * ATTN_SCALE
        pr = jax.nn.softmax(s, axis=-1).astype(v.dtype)
        return jnp.einsum('bhqk,bkhd->bqhd', pr, v)

    o = lax.map(block, (qn, qr))
    return jnp.moveaxis(o, 0, 1).reshape(B, Lq, H * V_DIM)


def _odd_split(h, w_in):
    z = h @ w_in
    o1 = W_C
    o2 = 2 * W_C
    o3 = o2 + Q_LORA
    o4 = o3 + KV_LORA
    return z[..., :o1], z[..., o1:o2], z[..., o2:o3], z[..., o3:o4], z[..., o4:]


def _odd_context(h, w_in, conv_w, conv_b, w_a, b_a, w_x, b_x, lam, q_g, kv_g, w_uq, w_uk, w_uv, w_out):
    xr, gr, q_lat, kv_lat, kr = _odd_split(h, w_in)
    zero = jnp.zeros((h.shape[0], W_C), F32)
    y_c, hT_f, hT_b = _rglru(xr, gr, conv_w, conv_b, w_a, b_a, w_x, b_x, lam, zero, zero)
    ckv = _rmsnorm(kv_lat, kv_g)
    qn, qr = _mla_queries(q_lat, q_g, w_uq)
    kn, v = _mla_kv(ckv, w_uk, w_uv)
    y_d = _attend(qn, qr, kn, kr, v)
    out = jnp.concatenate([y_c, y_d], axis=-1) @ w_out
    return out, ckv, kr, jnp.stack([hT_f, hT_b], axis=1)


def _odd_latent(h, ctx_ckv, ctx_kr, ctx_state, cos, sin,
                w_in, conv_w, conv_b, w_a, b_a, w_x, b_x, lam, q_g, kv_g, w_uq, w_uk, w_uv, w_out):
    xr, gr, q_lat, kv_lat, kr = _odd_split(h, w_in)
    st = ctx_state.astype(F32)
    y_c, _, _ = _rglru(xr, gr, conv_w, conv_b, w_a, b_a, w_x, b_x, lam, st[:, 0], st[:, 1])
    ckv = _rmsnorm(kv_lat, kv_g)
    qn, qr = _mla_queries(q_lat, q_g, w_uq)
    qr = _rope(qr, cos[:, None, :], sin[:, None, :])
    kr = _rope(kr, cos, sin)
    kn_all, v_all = _mla_kv(jnp.concatenate([ctx_ckv.astype(ckv.dtype), ckv], axis=1), w_uk, w_uv)
    kr_all = jnp.concatenate([ctx_kr.astype(kr.dtype), kr], axis=1)
    y_d = _attend(qn, qr, kn_all, kr_all, v_all)
    return jnp.concatenate([y_c, y_d], axis=-1) @ w_out


def _conv_ffn(h, w_up, conv_w, conv_b, w_down):
    z = _dwconv(h @ w_up, conv_w, conv_b, 1, 1)
    g, v = z[..., :D_FF], z[..., D_FF:]
    return (jax.nn.silu(g) * v) @ w_down


def setup_inputs(seed: int = 0) -> dict:
    key = jax.random.key(seed)
    ks = iter(jax.random.split(key, 48))
    D = D_MODEL

    def nrm(shape, scale):
        return scale * jax.random.normal(next(ks), shape, F32)

    u = jax.random.uniform(next(ks), (N_ODD, 2, W_C), F32, 0.9, 0.999)
    a0 = u ** (1.0 / RG_C)
    rg_lam = jnp.log(a0) - jnp.log1p(-a0)
    return {
        'x_prompt': nrm((BATCH, SEQ, D), 1.0),
        'x_sample': nrm((DEC_BATCH, DEC_SEQ, D), 1.0),
        'cache_mla_ckv': nrm((DEC_BATCH, N_ODD, PAST_LEN, KV_LORA), 1.0),
        'cache_mla_krope': nrm((DEC_BATCH, N_ODD, PAST_LEN, QK_ROPE), 1.0),
        'state_rglru': nrm((DEC_BATCH, N_ODD, 2, W_C), 0.5),
        'c': nrm((DEC_BATCH, D), 1.0),
        'c_ctx': nrm((D,), 1.0),
        'w_mod': nrm((DEPTH, D, 6 * D), 0.5 * D ** -0.5),
        'b_mod': nrm((DEPTH, 6 * D), 0.02),
        'ln1_g': 1.0 + nrm((DEPTH, D), 0.02),
        'ln1_b': nrm((DEPTH, D), 0.02),
        'ln2_g': 1.0 + nrm((DEPTH, D), 0.02),
        'ln2_b': nrm((DEPTH, D), 0.02),
        'even_w_in': nrm((N_EVEN, D, EVEN_IN), D ** -0.5),
        'even_w_s': nrm((N_EVEN, A_HEADS, CHUNK, CHUNK), CHUNK ** -0.5),
        'even_b_s': 1.0 + nrm((N_EVEN, A_HEADS, CHUNK), 0.02),
        'even_w_pool': nrm((N_EVEN, len(POOL_WINDOWS), B_GROUP, B_GROUP), B_GROUP ** -0.5),
        'even_pool_scale': 1.0 + nrm((N_EVEN, B_WIDTH), 0.1),
        'even_w_out': nrm((N_EVEN, EVEN_MIX, D), BETA * EVEN_MIX ** -0.5),
        'odd_w_in': nrm((N_ODD, D, ODD_IN), D ** -0.5),
        'rg_conv_w': nrm((N_ODD, RG_CONV, W_C), RG_CONV ** -0.5),
        'rg_conv_b': nrm((N_ODD, W_C), 0.02),
        'rg_w_a': nrm((N_ODD, 2, C_HEADS, C_BLOCK, C_BLOCK), C_BLOCK ** -0.5),
        'rg_b_a': nrm((N_ODD, 2, W_C), 0.02),
        'rg_w_x': nrm((N_ODD, 2, C_HEADS, C_BLOCK, C_BLOCK), C_BLOCK ** -0.5),
        'rg_b_x': nrm((N_ODD, 2, W_C), 0.02),
        'rg_lam': rg_lam,
        'mla_q_g': 1.0 + nrm((N_ODD, Q_LORA), 0.02),
        'mla_kv_g': 1.0 + nrm((N_ODD, KV_LORA), 0.02),
        'mla_w_uq': nrm((N_ODD, Q_LORA, D_HEADS, QK_NOPE + QK_ROPE), Q_LORA ** -0.5),
        'mla_w_uk': nrm((N_ODD, KV_LORA, D_HEADS, QK_NOPE), KV_LORA ** -0.5),
        'mla_w_uv': nrm((N_ODD, KV_LORA, D_HEADS, V_DIM), KV_LORA ** -0.5),
        'odd_w_out': nrm((N_ODD, ODD_MIX, D), BETA * ODD_MIX ** -0.5),
        'ffn_w_up': nrm((DEPTH, D, 2 * D_FF), D ** -0.5),
        'ffn_conv_w': nrm((DEPTH, FFN_CONV, 2 * D_FF), FFN_CONV ** -0.5),
        'ffn_conv_b': nrm((DEPTH, 2 * D_FF), 0.02),
        'ffn_w_down': nrm((DEPTH, D_FF, D), BETA * D_FF ** -0.5),
    }


def reference(x_prompt, x_sample, cache_mla_ckv, cache_mla_krope, state_rglru, c, c_ctx,
              w_mod, b_mod, ln1_g, ln1_b, ln2_g, ln2_b,
              even_w_in, even_w_s, even_b_s, even_w_pool, even_pool_scale, even_w_out,
              odd_w_in, rg_conv_w, rg_conv_b, rg_w_a, rg_b_a, rg_w_x, rg_b_x, rg_lam,
              mla_q_g, mla_kv_g, mla_w_uq, mla_w_uk, mla_w_uv, odd_w_out,
              ffn_w_up, ffn_conv_w, ffn_conv_b, ffn_w_down):
    rows = x_sample.shape[1] // GRID_W
    cos_lat, sin_lat = _axial_rope(rows)
    s_ctx = jax.nn.silu(c_ctx)
    s_lat = jax.nn.silu(c)
    xp, xs = x_prompt, x_sample
    ckv_list, kr_list, st_list = [], [], []
    for l in range(DEPTH):
        sh1c, sc1c, g1c, sh2c, sc2c, g2c = jnp.split(s_ctx @ w_mod[l] + b_mod[l], 6, axis=-1)
        sh1s, sc1s, g1s, sh2s, sc2s, g2s = jnp.split((s_lat @ w_mod[l] + b_mod[l])[:, None, :], 6, axis=-1)
        hp = xp * (1 + sc1c) + sh1c
        hs = xs * (1 + sc1s) + sh1s
        if l % 2 == 0:
            e = l // 2
            dp = _even_mixer(hp, even_w_in[e], even_w_s[e], even_b_s[e], even_w_pool[e],
                             even_pool_scale[e], even_w_out[e])
            ds = _even_mixer(hs, even_w_in[e], even_w_s[e], even_b_s[e], even_w_pool[e],
                             even_pool_scale[e], even_w_out[e])
        else:
            o = l // 2
            dp, ckv, kr, st = _odd_context(
                hp, odd_w_in[o], rg_conv_w[o], rg_conv_b[o], rg_w_a[o], rg_b_a[o], rg_w_x[o], rg_b_x[o],
                rg_lam[o], mla_q_g[o], mla_kv_g[o], mla_w_uq[o], mla_w_uk[o], mla_w_uv[o], odd_w_out[o])
            ckv_list.append(ckv)
            kr_list.append(kr)
            st_list.append(st)
            ds = _odd_latent(
                hs, cache_mla_ckv[:, o], cache_mla_krope[:, o], state_rglru[:, o], cos_lat, sin_lat,
                odd_w_in[o], rg_conv_w[o], rg_conv_b[o], rg_w_a[o], rg_b_a[o], rg_w_x[o], rg_b_x[o],
                rg_lam[o], mla_q_g[o], mla_kv_g[o], mla_w_uq[o], mla_w_uk[o], mla_w_uv[o], odd_w_out[o])
        xp = _layernorm(ALPHA * xp + g1c * dp, ln1_g[l], ln1_b[l])
        xs = _layernorm(ALPHA * xs + g1s * ds, ln1_g[l], ln1_b[l])
        hp = xp * (1 + sc2c) + sh2c
        hs = xs * (1 + sc2s) + sh2s
        fp = _conv_ffn(hp, ffn_w_up[l], ffn_conv_w[l], ffn_conv_b[l], ffn_w_down[l])
        fs = _conv_ffn(hs, ffn_w_up[l], ffn_conv_w[l], ffn_conv_b[l], ffn_w_down[l])
        xp = _layernorm(ALPHA * xp + g2c * fp, ln2_g[l], ln2_b[l])
        xs = _layernorm(ALPHA * xs + g2s * fs, ln2_g[l], ln2_b[l])
    new_mla_ckv = jnp.stack(ckv_list, axis=1)
    new_mla_krope = jnp.stack(kr_list, axis=1)
    new_rglru_state = jnp.stack(st_list, axis=1)
    return (xp, xs, new_mla_ckv, new_mla_krope, new_rglru_state)
```

```python
import functools
import math

import jax
import jax.numpy as jnp
from jax import lax
from jax.experimental import pallas as pl
from jax.experimental.pallas import tpu as pltpu

F32 = jnp.float32
BF16 = jnp.bfloat16

D = 1024
DEPTH = 2
GRID_W = 64
ALPHA = (2 * DEPTH) ** 0.25
LN_EPS = 1e-5
RMS_EPS = 1e-6
A_HEADS = 4
A_WIDTH = 512
A_GROUP_W = A_WIDTH // A_HEADS
CHUNK = 128
POOL_WINDOWS = (2, 4, 8, 16)
B_GROUP = 128
B_WIDTH = 512
W_C = 512
C_HEADS = 8
C_BLOCK = 64
RG_C = 8.0
D_HEADS = 8
Q_LORA = 384
KV_LORA = 256
QK_NOPE = 64
QK_ROPE = 32
V_DIM = 64
ROPE_BASE = 10000.0
ATTN_SCALE = 1.0 / math.sqrt(QK_NOPE + QK_ROPE)
D_FF = 2816

LANES = 128
SUBLANES = 8
HEAD_PAD = LANES
ODD_IN_PAD = 2 * W_C + Q_LORA + KV_LORA + HEAD_PAD
KR_LANE = QK_NOPE
TILE_ROWS = 1024
ROW_BLOCK = 256
FF_CHUNK = 256
VMEM_LIMIT = 56 * 1024 * 1024


def _dot(a, b):
    return jnp.dot(a, b, preferred_element_type=F32)


def _gelu(x):
    return x * (0.5 * (1.0 + jnp.tanh(0.7978845608028654 * (x + 0.044715 * (x * x * x)))))


def _layernorm(r, g, b):
    mu = jnp.mean(r, axis=-1, keepdims=True)
    c = r - mu
    var = jnp.mean(c * c, axis=-1, keepdims=True)
    return c * lax.rsqrt(var + LN_EPS) * g + b


def _row_pos(shape, seq_len):
    return lax.broadcasted_iota(jnp.int32, shape, 0) & (seq_len - 1)


def _shift_rows(x, d, pos, seq_len):
    n = x.shape[0]
    y = pltpu.roll(x, (-d) % n, 0)
    valid = (pos < seq_len - d) if d > 0 else (pos >= -d)
    return jnp.where(valid, y, 0.0)


def _mod_kernel(s_ref, w_ref, b_ref, o_ref):
    s = s_ref[...]
    s = s * jax.nn.sigmoid(s)
    o_ref[...] = _dot(s.astype(BF16), w_ref[...].astype(BF16)) + b_ref[...]


def _modulation(cond, w_mod, b_mod):
    nc = 1536
    return pl.pallas_call(
        _mod_kernel,
        grid=(DEPTH, 6 * D // nc),
        in_specs=[
            pl.BlockSpec((SUBLANES, D), lambda l, j: (0, 0)),
            pl.BlockSpec((None, D, nc), lambda l, j: (l, 0, j)),
            pl.BlockSpec((None, 1, nc), lambda l, j: (l, 0, j)),
        ],
        out_specs=pl.BlockSpec((None, SUBLANES, nc), lambda l, j: (l, 0, j)),
        out_shape=jax.ShapeDtypeStruct((DEPTH, SUBLANES, 6 * D), F32),
        compiler_params=pltpu.CompilerParams(
            dimension_semantics=("arbitrary", "arbitrary"), vmem_limit_bytes=VMEM_LIMIT),
        name="modulation",
    )(cond, w_mod, b_mod.reshape(DEPTH, 1, 6 * D))


def _mod_rows(mod_ref, row0, row_step):
    row = row0 + pl.program_id(0) * row_step
    m = mod_ref[pl.ds(row, 1), :]
    return m[:, 0:D], m[:, D:2 * D], m[:, 2 * D:3 * D]


def _const_spec(shape):
    nd = len(shape)
    return pl.BlockSpec(shape, lambda *_: (0,) * nd, pipeline_mode=pl.Buffered(1))


def _even_kernel(x_ref, mod_ref, win_ref, ws_ref, bst_ref, wpool_ref, pscale_ref, wout_ref,
                 lng_ref, lnb_ref, o_ref, u_ref, vn_ref, p_ref, ycat_ref, *, seq_len, row0, row_step):
    T = x_ref.shape[0]
    shift, scale, gate = _mod_rows(mod_ref, row0, row_step)

    def proj(rb, carry):
        rows = pl.ds(pl.multiple_of(rb * ROW_BLOCK, ROW_BLOCK), ROW_BLOCK)
        hb = (x_ref[rows, :] * (1.0 + scale) + shift).astype(BF16)
        u_ref[rows, :] = _gelu(_dot(hb, win_ref[:, 0:A_WIDTH]))
        v = _gelu(_dot(hb, win_ref[:, A_WIDTH:2 * A_WIDTH]))
        mu = jnp.mean(v, axis=-1, keepdims=True)
        c = v - mu
        var = jnp.mean(c * c, axis=-1, keepdims=True)
        vn_ref[rows, :] = (c * lax.rsqrt(var + LN_EPS)).astype(BF16)
        p_ref[rows, :] = _dot(hb, win_ref[:, 2 * A_WIDTH:])
        return carry

    lax.fori_loop(0, T // ROW_BLOCK, proj, 0)

    def gate_chunk(n, carry):
        rows = pl.ds(pl.multiple_of(n * CHUNK, CHUNK), CHUNK)
        for h in range(A_HEADS):
            cols = slice(h * A_GROUP_W, (h + 1) * A_GROUP_W)
            s = _dot(ws_ref[h], vn_ref[rows, cols]) + bst_ref[:, h:h + 1]
            ycat_ref[rows, cols] = (u_ref[rows, cols] * s).astype(BF16)
        return carry

    lax.fori_loop(0, T // CHUNK, gate_chunk, 0)

    pos = _row_pos((T, B_GROUP), seq_len)
    for gi, w in enumerate(POOL_WINDOWS):
        cols = slice(gi * B_GROUP, (gi + 1) * B_GROUP)
        half = w // 2
        pg = p_ref[:, cols]
        fwd = pg
        bwd = _shift_rows(pg, -1, pos, seq_len)
        k = 1
        while k < half:
            fwd = fwd + _shift_rows(fwd, k, pos, seq_len)
            bwd = bwd + _shift_rows(bwd, -k, pos, seq_len)
            k *= 2
        cnt = jnp.minimum(pos + half, seq_len) - jnp.maximum(pos - half, 0)
        pooled = (fwd + bwd) / cnt.astype(F32) - pg
        yb = _dot(pooled.astype(BF16), wpool_ref[gi]) * pscale_ref[:, cols]
        ycat_ref[:, A_WIDTH + gi * B_GROUP:A_WIDTH + (gi + 1) * B_GROUP] = yb.astype(BF16)

    def out(rb, carry):
        rows = pl.ds(pl.multiple_of(rb * ROW_BLOCK, ROW_BLOCK), ROW_BLOCK)
        d = _dot(ycat_ref[rows, :], wout_ref[...])
        r = ALPHA * x_ref[rows, :] + gate * d
        o_ref[rows, :] = _layernorm(r, lng_ref[...], lnb_ref[...])
        return carry

    lax.fori_loop(0, T // ROW_BLOCK, out, 0)


def _even_layer(x, mod_l, w, seq_len, row0, row_step, name):
    n_rows = x.shape[0]
    kern = functools.partial(_even_kernel, seq_len=seq_len, row0=row0, row_step=row_step)
    return pl.pallas_call(
        kern,
        grid=(n_rows // TILE_ROWS,),
        in_specs=[
            pl.BlockSpec((TILE_ROWS, D), lambda i: (i, 0)),
            pl.BlockSpec((SUBLANES, 3 * D), lambda i: (0, 0)),
            _const_spec(w["w_in"].shape),
            _const_spec(w["w_s"].shape),
            _const_spec(w["b_s_t"].shape),
            _const_spec(w["w_pool"].shape),
            _const_spec(w["pool_scale"].shape),
            _const_spec(w["w_out"].shape),
            _const_spec((1, D)),
            _const_spec((1, D)),
        ],
        out_specs=pl.BlockSpec((TILE_ROWS, D), lambda i: (i, 0)),
        out_shape=jax.ShapeDtypeStruct((n_rows, D), F32),
        scratch_shapes=[
            pltpu.VMEM((TILE_ROWS, A_WIDTH), F32),
            pltpu.VMEM((TILE_ROWS, A_WIDTH), BF16),
            pltpu.VMEM((TILE_ROWS, B_WIDTH), F32),
            pltpu.VMEM((TILE_ROWS, D), BF16),
        ],
        compiler_params=pltpu.CompilerParams(
            dimension_semantics=("arbitrary",), vmem_limit_bytes=VMEM_LIMIT),
        name=name,
    )(x, mod_l, w["w_in"], w["w_s"], w["b_s_t"], w["w_pool"], w["pool_scale"], w["w_out"],
      w["ln_g"], w["ln_b"])


def _ffn_kernel(x_ref, mod_ref, wg_ref, wv_ref, cwg_ref, cwv_ref, cbg_ref, cbv_ref, wd_ref,
                lng_ref, lnb_ref, o_ref, hb_ref, acc_ref, *, seq_len, row0, row_step):
    c = pl.program_id(1)
    T = x_ref.shape[0]
    shift, scale, gate = _mod_rows(mod_ref, row0, row_step)

    @pl.when(c == 0)
    def _():
        hb_ref[...] = (x_ref[...] * (1.0 + scale) + shift).astype(BF16)
        acc_ref[...] = jnp.zeros_like(acc_ref)

    hb = hb_ref[...]
    pos = _row_pos((T, FF_CHUNK), seq_len)

    def branch(w_ref, cw_ref, cb_ref):
        z = _dot(hb, w_ref[...])
        cw = cw_ref[...]
        return (cw[0:1, :] * _shift_rows(z, -1, pos, seq_len) + cw[1:2, :] * z
                + cw[2:3, :] * _shift_rows(z, 1, pos, seq_len) + cb_ref[...])

    g = branch(wg_ref, cwg_ref, cbg_ref)
    v = branch(wv_ref, cwv_ref, cbv_ref)
    a = (g * jax.nn.sigmoid(g) * v).astype(BF16)
    acc_ref[...] += _dot(a, wd_ref[...])

    @pl.when(c == pl.num_programs(1) - 1)
    def _():
        r = ALPHA * x_ref[...] + gate * acc_ref[...]
        o_ref[...] = _layernorm(r, lng_ref[...], lnb_ref[...])


def _ffn_layer(x, mod_l, w, seq_len, row0, row_step, name):
    n_rows = x.shape[0]
    n_chunks = D_FF // FF_CHUNK
    kern = functools.partial(_ffn_kernel, seq_len=seq_len, row0=row0, row_step=row_step)
    return pl.pallas_call(
        kern,
        grid=(n_rows // TILE_ROWS, n_chunks),
        in_specs=[
            pl.BlockSpec((TILE_ROWS, D), lambda i, c: (i, 0)),
            pl.BlockSpec((SUBLANES, 3 * D), lambda i, c: (0, 1)),
            pl.BlockSpec((D, FF_CHUNK), lambda i, c: (0, c)),
            pl.BlockSpec((D, FF_CHUNK), lambda i, c: (0, c + n_chunks)),
            pl.BlockSpec((3, FF_CHUNK), lambda i, c: (0, c)),
            pl.BlockSpec((3, FF_CHUNK), lambda i, c: (0, c + n_chunks)),
            pl.BlockSpec((1, FF_CHUNK), lambda i, c: (0, c)),
            pl.BlockSpec((1, FF_CHUNK), lambda i, c: (0, c + n_chunks)),
            pl.BlockSpec((FF_CHUNK, D), lambda i, c: (c, 0)),
            pl.BlockSpec((1, D), lambda i, c: (0, 0)),
            pl.BlockSpec((1, D), lambda i, c: (0, 0)),
        ],
        out_specs=pl.BlockSpec((TILE_ROWS, D), lambda i, c: (i, 0)),
        out_shape=jax.ShapeDtypeStruct((n_rows, D), F32),
        scratch_shapes=[pltpu.VMEM((TILE_ROWS, D), BF16), pltpu.VMEM((TILE_ROWS, D), F32)],
        compiler_params=pltpu.CompilerParams(
            dimension_semantics=("arbitrary", "arbitrary"), vmem_limit_bytes=VMEM_LIMIT),
        name=name,
    )(x, mod_l, w["w_up"], w["w_up"], w["conv_w"], w["conv_w"], w["conv_b"], w["conv_b"],
      w["w_down"], w["ln_g"], w["ln_b"])


def _scan_groups(a, b, sub, reverse):
    n = a.shape[0]
    for d in (1, 2, 4):
        if reverse:
            valid, sh = sub < SUBLANES - d, n - d
        else:
            valid, sh = sub >= d, d
        a_prev = jnp.where(valid, pltpu.roll(a, sh, 0), 1.0)
        b_prev = jnp.where(valid, pltpu.roll(b, sh, 0), 0.0)
        b = b + a * b_prev
        a = a * a_prev
    return a, b


def _scan_ends(a, b, posg, n_groups, reverse):
    n = a.shape[0]
    d = 1
    while d < n_groups:
        if reverse:
            valid, sh = posg < n_groups - d, n - d
        else:
            valid, sh = posg >= d, d
        a_prev = jnp.where(valid, pltpu.roll(a, sh, 0), 1.0)
        b_prev = jnp.where(valid, pltpu.roll(b, sh, 0), 0.0)
        b = b + a * b_prev
        a = a * a_prev
        d *= 2
    return a, b


def _rope_pairs(x, cos, sin_signed, even_lane):
    n = x.shape[1]
    swapped = jnp.where(even_lane, pltpu.roll(x, n - 1, 1), pltpu.roll(x, 1, 1))
    return x * cos + swapped * sin_signed


def _rmsnorm(x, g):
    return x * lax.rsqrt(jnp.mean(x * x, axis=-1, keepdims=True) + RMS_EPS) * g


def _odd_kernel(*refs, seq_len, latent, past_len, row0, row_step):
    if latent:
        (x_ref, mod_ref, win_ref, cw_ref, cb_ref, wgate_ref, bgate_ref, lam_ref, qg_ref, kvg_ref,
         wuq_ref, wuk_ref, wuv_ref, wout_ref, lng_ref, lnb_ref,
         cckv_ref, ckr_ref, st_ref, cos_ref, sin_ref,
         o_ref,
         zrg_ref, xc_ref, a_ref, b_ref, hrep_ref, q_ref, k_ref, v_ref, ycat_ref) = refs
    else:
        (x_ref, mod_ref, win_ref, cw_ref, cb_ref, wgate_ref, bgate_ref, lam_ref, qg_ref, kvg_ref,
         wuq_ref, wuk_ref, wuv_ref, wout_ref, lng_ref, lnb_ref,
         o_ref, ckv_out_ref, kr_out_ref, st_out_ref,
         zrg_ref, xc_ref, a_ref, b_ref, hrep_ref, q_ref, k_ref, v_ref, ycat_ref) = refs
    T = x_ref.shape[0]
    n_seq = T // seq_len
    shift, scale, gate = _mod_rows(mod_ref, row0, row_step)
    o_q = 2 * W_C
    o_kv = o_q + Q_LORA
    o_kr = o_kv + KV_LORA
    pair_w = 2 * HEAD_PAD
    even_lane = (lax.broadcasted_iota(jnp.int32, (ROW_BLOCK, HEAD_PAD), 1) & 1) == 0

    def row_block(rb):
        return pl.ds(pl.multiple_of(rb * ROW_BLOCK, ROW_BLOCK), ROW_BLOCK)

    def proj(rb, carry):
        rows = row_block(rb)
        k_rows = pl.ds(pl.multiple_of(past_len + rb * ROW_BLOCK, ROW_BLOCK), ROW_BLOCK)
        hb = (x_ref[rows, :] * (1.0 + scale) + shift).astype(BF16)
        zrg_ref[rows, :] = _dot(hb, win_ref[:, 0:o_q])
        qn = _rmsnorm(_dot(hb, win_ref[:, o_q:o_kv]), qg_ref[...]).astype(BF16)
        ckv = _rmsnorm(_dot(hb, win_ref[:, o_kv:o_kr]), kvg_ref[...])
        kr = _dot(hb, win_ref[:, o_kr:o_kr + HEAD_PAD])
        if latent:
            cos = cos_ref[rows, :]
            sin = sin_ref[rows, :]
            kr = _rope_pairs(kr, cos, sin, even_lane)
        else:
            ckv_out_ref[rows, :] = ckv
            kr_out_ref[rows, :] = kr[:, KR_LANE:KR_LANE + QK_ROPE]
        ckvb = ckv.astype(BF16)
        v_ref[k_rows, :] = _dot(ckvb, wuv_ref[...]).astype(BF16)
        for pair in range(D_HEADS // 2):
            pcols = slice(pair * pair_w, (pair + 1) * pair_w)
            q2 = _dot(qn, wuq_ref[:, pcols])
            k2 = _dot(ckvb, wuk_ref[:, pcols])
            for hh in range(2):
                cols = slice(pair * pair_w + hh * HEAD_PAD, pair * pair_w + (hh + 1) * HEAD_PAD)
                qh = q2[:, hh * HEAD_PAD:(hh + 1) * HEAD_PAD]
                if latent:
                    qh = _rope_pairs(qh, cos, sin, even_lane)
                q_ref[rows, cols] = qh.astype(BF16)
                k_ref[k_rows, cols] = (k2[:, hh * HEAD_PAD:(hh + 1) * HEAD_PAD] + kr).astype(BF16)
        return carry

    lax.fori_loop(0, T // ROW_BLOCK, proj, 0)

    if latent:
        def cached(cbk, carry):
            rows = row_block(cbk)
            cckv = cckv_ref[rows, :].astype(BF16)
            ckr = ckr_ref[rows, :]
            v_ref[rows, :] = _dot(cckv, wuv_ref[...]).astype(BF16)
            for pair in range(D_HEADS // 2):
                pcols = slice(pair * pair_w, (pair + 1) * pair_w)
                k2 = _dot(cckv, wuk_ref[:, pcols])
                for hh in range(2):
                    cols = slice(pair * pair_w + hh * HEAD_PAD, pair * pair_w + (hh + 1) * HEAD_PAD)
                    k_ref[rows, cols] = (k2[:, hh * HEAD_PAD:(hh + 1) * HEAD_PAD] + ckr).astype(BF16)
            return carry

        lax.fori_loop(0, past_len // ROW_BLOCK, cached, 0)

    pos_l = _row_pos((T, LANES), seq_len)
    cw = cw_ref[...]
    for cbk in range(W_C // LANES):
        cols = slice(cbk * LANES, (cbk + 1) * LANES)
        xr = zrg_ref[:, cols]
        xc_ref[:, cols] = (cw[0:1, cols] * _shift_rows(xr, -1, pos_l, seq_len) + cw[1:2, cols] * xr
                           + cw[2:3, cols] * _shift_rows(xr, 1, pos_l, seq_len)
                           + cw[3:4, cols] * _shift_rows(xr, 2, pos_l, seq_len) + cb_ref[:, cols])

    half_w = W_C // 2
    n_groups = seq_len // SUBLANES
    n_ends = T // SUBLANES
    sub = lax.broadcasted_iota(jnp.int32, (T, LANES), 0) & (SUBLANES - 1)
    posg = lax.broadcasted_iota(jnp.int32, (n_ends, LANES), 0) & (n_groups - 1)
    for direction in range(2):
        reverse = direction == 1
        neg = -lam_ref[direction:direction + 1, :]
        softplus = jnp.maximum(neg, 0.0) + jnp.log1p(jnp.exp(-jnp.abs(neg)))

        def gates(rb, carry, direction=direction, softplus=softplus):
            rows = row_block(rb)
            for j in range(2):
                cols = slice(j * half_w, (j + 1) * half_w)
                xcj = xc_ref[rows, cols]
                xcb = xcj.astype(BF16)
                r = jax.nn.sigmoid(_dot(xcb, wgate_ref[direction, 0, j]) + bgate_ref[direction, 0:1, cols])
                i = jax.nn.sigmoid(_dot(xcb, wgate_ref[direction, 1, j]) + bgate_ref[direction, 1:2, cols])
                log_a = -RG_C * r * softplus[:, cols]
                a = jnp.exp(log_a)
                b = jnp.sqrt(jnp.tanh(-log_a) * (a * a + 1.0)) * i * xcj
                for k in range(half_w // LANES):
                    cbk = j * (half_w // LANES) + k
                    a_ref[cbk, rows, :] = a[:, k * LANES:(k + 1) * LANES]
                    b_ref[cbk, rows, :] = b[:, k * LANES:(k + 1) * LANES]
            return carry

        lax.fori_loop(0, T // ROW_BLOCK, gates, 0)

        end_row = 0 if reverse else SUBLANES - 1
        for cbk in range(W_C // LANES):
            cols = slice(cbk * LANES, (cbk + 1) * LANES)
            a_loc, b_loc = _scan_groups(a_ref[cbk], b_ref[cbk], sub, reverse)
            a_ref[cbk] = a_loc
            b_ref[cbk] = b_loc
            a_end = a_ref[cbk, pl.ds(end_row, n_ends, stride=SUBLANES), :]
            b_end = b_ref[cbk, pl.ds(end_row, n_ends, stride=SUBLANES), :]
            a_tot, b_tot = _scan_ends(a_end, b_end, posg, n_groups, reverse)
            if latent:
                h0 = st_ref[direction:direction + 1, cols]
                ends = a_tot * h0 + b_tot
            else:
                h0 = jnp.zeros((1, LANES), F32)
                ends = b_tot
            if reverse:
                carry_in = jnp.where(posg < n_groups - 1, pltpu.roll(ends, n_ends - 1, 0), h0)
            else:
                carry_in = jnp.where(posg >= 1, pltpu.roll(ends, 1, 0), h0)
            for rr in range(SUBLANES):
                hrep_ref[pl.ds(rr, n_ends, stride=SUBLANES), :] = carry_in
            h = a_ref[cbk] * hrep_ref[...] + b_ref[cbk]
            if reverse:
                zrg_ref[:, cols] = zrg_ref[:, cols] + h
            else:
                zrg_ref[:, cols] = h
            if not latent:
                for s in range(n_seq):
                    g_last = s * n_groups + (0 if reverse else n_groups - 1)
                    st_out_ref[s, direction:direction + 1, cols] = ends[g_last:g_last + 1, :]

    def mix(rb, carry):
        rows = row_block(rb)
        ycat_ref[rows, 0:W_C] = (zrg_ref[rows, 0:W_C] * _gelu(zrg_ref[rows, W_C:2 * W_C])).astype(BF16)
        return carry

    lax.fori_loop(0, T // ROW_BLOCK, mix, 0)

    n_keys = past_len + seq_len

    def attend(qb, carry):
        q_rows = row_block(qb)
        if latent:
            k_rows = slice(0, n_keys)
        else:
            k_rows = pl.ds(pl.multiple_of((qb * ROW_BLOCK // seq_len) * seq_len, seq_len), seq_len)
        for pair in range(D_HEADS // 2):
            o_pair = None
            for h in (2 * pair, 2 * pair + 1):
                cols = slice(h * HEAD_PAD, (h + 1) * HEAD_PAD)
                s = lax.dot_general(q_ref[q_rows, cols], k_ref[k_rows, cols],
                                    (((1,), (1,)), ((), ())), preferred_element_type=F32) * ATTN_SCALE
                m = jnp.max(s, axis=-1, keepdims=True)
                p = jnp.exp(s - m)
                l = jnp.sum(p, axis=-1, keepdims=True)
                o = _dot(p.astype(BF16), v_ref[k_rows, cols]) / l
                o_pair = o if o_pair is None else o_pair + o
            ycat_ref[q_rows, W_C + pair * HEAD_PAD:W_C + (pair + 1) * HEAD_PAD] = o_pair.astype(BF16)
        return carry

    lax.fori_loop(0, T // ROW_BLOCK, attend, 0)

    def out(rb, carry):
        rows = row_block(rb)
        d = _dot(ycat_ref[rows, :], wout_ref[...])
        r = ALPHA * x_ref[rows, :] + gate * d
        o_ref[rows, :] = _layernorm(r, lng_ref[...], lnb_ref[...])
        return carry

    lax.fori_loop(0, T // ROW_BLOCK, out, 0)


def _odd_layer(x, mod_l, w, seq_len, row0, row_step, name, latent_inputs=None):
    n_rows = x.shape[0]
    n_tiles = n_rows // TILE_ROWS
    latent = latent_inputs is not None
    past_len = latent_inputs["cache_ckv"].shape[1] if latent else 0
    n_seq = TILE_ROWS // seq_len
    kern = functools.partial(_odd_kernel, seq_len=seq_len, latent=latent, past_len=past_len,
                             row0=row0, row_step=row_step)
    weights = [w["w_in"], w["conv_w"], w["conv_b"], w["w_gate"], w["b_gate"], w["lam"], w["q_g"], w["kv_g"],
               w["w_uq"], w["w_uk"], w["w_uv"], w["w_out"], w["ln_g"], w["ln_b"]]
    in_specs = [pl.BlockSpec((TILE_ROWS, D), lambda i: (i, 0), pipeline_mode=pl.Buffered(1)),
                pl.BlockSpec((SUBLANES, 3 * D), lambda i: (0, 0))]
    in_specs += [_const_spec(a.shape) for a in weights]
    args = [x, mod_l] + weights
    out_specs = [pl.BlockSpec((TILE_ROWS, D), lambda i: (i, 0), pipeline_mode=pl.Buffered(1))]
    out_shape = [jax.ShapeDtypeStruct((n_rows, D), F32)]
    if latent:
        li = latent_inputs
        in_specs += [
            pl.BlockSpec((None, past_len, KV_LORA), lambda i: (i, 0, 0)),
            pl.BlockSpec((None, past_len, HEAD_PAD), lambda i: (i, 0, 0)),
            pl.BlockSpec((None, 2, W_C), lambda i: (i, 0, 0)),
            _const_spec(li["cos"].shape),
            _const_spec(li["sin"].shape),
        ]
        args += [li["cache_ckv"], li["cache_kr"], li["state"], li["cos"], li["sin"]]
    else:
        out_specs += [
            pl.BlockSpec((TILE_ROWS, KV_LORA), lambda i: (i, 0)),
            pl.BlockSpec((TILE_ROWS, QK_ROPE), lambda i: (i, 0)),
            pl.BlockSpec((n_seq, 2, W_C), lambda i: (i, 0, 0)),
        ]
        out_shape += [
            jax.ShapeDtypeStruct((n_rows, KV_LORA), F32),
            jax.ShapeDtypeStruct((n_rows, QK_ROPE), F32),
            jax.ShapeDtypeStruct((n_rows // seq_len, 2, W_C), F32),
        ]
    n_keys_buf = past_len + TILE_ROWS
    scratch = [
        pltpu.VMEM((TILE_ROWS, 2 * W_C), F32),
        pltpu.VMEM((TILE_ROWS, W_C), F32),
        pltpu.VMEM((W_C // LANES, TILE_ROWS, LANES), F32),
        pltpu.VMEM((W_C // LANES, TILE_ROWS, LANES), F32),
        pltpu.VMEM((TILE_ROWS, LANES), F32),
        pltpu.VMEM((TILE_ROWS, D_HEADS * HEAD_PAD), BF16),
        pltpu.VMEM((n_keys_buf, D_HEADS * HEAD_PAD), BF16),
        pltpu.VMEM((n_keys_buf, D_HEADS * HEAD_PAD), BF16),
        pltpu.VMEM((TILE_ROWS, D), BF16),
    ]
    return pl.pallas_call(
        kern,
        grid=(n_tiles,),
        in_specs=in_specs,
        out_specs=out_specs,
        out_shape=out_shape,
        scratch_shapes=scratch,
        compiler_params=pltpu.CompilerParams(
            dimension_semantics=("arbitrary",), vmem_limit_bytes=VMEM_LIMIT),
        name=name,
    )(*args)


def _axial_rope_tables(rows):
    half = QK_ROPE // 2
    inv = ROPE_BASE ** (-jnp.arange(0, half, 2, dtype=F32) / half)
    r = jnp.repeat(jnp.arange(rows, dtype=F32), GRID_W)
    col = jnp.tile(jnp.arange(GRID_W, dtype=F32), rows)
    ang = jnp.concatenate([r[:, None] * inv, col[:, None] * inv], axis=-1)
    cos = jnp.repeat(jnp.cos(ang), 2, axis=-1)
    sin = jnp.repeat(jnp.sin(ang), 2, axis=-1) * jnp.tile(jnp.array([-1.0, 1.0], F32), half)
    n = ang.shape[0]
    pad_lo, pad_hi = KR_LANE, HEAD_PAD - KR_LANE - QK_ROPE
    cos_t = jnp.concatenate([jnp.ones((n, pad_lo), F32), cos, jnp.ones((n, pad_hi), F32)], axis=-1)
    sin_t = jnp.concatenate([jnp.zeros((n, pad_lo), F32), sin, jnp.zeros((n, pad_hi), F32)], axis=-1)
    return cos_t, sin_t


def _block_diag_gates(w):
    per = (W_C // 2) // C_BLOCK
    w4 = w.reshape(2, 2, per, C_BLOCK, C_BLOCK)
    eye = jnp.eye(per, dtype=w.dtype)
    bd = jnp.einsum("dJhij,hk->dJhikj", w4, eye)
    return bd.reshape(2, 2, per * C_BLOCK, per * C_BLOCK)


def kernel(x_prompt, x_sample, cache_mla_ckv, cache_mla_krope, state_rglru, c, c_ctx, w_mod, b_mod, ln1_g, ln1_b, ln2_g, ln2_b, even_w_in, even_w_s, even_b_s, even_w_pool, even_pool_scale, even_w_out, odd_w_in, rg_conv_w, rg_conv_b, rg_w_a, rg_b_a, rg_w_x, rg_b_x, rg_lam, mla_q_g, mla_kv_g, mla_w_uq, mla_w_uk, mla_w_uv, odd_w_out, ffn_w_up, ffn_conv_w, ffn_conv_b, ffn_w_down):
    batch, seq, _ = x_prompt.shape
    dec_batch, dec_seq, _ = x_sample.shape
    assert TILE_ROWS % seq == 0 and dec_seq == TILE_ROWS and dec_batch + 1 <= SUBLANES

    cond = jnp.concatenate([c_ctx[None, :], c, jnp.zeros((SUBLANES - 1 - dec_batch, D), F32)], axis=0)
    mod = _modulation(cond, w_mod, b_mod)

    xp = x_prompt.reshape(batch * seq, D)
    xs = x_sample.reshape(dec_batch * dec_seq, D)
    new_ckv = new_kr = new_state = None
    for l in range(DEPTH):
        ln1 = {"ln_g": ln1_g[l][None, :], "ln_b": ln1_b[l][None, :]}
        if l % 2 == 0:
            e = l // 2
            w = dict(ln1,
                     w_in=even_w_in[e].astype(BF16), w_s=even_w_s[e].astype(BF16),
                     b_s_t=even_b_s[e].T, w_pool=even_w_pool[e].astype(BF16),
                     pool_scale=even_pool_scale[e][None, :], w_out=even_w_out[e].astype(BF16))
            xp = _even_layer(xp, mod[l], w, seq, 0, 0, f"even{l}_ctx")
            xs = _even_layer(xs, mod[l], w, dec_seq, 1, 1, f"even{l}_lat")
        else:
            o = l // 2
            w_in = odd_w_in[o]
            n_main = 2 * W_C + Q_LORA + KV_LORA
            w_in_pad = jnp.concatenate(
                [w_in[:, :n_main], jnp.zeros((D, KR_LANE), F32), w_in[:, n_main:],
                 jnp.zeros((D, HEAD_PAD - KR_LANE - QK_ROPE), F32)], axis=1).astype(BF16)
            w_gate = jnp.stack([_block_diag_gates(rg_w_a[o]), _block_diag_gates(rg_w_x[o])], axis=1)
            b_gate = jnp.stack([rg_b_a[o], rg_b_x[o]], axis=1)
            w_uq = jnp.pad(mla_w_uq[o], ((0, 0), (0, 0), (0, HEAD_PAD - QK_NOPE - QK_ROPE)))
            w_uk = jnp.pad(mla_w_uk[o], ((0, 0), (0, 0), (0, HEAD_PAD - QK_NOPE)))
            uv = mla_w_uv[o].reshape(KV_LORA, D_HEADS // 2, 2, V_DIM)
            zeros_v = jnp.zeros((KV_LORA, D_HEADS // 2, V_DIM), F32)
            w_uv = jnp.stack([jnp.concatenate([uv[:, :, 0], zeros_v], axis=-1),
                              jnp.concatenate([zeros_v, uv[:, :, 1]], axis=-1)], axis=2)
            w = dict(ln1,
                     w_in=w_in_pad, conv_w=rg_conv_w[o], conv_b=rg_conv_b[o][None, :],
                     w_gate=w_gate.astype(BF16), b_gate=b_gate, lam=rg_lam[o],
                     q_g=mla_q_g[o][None, :], kv_g=mla_kv_g[o][None, :],
                     w_uq=w_uq.reshape(Q_LORA, D_HEADS * HEAD_PAD).astype(BF16),
                     w_uk=w_uk.reshape(KV_LORA, D_HEADS * HEAD_PAD).astype(BF16),
                     w_uv=w_uv.reshape(KV_LORA, D_HEADS * HEAD_PAD).astype(BF16),
                     w_out=odd_w_out[o].astype(BF16))
            xp, ckv, kr, st = _odd_layer(xp, mod[l], w, seq, 0, 0, f"odd{l}_ctx")
            new_ckv = ckv.reshape(batch, 1, seq, KV_LORA)
            new_kr = kr.reshape(batch, 1, seq, QK_ROPE)
            new_state = st.reshape(batch, 1, 2, W_C)
            cos_t, sin_t = _axial_rope_tables(dec_seq // GRID_W)
            latent_inputs = {
                "cache_ckv": cache_mla_ckv[:, o],
                "cache_kr": jnp.pad(cache_mla_krope[:, o],
                                    ((0, 0), (0, 0), (KR_LANE, HEAD_PAD - KR_LANE - QK_ROPE))),
                "state": state_rglru[:, o],
                "cos": cos_t, "sin": sin_t,
            }
            (xs,) = _odd_layer(xs, mod[l], w, dec_seq, 1, 1, f"odd{l}_lat", latent_inputs)
        wf = {"ln_g": ln2_g[l][None, :], "ln_b": ln2_b[l][None, :],
              "w_up": ffn_w_up[l].astype(BF16), "conv_w": ffn_conv_w[l],
              "conv_b": ffn_conv_b[l][None, :], "w_down": ffn_w_down[l].astype(BF16)}
        xp = _ffn_layer(xp, mod[l], wf, seq, 0, 0, f"ffn{l}_ctx")
        xs = _ffn_layer(xs, mod[l], wf, dec_seq, 1, 1, f"ffn{l}_lat")
    return (xp.reshape(batch, seq, D), xs.reshape(dec_batch, dec_seq, D), new_ckv, new_kr, new_state)
```

```python
import functools
import math

import jax
import jax.numpy as jnp
from jax import lax
from jax.experimental import pallas as pl
from jax.experimental.pallas import tpu as pltpu

F32 = jnp.float32
BF16 = jnp.bfloat16

D = 1024
DEPTH = 2
GRID_W = 64
ALPHA = (2 * DEPTH) ** 0.25
LN_EPS = 1e-5
RMS_EPS = 1e-6
A_HEADS = 4
A_WIDTH = 512
A_GROUP_W = A_WIDTH // A_HEADS
CHUNK = 128
POOL_WINDOWS = (2, 4, 8, 16)
B_GROUP = 128
B_WIDTH = 512
W_C = 512
C_HEADS = 8
C_BLOCK = 64
RG_C = 8.0
D_HEADS = 8
Q_LORA = 384
KV_LORA = 256
QK_NOPE = 64
QK_ROPE = 32
V_DIM = 64
ROPE_BASE = 10000.0
ATTN_SCALE = 1.0 / math.sqrt(QK_NOPE + QK_ROPE)
D_FF = 2816

LANES = 128
SUBLANES = 8
HEAD_PAD = LANES
ODD_IN_PAD = 2 * W_C + Q_LORA + KV_LORA + HEAD_PAD
KR_LANE = QK_NOPE
TILE_ROWS = 1024
ROW_BLOCK = 256
FF_CHUNK = 256
FFN_DOT_ROWS = 128
FFN_RB = 64
VMEM_LIMIT = 56 * 1024 * 1024


def _dot(a, b):
    return jnp.dot(a, b, preferred_element_type=F32)


def _gelu(x):
    return x * (0.5 * (1.0 + jnp.tanh(0.7978845608028654 * (x + 0.044715 * (x * x * x)))))


def _layernorm(r, g, b):
    mu = jnp.mean(r, axis=-1, keepdims=True)
    c = r - mu
    var = jnp.mean(c * c, axis=-1, keepdims=True)
    return c * lax.rsqrt(var + LN_EPS) * g + b


def _row_pos(shape, seq_len):
    return lax.broadcasted_iota(jnp.int32, shape, 0) & (seq_len - 1)


def _shift_rows(x, d, pos, seq_len):
    n = x.shape[0]
    y = pltpu.roll(x, (-d) % n, 0)
    valid = (pos < seq_len - d) if d > 0 else (pos >= -d)
    return jnp.where(valid, y, 0.0)


def _mod_kernel(s_ref, w_ref, b_ref, o_ref):
    s = s_ref[...]
    s = s * jax.nn.sigmoid(s)
    o_ref[...] = _dot(s.astype(BF16), w_ref[...].astype(BF16)) + b_ref[...]


def _modulation(cond, w_mod, b_mod):
    nc = 1536
    return pl.pallas_call(
        _mod_kernel,
        grid=(DEPTH, 6 * D // nc),
        in_specs=[
            pl.BlockSpec((SUBLANES, D), lambda l, j: (0, 0)),
            pl.BlockSpec((None, D, nc), lambda l, j: (l, 0, j)),
            pl.BlockSpec((None, 1, nc), lambda l, j: (l, 0, j)),
        ],
        out_specs=pl.BlockSpec((None, SUBLANES, nc), lambda l, j: (l, 0, j)),
        out_shape=jax.ShapeDtypeStruct((DEPTH, SUBLANES, 6 * D), F32),
        compiler_params=pltpu.CompilerParams(
            dimension_semantics=("arbitrary", "arbitrary"), vmem_limit_bytes=VMEM_LIMIT),
        name="modulation",
    )(cond, w_mod, b_mod.reshape(DEPTH, 1, 6 * D))


def _mod_rows(mod_ref, row0, row_step):
    row = row0 + pl.program_id(0) * row_step
    m = mod_ref[pl.ds(row, 1), :]
    return m[:, 0:D], m[:, D:2 * D], m[:, 2 * D:3 * D]


def _const_spec(shape):
    nd = len(shape)
    return pl.BlockSpec(shape, lambda *_: (0,) * nd, pipeline_mode=pl.Buffered(1))


def _even_kernel(x_ref, mod_ref, win_ref, ws_ref, bst_ref, wpool_ref, pscale_ref, wout_ref,
                 lng_ref, lnb_ref, o_ref, u_ref, vn_ref, p_ref, ycat_ref, *, seq_len, row0, row_step):
    T = x_ref.shape[0]
    shift, scale, gate = _mod_rows(mod_ref, row0, row_step)

    def proj(rb, carry):
        rows = pl.ds(pl.multiple_of(rb * ROW_BLOCK, ROW_BLOCK), ROW_BLOCK)
        hb = (x_ref[rows, :] * (1.0 + scale) + shift).astype(BF16)
        u_ref[rows, :] = _gelu(_dot(hb, win_ref[:, 0:A_WIDTH]))
        v = _gelu(_dot(hb, win_ref[:, A_WIDTH:2 * A_WIDTH]))
        mu = jnp.mean(v, axis=-1, keepdims=True)
        c = v - mu
        var = jnp.mean(c * c, axis=-1, keepdims=True)
        vn_ref[rows, :] = (c * lax.rsqrt(var + LN_EPS)).astype(BF16)
        p_ref[rows, :] = _dot(hb, win_ref[:, 2 * A_WIDTH:])
        return carry

    lax.fori_loop(0, T // ROW_BLOCK, proj, 0)

    def gate_chunk(n, carry):
        rows = pl.ds(pl.multiple_of(n * CHUNK, CHUNK), CHUNK)
        for h in range(A_HEADS):
            cols = slice(h * A_GROUP_W, (h + 1) * A_GROUP_W)
            s = _dot(ws_ref[h], vn_ref[rows, cols]) + bst_ref[:, h:h + 1]
            ycat_ref[rows, cols] = (u_ref[rows, cols] * s).astype(BF16)
        return carry

    lax.fori_loop(0, T // CHUNK, gate_chunk, 0)

    pos = _row_pos((T, B_GROUP), seq_len)
    for gi, w in enumerate(POOL_WINDOWS):
        cols = slice(gi * B_GROUP, (gi + 1) * B_GROUP)
        half = w // 2
        pg = p_ref[:, cols]
        fwd = pg
        bwd = _shift_rows(pg, -1, pos, seq_len)
        k = 1
        while k < half:
            fwd = fwd + _shift_rows(fwd, k, pos, seq_len)
            bwd = bwd + _shift_rows(bwd, -k, pos, seq_len)
            k *= 2
        cnt = jnp.minimum(pos + half, seq_len) - jnp.maximum(pos - half, 0)
        pooled = (fwd + bwd) / cnt.astype(F32) - pg
        yb = _dot(pooled.astype(BF16), wpool_ref[gi]) * pscale_ref[:, cols]
        ycat_ref[:, A_WIDTH + gi * B_GROUP:A_WIDTH + (gi + 1) * B_GROUP] = yb.astype(BF16)

    def out(rb, carry):
        rows = pl.ds(pl.multiple_of(rb * ROW_BLOCK, ROW_BLOCK), ROW_BLOCK)
        d = _dot(ycat_ref[rows, :], wout_ref[...])
        r = ALPHA * x_ref[rows, :] + gate * d
        o_ref[rows, :] = _layernorm(r, lng_ref[...], lnb_ref[...])
        return carry

    lax.fori_loop(0, T // ROW_BLOCK, out, 0)


def _even_layer(x, mod_l, w, seq_len, row0, row_step, name):
    n_rows = x.shape[0]
    kern = functools.partial(_even_kernel, seq_len=seq_len, row0=row0, row_step=row_step)
    return pl.pallas_call(
        kern,
        grid=(n_rows // TILE_ROWS,),
        in_specs=[
            pl.BlockSpec((TILE_ROWS, D), lambda i: (i, 0)),
            pl.BlockSpec((SUBLANES, 3 * D), lambda i: (0, 0)),
            _const_spec(w["w_in"].shape),
            _const_spec(w["w_s"].shape),
            _const_spec(w["b_s_t"].shape),
            _const_spec(w["w_pool"].shape),
            _const_spec(w["pool_scale"].shape),
            _const_spec(w["w_out"].shape),
            _const_spec((1, D)),
            _const_spec((1, D)),
        ],
        out_specs=pl.BlockSpec((TILE_ROWS, D), lambda i: (i, 0)),
        out_shape=jax.ShapeDtypeStruct((n_rows, D), F32),
        scratch_shapes=[
            pltpu.VMEM((TILE_ROWS, A_WIDTH), F32),
            pltpu.VMEM((TILE_ROWS, A_WIDTH), BF16),
            pltpu.VMEM((TILE_ROWS, B_WIDTH), F32),
            pltpu.VMEM((TILE_ROWS, D), BF16),
        ],
        compiler_params=pltpu.CompilerParams(
            dimension_semantics=("arbitrary",), vmem_limit_bytes=VMEM_LIMIT),
        name=name,
    )(x, mod_l, w["w_in"], w["w_s"], w["b_s_t"], w["w_pool"], w["pool_scale"], w["w_out"],
      w["ln_g"], w["ln_b"])


def _conv3_block(z_ref, r0, lanes, cw, cb, seq_len):
    nv = FFN_RB // SUBLANES
    z3 = z_ref[r0:r0 + FFN_RB, lanes].reshape(nv, SUBLANES, LANES)
    sub = lax.broadcasted_iota(jnp.int32, (nv, SUBLANES, LANES), 1)
    down = pltpu.roll(z3, 1, 1)
    up = pltpu.roll(z3, SUBLANES - 1, 1)
    if r0 % seq_len == 0:
        prev = jnp.zeros((1, SUBLANES, LANES), F32)
    else:
        prev = pltpu.roll(z_ref[r0 - SUBLANES:r0, lanes], 1, 0)[None]
    if (r0 + FFN_RB) % seq_len == 0:
        nxt = jnp.zeros((1, SUBLANES, LANES), F32)
    else:
        nxt = pltpu.roll(z_ref[r0 + FFN_RB:r0 + FFN_RB + SUBLANES, lanes], SUBLANES - 1, 0)[None]
    zm = jnp.where(sub == 0, jnp.concatenate([prev, down[:-1]], axis=0), down)
    zp = jnp.where(sub == SUBLANES - 1, jnp.concatenate([up[1:], nxt], axis=0), up)
    y = cw[0:1, lanes] * zm + cw[1:2, lanes] * z3 + cw[2:3, lanes] * zp + cb[:, lanes]
    return y.reshape(FFN_RB, LANES)


def _ffn_kernel(x_ref, mod_ref, wup_ref, cw_ref, cb_ref, wd_ref, lng_ref, lnb_ref, o_ref,
                hb_ref, zg0_ref, zv0_ref, zg1_ref, zv1_ref, a_ref, *, seq_len, row0, row_step):
    T = x_ref.shape[0]
    n_chunks = wd_ref.shape[0]
    shift, scale, gate = _mod_rows(mod_ref, row0, row_step)
    z_bufs = ((zg0_ref, zv0_ref), (zg1_ref, zv1_ref))

    def row_block(rb):
        return pl.ds(pl.multiple_of(rb * ROW_BLOCK, ROW_BLOCK), ROW_BLOCK)

    def modulate(rb, carry):
        rows = row_block(rb)
        hb_ref[rows, :] = (x_ref[rows, :] * (1.0 + scale) + shift).astype(BF16)
        return carry

    lax.fori_loop(0, T // ROW_BLOCK, modulate, 0)

    n_parts = T // FFN_DOT_ROWS
    blocks_per_part = FFN_DOT_ROWS // FFN_RB

    def up_proj(c, bufs, q):
        rows = slice(q * FFN_DOT_ROWS, (q + 1) * FFN_DOT_ROWS)
        bufs[0][rows, :] = _dot(hb_ref[rows, :], wup_ref[c])
        bufs[1][rows, :] = _dot(hb_ref[rows, :], wup_ref[c + n_chunks])

    def gate_blocks(c, bufs, q):
        cwg, cwv = cw_ref[c], cw_ref[c + n_chunks]
        cbg, cbv = cb_ref[c], cb_ref[c + n_chunks]
        for rb in range(q * blocks_per_part, (q + 1) * blocks_per_part):
            r0 = rb * FFN_RB
            for lt in range(FF_CHUNK // LANES):
                lanes = slice(lt * LANES, (lt + 1) * LANES)
                g = _conv3_block(bufs[0], r0, lanes, cwg, cbg, seq_len)
                v = _conv3_block(bufs[1], r0, lanes, cwv, cbv, seq_len)
                a_ref[c, r0:r0 + FFN_RB, lanes] = (g * jax.nn.sigmoid(g) * v).astype(BF16)

    for q in range(n_parts):
        up_proj(0, z_bufs[0], q)

    def chunk_pair(k, carry):
        c = 2 * k
        for q in range(n_parts):
            up_proj(c + 1, z_bufs[1], q)
            gate_blocks(c, z_bufs[0], q)
        for q in range(n_parts):
            up_proj(c + 2, z_bufs[0], q)
            gate_blocks(c + 1, z_bufs[1], q)
        return carry

    assert n_chunks % 2 == 1
    lax.fori_loop(0, n_chunks // 2, chunk_pair, 0)
    for q in range(n_parts):
        gate_blocks(n_chunks - 1, z_bufs[0], q)

    def out(rb, carry):
        rows = row_block(rb)
        acc = _dot(a_ref[0, rows, :], wd_ref[0])
        for c in range(1, n_chunks):
            acc = acc + _dot(a_ref[c, rows, :], wd_ref[c])
        r = ALPHA * x_ref[rows, :] + gate * acc
        o_ref[rows, :] = _layernorm(r, lng_ref[...], lnb_ref[...])
        return carry

    lax.fori_loop(0, T // ROW_BLOCK, out, 0)


def _ffn_layer(x, mod_l, w, seq_len, row0, row_step, name):
    n_rows = x.shape[0]
    n_chunks = D_FF // FF_CHUNK
    kern = functools.partial(_ffn_kernel, seq_len=seq_len, row0=row0, row_step=row_step)
    return pl.pallas_call(
        kern,
        grid=(n_rows // TILE_ROWS,),
        in_specs=[
            pl.BlockSpec((TILE_ROWS, D), lambda i: (i, 0)),
            pl.BlockSpec((SUBLANES, 3 * D), lambda i: (0, 1)),
            _const_spec(w["w_up"].shape),
            _const_spec(w["conv_w"].shape),
            _const_spec(w["conv_b"].shape),
            _const_spec(w["w_down"].shape),
            _const_spec((1, D)),
            _const_spec((1, D)),
        ],
        out_specs=pl.BlockSpec((TILE_ROWS, D), lambda i: (i, 0)),
        out_shape=jax.ShapeDtypeStruct((n_rows, D), F32),
        scratch_shapes=[
            pltpu.VMEM((TILE_ROWS, D), BF16),
            pltpu.VMEM((TILE_ROWS, FF_CHUNK), F32),
            pltpu.VMEM((TILE_ROWS, FF_CHUNK), F32),
            pltpu.VMEM((TILE_ROWS, FF_CHUNK), F32),
            pltpu.VMEM((TILE_ROWS, FF_CHUNK), F32),
            pltpu.VMEM((n_chunks, TILE_ROWS, FF_CHUNK), BF16),
        ],
        compiler_params=pltpu.CompilerParams(
            dimension_semantics=("arbitrary",), vmem_limit_bytes=VMEM_LIMIT),
        name=name,
    )(x, mod_l, w["w_up"], w["conv_w"], w["conv_b"], w["w_down"], w["ln_g"], w["ln_b"])


def _scan_groups(a, b, sub, reverse):
    n = a.shape[0]
    for d in (1, 2, 4):
        if reverse:
            valid, sh = sub < SUBLANES - d, n - d
        else:
            valid, sh = sub >= d, d
        a_prev = jnp.where(valid, pltpu.roll(a, sh, 0), 1.0)
        b_prev = jnp.where(valid, pltpu.roll(b, sh, 0), 0.0)
        b = b + a * b_prev
        a = a * a_prev
    return a, b


def _scan_ends(a, b, posg, n_groups, reverse):
    n = a.shape[0]
    d = 1
    while d < n_groups:
        if reverse:
            valid, sh = posg < n_groups - d, n - d
        else:
            valid, sh = posg >= d, d
        a_prev = jnp.where(valid, pltpu.roll(a, sh, 0), 1.0)
        b_prev = jnp.where(valid, pltpu.roll(b, sh, 0), 0.0)
        b = b + a * b_prev
        a = a * a_prev
        d *= 2
    return a, b


def _rope_pairs(x, cos, sin_signed, even_lane):
    n = x.shape[1]
    swapped = jnp.where(even_lane, pltpu.roll(x, n - 1, 1), pltpu.roll(x, 1, 1))
    return x * cos + swapped * sin_signed


def _rmsnorm(x, g):
    return x * lax.rsqrt(jnp.mean(x * x, axis=-1, keepdims=True) + RMS_EPS) * g


def _odd_kernel(*refs, seq_len, latent, past_len, row0, row_step):
    if latent:
        (x_ref, mod_ref, win_ref, cw_ref, cb_ref, wgate_ref, bgate_ref, lam_ref, qg_ref, kvg_ref,
         wuq_ref, wuk_ref, wuv_ref, wout_ref, lng_ref, lnb_ref,
         cckv_ref, ckr_ref, st_ref, cos_ref, sin_ref,
         o_ref,
         zrg_ref, xc_ref, a_ref, b_ref, hrep_ref, q_ref, k_ref, v_ref, ycat_ref) = refs
    else:
        (x_ref, mod_ref, win_ref, cw_ref, cb_ref, wgate_ref, bgate_ref, lam_ref, qg_ref, kvg_ref,
         wuq_ref, wuk_ref, wuv_ref, wout_ref, lng_ref, lnb_ref,
         o_ref, ckv_out_ref, kr_out_ref, st_out_ref,
         zrg_ref, xc_ref, a_ref, b_ref, hrep_ref, q_ref, k_ref, v_ref, ycat_ref) = refs
    T = x_ref.shape[0]
    n_seq = T // seq_len
    shift, scale, gate = _mod_rows(mod_ref, row0, row_step)
    o_q = 2 * W_C
    o_kv = o_q + Q_LORA
    o_kr = o_kv + KV_LORA
    pair_w = 2 * HEAD_PAD
    even_lane = (lax.broadcasted_iota(jnp.int32, (ROW_BLOCK, HEAD_PAD), 1) & 1) == 0

    def row_block(rb):
        return pl.ds(pl.multiple_of(rb * ROW_BLOCK, ROW_BLOCK), ROW_BLOCK)

    def proj(rb, carry):
        rows = row_block(rb)
        k_rows = pl.ds(pl.multiple_of(past_len + rb * ROW_BLOCK, ROW_BLOCK), ROW_BLOCK)
        hb = (x_ref[rows, :] * (1.0 + scale) + shift).astype(BF16)
        zrg_ref[rows, :] = _dot(hb, win_ref[:, 0:o_q])
        qn = _rmsnorm(_dot(hb, win_ref[:, o_q:o_kv]), qg_ref[...]).astype(BF16)
        ckv = _rmsnorm(_dot(hb, win_ref[:, o_kv:o_kr]), kvg_ref[...])
        kr = _dot(hb, win_ref[:, o_kr:o_kr + HEAD_PAD])
        if latent:
            cos = cos_ref[rows, :]
            sin = sin_ref[rows, :]
            kr = _rope_pairs(kr, cos, sin, even_lane)
        else:
            ckv_out_ref[rows, :] = ckv
            kr_out_ref[rows, :] = kr[:, KR_LANE:KR_LANE + QK_ROPE]
        ckvb = ckv.astype(BF16)
        v_ref[k_rows, :] = _dot(ckvb, wuv_ref[...]).astype(BF16)
        for pair in range(D_HEADS // 2):
            pcols = slice(pair * pair_w, (pair + 1) * pair_w)
            q2 = _dot(qn, wuq_ref[:, pcols])
            k2 = _dot(ckvb, wuk_ref[:, pcols])
            for hh in range(2):
                cols = slice(pair * pair_w + hh * HEAD_PAD, pair * pair_w + (hh + 1) * HEAD_PAD)
                qh = q2[:, hh * HEAD_PAD:(hh + 1) * HEAD_PAD]
                if latent:
                    qh = _rope_pairs(qh, cos, sin, even_lane)
                q_ref[rows, cols] = qh.astype(BF16)
                k_ref[k_rows, cols] = (k2[:, hh * HEAD_PAD:(hh + 1) * HEAD_PAD] + kr).astype(BF16)
        return carry

    lax.fori_loop(0, T // ROW_BLOCK, proj, 0)

    if latent:
        def cached(cbk, carry):
            rows = row_block(cbk)
            cckv = cckv_ref[rows, :].astype(BF16)
            ckr = ckr_ref[rows, :]
            v_ref[rows, :] = _dot(cckv, wuv_ref[...]).astype(BF16)
            for pair in range(D_HEADS // 2):
                pcols = slice(pair * pair_w, (pair + 1) * pair_w)
                k2 = _dot(cckv, wuk_ref[:, pcols])
                for hh in range(2):
                    cols = slice(pair * pair_w + hh * HEAD_PAD, pair * pair_w + (hh + 1) * HEAD_PAD)
                    k_ref[rows, cols] = (k2[:, hh * HEAD_PAD:(hh + 1) * HEAD_PAD] + ckr).astype(BF16)
            return carry

        lax.fori_loop(0, past_len // ROW_BLOCK, cached, 0)

    pos_l = _row_pos((T, LANES), seq_len)
    cw = cw_ref[...]
    for cbk in range(W_C // LANES):
        cols = slice(cbk * LANES, (cbk + 1) * LANES)
        xr = zrg_ref[:, cols]
        xc_ref[:, cols] = (cw[0:1, cols] * _shift_rows(xr, -1, pos_l, seq_len) + cw[1:2, cols] * xr
                           + cw[2:3, cols] * _shift_rows(xr, 1, pos_l, seq_len)
                           + cw[3:4, cols] * _shift_rows(xr, 2, pos_l, seq_len) + cb_ref[:, cols])

    half_w = W_C // 2
    n_groups = seq_len // SUBLANES
    n_ends = T // SUBLANES
    sub = lax.broadcasted_iota(jnp.int32, (T, LANES), 0) & (SUBLANES - 1)
    posg = lax.broadcasted_iota(jnp.int32, (n_ends, LANES), 0) & (n_groups - 1)
    for direction in range(2):
        reverse = direction == 1
        neg = -lam_ref[direction:direction + 1, :]
        softplus = jnp.maximum(neg, 0.0) + jnp.log1p(jnp.exp(-jnp.abs(neg)))

        def gates(rb, carry, direction=direction, softplus=softplus):
            rows = row_block(rb)
            for j in range(2):
                cols = slice(j * half_w, (j + 1) * half_w)
                xcj = xc_ref[rows, cols]
                xcb = xcj.astype(BF16)
                r = jax.nn.sigmoid(_dot(xcb, wgate_ref[direction, 0, j]) + bgate_ref[direction, 0:1, cols])
                i = jax.nn.sigmoid(_dot(xcb, wgate_ref[direction, 1, j]) + bgate_ref[direction, 1:2, cols])
                log_a = -RG_C * r * softplus[:, cols]
                a = jnp.exp(log_a)
                b = jnp.sqrt(jnp.tanh(-log_a) * (a * a + 1.0)) * i * xcj
                for k in range(half_w // LANES):
                    cbk = j * (half_w // LANES) + k
                    a_ref[cbk, rows, :] = a[:, k * LANES:(k + 1) * LANES]
                    b_ref[cbk, rows, :] = b[:, k * LANES:(k + 1) * LANES]
            return carry

        lax.fori_loop(0, T // ROW_BLOCK, gates, 0)

        end_row = 0 if reverse else SUBLANES - 1
        for cbk in range(W_C // LANES):
            cols = slice(cbk * LANES, (cbk + 1) * LANES)
            a_loc, b_loc = _scan_groups(a_ref[cbk], b_ref[cbk], sub, reverse)
            a_ref[cbk] = a_loc
            b_ref[cbk] = b_loc
            a_end = a_ref[cbk, pl.ds(end_row, n_ends, stride=SUBLANES), :]
            b_end = b_ref[cbk, pl.ds(end_row, n_ends, stride=SUBLANES), :]
            a_tot, b_tot = _scan_ends(a_end, b_end, posg, n_groups, reverse)
            if latent:
                h0 = st_ref[direction:direction + 1, cols]
                ends = a_tot * h0 + b_tot
            else:
                h0 = jnp.zeros((1, LANES), F32)
                ends = b_tot
            if reverse:
                carry_in = jnp.where(posg < n_groups - 1, pltpu.roll(ends, n_ends - 1, 0), h0)
            else:
                carry_in = jnp.where(posg >= 1, pltpu.roll(ends, 1, 0), h0)
            for rr in range(SUBLANES):
                hrep_ref[pl.ds(rr, n_ends, stride=SUBLANES), :] = carry_in
            h = a_ref[cbk] * hrep_ref[...] + b_ref[cbk]
            if reverse:
                zrg_ref[:, cols] = zrg_ref[:, cols] + h
            else:
                zrg_ref[:, cols] = h
            if not latent:
                for s in range(n_seq):
                    g_last = s * n_groups + (0 if reverse else n_groups - 1)
                    st_out_ref[s, direction:direction + 1, cols] = ends[g_last:g_last + 1, :]

    def mix(rb, carry):
        rows = row_block(rb)
        ycat_ref[rows, 0:W_C] = (zrg_ref[rows, 0:W_C] * _gelu(zrg_ref[rows, W_C:2 * W_C])).astype(BF16)
        return carry

    lax.fori_loop(0, T // ROW_BLOCK, mix, 0)

    n_keys = past_len + seq_len

    def attend(qb, carry):
        q_rows = row_block(qb)
        if latent:
            k_rows = slice(0, n_keys)
        else:
            k_rows = pl.ds(pl.multiple_of((qb * ROW_BLOCK // seq_len) * seq_len, seq_len), seq_len)
        for pair in range(D_HEADS // 2):
            o_pair = None
            for h in (2 * pair, 2 * pair + 1):
                cols = slice(h * HEAD_PAD, (h + 1) * HEAD_PAD)
                s = lax.dot_general(q_ref[q_rows, cols], k_ref[k_rows, cols],
                                    (((1,), (1,)), ((), ())), preferred_element_type=F32) * ATTN_SCALE
                m = jnp.max(s, axis=-1, keepdims=True)
                p = jnp.exp(s - m)
                l = jnp.sum(p, axis=-1, keepdims=True)
                o = _dot(p.astype(BF16), v_ref[k_rows, cols]) / l
                o_pair = o if o_pair is None else o_pair + o
            ycat_ref[q_rows, W_C + pair * HEAD_PAD:W_C + (pair + 1) * HEAD_PAD] = o_pair.astype(BF16)
        return carry

    lax.fori_loop(0, T // ROW_BLOCK, attend, 0)

    def out(rb, carry):
        rows = row_block(rb)
        d = _dot(ycat_ref[rows, :], wout_ref[...])
        r = ALPHA * x_ref[rows, :] + gate * d
        o_ref[rows, :] = _layernorm(r, lng_ref[...], lnb_ref[...])
        return carry

    lax.fori_loop(0, T // ROW_BLOCK, out, 0)


def _odd_layer(x, mod_l, w, seq_len, row0, row_step, name, latent_inputs=None):
    n_rows = x.shape[0]
    n_tiles = n_rows // TILE_ROWS
    latent = latent_inputs is not None
    past_len = latent_inputs["cache_ckv"].shape[1] if latent else 0
    n_seq = TILE_ROWS // seq_len
    kern = functools.partial(_odd_kernel, seq_len=seq_len, latent=latent, past_len=past_len,
                             row0=row0, row_step=row_step)
    weights = [w["w_in"], w["conv_w"], w["conv_b"], w["w_gate"], w["b_gate"], w["lam"], w["q_g"], w["kv_g"],
               w["w_uq"], w["w_uk"], w["w_uv"], w["w_out"], w["ln_g"], w["ln_b"]]
    in_specs = [pl.BlockSpec((TILE_ROWS, D), lambda i: (i, 0), pipeline_mode=pl.Buffered(1)),
                pl.BlockSpec((SUBLANES, 3 * D), lambda i: (0, 0))]
    in_specs += [_const_spec(a.shape) for a in weights]
    args = [x, mod_l] + weights
    out_specs = [pl.BlockSpec((TILE_ROWS, D), lambda i: (i, 0), pipeline_mode=pl.Buffered(1))]
    out_shape = [jax.ShapeDtypeStruct((n_rows, D), F32)]
    if latent:
        li = latent_inputs
        in_specs += [
            pl.BlockSpec((None, past_len, KV_LORA), lambda i: (i, 0, 0)),
            pl.BlockSpec((None, past_len, HEAD_PAD), lambda i: (i, 0, 0)),
            pl.BlockSpec((None, 2, W_C), lambda i: (i, 0, 0)),
            _const_spec(li["cos"].shape),
            _const_spec(li["sin"].shape),
        ]
        args += [li["cache_ckv"], li["cache_kr"], li["state"], li["cos"], li["sin"]]
    else:
        out_specs += [
            pl.BlockSpec((TILE_ROWS, KV_LORA), lambda i: (i, 0)),
            pl.BlockSpec((TILE_ROWS, QK_ROPE), lambda i: (i, 0)),
            pl.BlockSpec((n_seq, 2, W_C), lambda i: (i, 0, 0)),
        ]
        out_shape += [
            jax.ShapeDtypeStruct((n_rows, KV_LORA), F32),
            jax.ShapeDtypeStruct((n_rows, QK_ROPE), F32),
            jax.ShapeDtypeStruct((n_rows // seq_len, 2, W_C), F32),
        ]
    n_keys_buf = past_len + TILE_ROWS
    scratch = [
        pltpu.VMEM((TILE_ROWS, 2 * W_C), F32),
        pltpu.VMEM((TILE_ROWS, W_C), F32),
        pltpu.VMEM((W_C // LANES, TILE_ROWS, LANES), F32),
        pltpu.VMEM((W_C // LANES, TILE_ROWS, LANES), F32),
        pltpu.VMEM((TILE_ROWS, LANES), F32),
        pltpu.VMEM((TILE_ROWS, D_HEADS * HEAD_PAD), BF16),
        pltpu.VMEM((n_keys_buf, D_HEADS * HEAD_PAD), BF16),
        pltpu.VMEM((n_keys_buf, D_HEADS * HEAD_PAD), BF16),
        pltpu.VMEM((TILE_ROWS, D), BF16),
    ]
    return pl.pallas_call(
        kern,
        grid=(n_tiles,),
        in_specs=in_specs,
        out_specs=out_specs,
        out_shape=out_shape,
        scratch_shapes=scratch,
        compiler_params=pltpu.CompilerParams(
            dimension_semantics=("arbitrary",), vmem_limit_bytes=VMEM_LIMIT),
        name=name,
    )(*args)


def _axial_rope_tables(rows):
    half = QK_ROPE // 2
    inv = ROPE_BASE ** (-jnp.arange(0, half, 2, dtype=F32) / half)
    r = jnp.repeat(jnp.arange(rows, dtype=F32), GRID_W)
    col = jnp.tile(jnp.arange(GRID_W, dtype=F32), rows)
    ang = jnp.concatenate([r[:, None] * inv, col[:, None] * inv], axis=-1)
    cos = jnp.repeat(jnp.cos(ang), 2, axis=-1)
    sin = jnp.repeat(jnp.sin(ang), 2, axis=-1) * jnp.tile(jnp.array([-1.0, 1.0], F32), half)
    n = ang.shape[0]
    pad_lo, pad_hi = KR_LANE, HEAD_PAD - KR_LANE - QK_ROPE
    cos_t = jnp.concatenate([jnp.ones((n, pad_lo), F32), cos, jnp.ones((n, pad_hi), F32)], axis=-1)
    sin_t = jnp.concatenate([jnp.zeros((n, pad_lo), F32), sin, jnp.zeros((n, pad_hi), F32)], axis=-1)
    return cos_t, sin_t


def _block_diag_gates(w):
    per = (W_C // 2) // C_BLOCK
    w4 = w.reshape(2, 2, per, C_BLOCK, C_BLOCK)
    eye = jnp.eye(per, dtype=w.dtype)
    bd = jnp.einsum("dJhij,hk->dJhikj", w4, eye)
    return bd.reshape(2, 2, per * C_BLOCK, per * C_BLOCK)


def kernel(x_prompt, x_sample, cache_mla_ckv, cache_mla_krope, state_rglru, c, c_ctx, w_mod, b_mod, ln1_g, ln1_b, ln2_g, ln2_b, even_w_in, even_w_s, even_b_s, even_w_pool, even_pool_scale, even_w_out, odd_w_in, rg_conv_w, rg_conv_b, rg_w_a, rg_b_a, rg_w_x, rg_b_x, rg_lam, mla_q_g, mla_kv_g, mla_w_uq, mla_w_uk, mla_w_uv, odd_w_out, ffn_w_up, ffn_conv_w, ffn_conv_b, ffn_w_down):
    batch, seq, _ = x_prompt.shape
    dec_batch, dec_seq, _ = x_sample.shape
    assert TILE_ROWS % seq == 0 and dec_seq == TILE_ROWS and dec_batch + 1 <= SUBLANES

    cond = jnp.concatenate([c_ctx[None, :], c, jnp.zeros((SUBLANES - 1 - dec_batch, D), F32)], axis=0)
    mod = _modulation(cond, w_mod, b_mod)

    xp = x_prompt.reshape(batch * seq, D)
    xs = x_sample.reshape(dec_batch * dec_seq, D)
    new_ckv = new_kr = new_state = None
    for l in range(DEPTH):
        ln1 = {"ln_g": ln1_g[l][None, :], "ln_b": ln1_b[l][None, :]}
        if l % 2 == 0:
            e = l // 2
            w = dict(ln1,
                     w_in=even_w_in[e].astype(BF16), w_s=even_w_s[e].astype(BF16),
                     b_s_t=even_b_s[e].T, w_pool=even_w_pool[e].astype(BF16),
                     pool_scale=even_pool_scale[e][None, :], w_out=even_w_out[e].astype(BF16))
            xp = _even_layer(xp, mod[l], w, seq, 0, 0, f"even{l}_ctx")
            xs = _even_layer(xs, mod[l], w, dec_seq, 1, 1, f"even{l}_lat")
        else:
            o = l // 2
            w_in = odd_w_in[o]
            n_main = 2 * W_C + Q_LORA + KV_LORA
            w_in_pad = jnp.concatenate(
                [w_in[:, :n_main], jnp.zeros((D, KR_LANE), F32), w_in[:, n_main:],
                 jnp.zeros((D, HEAD_PAD - KR_LANE - QK_ROPE), F32)], axis=1).astype(BF16)
            w_gate = jnp.stack([_block_diag_gates(rg_w_a[o]), _block_diag_gates(rg_w_x[o])], axis=1)
            b_gate = jnp.stack([rg_b_a[o], rg_b_x[o]], axis=1)
            w_uq = jnp.pad(mla_w_uq[o], ((0, 0), (0, 0), (0, HEAD_PAD - QK_NOPE - QK_ROPE)))
            w_uk = jnp.pad(mla_w_uk[o], ((0, 0), (0, 0), (0, HEAD_PAD - QK_NOPE)))
            uv = mla_w_uv[o].reshape(KV_LORA, D_HEADS // 2, 2, V_DIM)
            zeros_v = jnp.zeros((KV_LORA, D_HEADS // 2, V_DIM), F32)
            w_uv = jnp.stack([jnp.concatenate([uv[:, :, 0], zeros_v], axis=-1),
                              jnp.concatenate([zeros_v, uv[:, :, 1]], axis=-1)], axis=2)
            w = dict(ln1,
                     w_in=w_in_pad, conv_w=rg_conv_w[o], conv_b=rg_conv_b[o][None, :],
                     w_gate=w_gate.astype(BF16), b_gate=b_gate, lam=rg_lam[o],
                     q_g=mla_q_g[o][None, :], kv_g=mla_kv_g[o][None, :],
                     w_uq=w_uq.reshape(Q_LORA, D_HEADS * HEAD_PAD).astype(BF16),
                     w_uk=w_uk.reshape(KV_LORA, D_HEADS * HEAD_PAD).astype(BF16),
                     w_uv=w_uv.reshape(KV_LORA, D_HEADS * HEAD_PAD).astype(BF16),
                     w_out=odd_w_out[o].astype(BF16))
            xp, ckv, kr, st = _odd_layer(xp, mod[l], w, seq, 0, 0, f"odd{l}_ctx")
            new_ckv = ckv.reshape(batch, 1, seq, KV_LORA)
            new_kr = kr.reshape(batch, 1, seq, QK_ROPE)
            new_state = st.reshape(batch, 1, 2, W_C)
            cos_t, sin_t = _axial_rope_tables(dec_seq // GRID_W)
            latent_inputs = {
                "cache_ckv": cache_mla_ckv[:, o],
                "cache_kr": jnp.pad(cache_mla_krope[:, o],
                                    ((0, 0), (0, 0), (KR_LANE, HEAD_PAD - KR_LANE - QK_ROPE))),
                "state": state_rglru[:, o],
                "cos": cos_t, "sin": sin_t,
            }
            (xs,) = _odd_layer(xs, mod[l], w, dec_seq, 1, 1, f"odd{l}_lat", latent_inputs)
        n_ch = 2 * D_FF // FF_CHUNK
        wf = {"ln_g": ln2_g[l][None, :], "ln_b": ln2_b[l][None, :],
              "w_up": ffn_w_up[l].reshape(D, n_ch, FF_CHUNK).transpose(1, 0, 2).astype(BF16),
              "conv_w": ffn_conv_w[l].reshape(3, n_ch, FF_CHUNK).transpose(1, 0, 2),
              "conv_b": ffn_conv_b[l].reshape(n_ch, 1, FF_CHUNK),
              "w_down": ffn_w_down[l].reshape(n_ch // 2, FF_CHUNK, D).astype(BF16)}
        xp = _ffn_layer(xp, mod[l], wf, seq, 0, 0, f"ffn{l}_ctx")
        xs = _ffn_layer(xs, mod[l], wf, dec_seq, 1, 1, f"ffn{l}_lat")
    return (xp.reshape(batch, seq, D), xs.reshape(dec_batch, dec_seq, D), new_ckv, new_kr, new_state)
```

```python
import functools
import math

import jax
import jax.numpy as jnp
from jax import lax
from jax.experimental import pallas as pl
from jax.experimental.pallas import tpu as pltpu

F32 = jnp.float32
BF16 = jnp.bfloat16

D = 1024
DEPTH = 2
GRID_W = 64
ALPHA = (2 * DEPTH) ** 0.25
LN_EPS = 1e-5
RMS_EPS = 1e-6
A_HEADS = 4
A_WIDTH = 512
A_GROUP_W = A_WIDTH // A_HEADS
CHUNK = 128
POOL_WINDOWS = (2, 4, 8, 16)
B_GROUP = 128
B_WIDTH = 512
W_C = 512
C_HEADS = 8
C_BLOCK = 64
RG_C = 8.0
D_HEADS = 8
Q_LORA = 384
KV_LORA = 256
QK_NOPE = 64
QK_ROPE = 32
V_DIM = 64
ROPE_BASE = 10000.0
ATTN_SCALE = 1.0 / math.sqrt(QK_NOPE + QK_ROPE)
D_FF = 2816

LANES = 128
SUBLANES = 8
HEAD_PAD = LANES
ODD_IN_PAD = 2 * W_C + Q_LORA + KV_LORA + HEAD_PAD
KR_LANE = QK_NOPE
TILE_ROWS = 1024
ROW_BLOCK = 256
FF_CHUNK = 256
FFN_DOT_ROWS = 128
FFN_RB = 64
VMEM_LIMIT = 56 * 1024 * 1024


def _dot(a, b):
    return jnp.dot(a, b, preferred_element_type=F32)


def _gelu(x):
    return x * (0.5 * (1.0 + jnp.tanh(0.7978845608028654 * (x + 0.044715 * (x * x * x)))))


def _layernorm(r, g, b):
    mu = jnp.mean(r, axis=-1, keepdims=True)
    c = r - mu
    var = jnp.mean(c * c, axis=-1, keepdims=True)
    return c * lax.rsqrt(var + LN_EPS) * g + b


def _row_pos(shape, seq_len):
    return lax.broadcasted_iota(jnp.int32, shape, 0) & (seq_len - 1)


def _shift_rows(x, d, pos, seq_len):
    n = x.shape[0]
    y = pltpu.roll(x, (-d) % n, 0)
    valid = (pos < seq_len - d) if d > 0 else (pos >= -d)
    return jnp.where(valid, y, 0.0)


def _mod_kernel(s_ref, w_ref, b_ref, o_ref):
    s = s_ref[...]
    s = s * jax.nn.sigmoid(s)
    o_ref[...] = _dot(s.astype(BF16), w_ref[...].astype(BF16)) + b_ref[...]


def _modulation(cond, w_mod, b_mod):
    nc = 1536
    return pl.pallas_call(
        _mod_kernel,
        grid=(DEPTH, 6 * D // nc),
        in_specs=[
            pl.BlockSpec((SUBLANES, D), lambda l, j: (0, 0)),
            pl.BlockSpec((None, D, nc), lambda l, j: (l, 0, j)),
            pl.BlockSpec((None, 1, nc), lambda l, j: (l, 0, j)),
        ],
        out_specs=pl.BlockSpec((None, SUBLANES, nc), lambda l, j: (l, 0, j)),
        out_shape=jax.ShapeDtypeStruct((DEPTH, SUBLANES, 6 * D), F32),
        compiler_params=pltpu.CompilerParams(
            dimension_semantics=("arbitrary", "arbitrary"), vmem_limit_bytes=VMEM_LIMIT),
        name="modulation",
    )(cond, w_mod, b_mod.reshape(DEPTH, 1, 6 * D))


def _mod_rows(mod_ref, row0, row_step):
    row = row0 + pl.program_id(0) * row_step
    m = mod_ref[pl.ds(row, 1), :]
    return m[:, 0:D], m[:, D:2 * D], m[:, 2 * D:3 * D]


def _const_spec(shape):
    nd = len(shape)
    return pl.BlockSpec(shape, lambda *_: (0,) * nd, pipeline_mode=pl.Buffered(1))


def _layer_spec(shape, l):
    nd = len(shape) - 1
    return pl.BlockSpec((None,) + tuple(shape[1:]), lambda *_: (l,) + (0,) * nd, pipeline_mode=pl.Buffered(1))


def _mod_spec(l, half):
    return pl.BlockSpec((None, SUBLANES, 3 * D), lambda *_: (l, 0, half))


def _even_kernel(x_ref, mod_ref, win_ref, ws_ref, bst_ref, wpool_ref, pscale_ref, wout_ref,
                 lng_ref, lnb_ref, o_ref, u_ref, vn_ref, p_ref, ycat_ref, *, seq_len, row0, row_step):
    T = x_ref.shape[0]
    shift, scale, gate = _mod_rows(mod_ref, row0, row_step)

    def proj(rb, carry):
        rows = pl.ds(pl.multiple_of(rb * ROW_BLOCK, ROW_BLOCK), ROW_BLOCK)
        hb = (x_ref[rows, :] * (1.0 + scale) + shift).astype(BF16)
        u_ref[rows, :] = _gelu(_dot(hb, win_ref[:, 0:A_WIDTH]))
        v = _gelu(_dot(hb, win_ref[:, A_WIDTH:2 * A_WIDTH]))
        mu = jnp.mean(v, axis=-1, keepdims=True)
        c = v - mu
        var = jnp.mean(c * c, axis=-1, keepdims=True)
        vn_ref[rows, :] = (c * lax.rsqrt(var + LN_EPS)).astype(BF16)
        p_ref[rows, :] = _dot(hb, win_ref[:, 2 * A_WIDTH:])
        return carry

    lax.fori_loop(0, T // ROW_BLOCK, proj, 0)

    def gate_chunk(n, carry):
        rows = pl.ds(pl.multiple_of(n * CHUNK, CHUNK), CHUNK)
        for h in range(A_HEADS):
            cols = slice(h * A_GROUP_W, (h + 1) * A_GROUP_W)
            s = _dot(ws_ref[h], vn_ref[rows, cols]) + bst_ref[:, h:h + 1]
            ycat_ref[rows, cols] = (u_ref[rows, cols] * s).astype(BF16)
        return carry

    lax.fori_loop(0, T // CHUNK, gate_chunk, 0)

    pos = _row_pos((T, B_GROUP), seq_len)
    for gi, w in enumerate(POOL_WINDOWS):
        cols = slice(gi * B_GROUP, (gi + 1) * B_GROUP)
        half = w // 2
        pg = p_ref[:, cols]
        fwd = pg
        bwd = _shift_rows(pg, -1, pos, seq_len)
        k = 1
        while k < half:
            fwd = fwd + _shift_rows(fwd, k, pos, seq_len)
            bwd = bwd + _shift_rows(bwd, -k, pos, seq_len)
            k *= 2
        cnt = jnp.minimum(pos + half, seq_len) - jnp.maximum(pos - half, 0)
        pooled = (fwd + bwd) / cnt.astype(F32) - pg
        yb = _dot(pooled.astype(BF16), wpool_ref[gi]) * pscale_ref[:, cols]
        ycat_ref[:, A_WIDTH + gi * B_GROUP:A_WIDTH + (gi + 1) * B_GROUP] = yb.astype(BF16)

    def out(rb, carry):
        rows = pl.ds(pl.multiple_of(rb * ROW_BLOCK, ROW_BLOCK), ROW_BLOCK)
        d = _dot(ycat_ref[rows, :], wout_ref[...])
        r = ALPHA * x_ref[rows, :] + gate * d
        o_ref[rows, :] = _layernorm(r, lng_ref[...], lnb_ref[...])
        return carry

    lax.fori_loop(0, T // ROW_BLOCK, out, 0)


def _even_layer(x, mod, w, l, seq_len, row0, row_step, name):
    n_rows = x.shape[0]
    kern = functools.partial(_even_kernel, seq_len=seq_len, row0=row0, row_step=row_step)
    return pl.pallas_call(
        kern,
        grid=(n_rows // TILE_ROWS,),
        in_specs=[
            pl.BlockSpec((TILE_ROWS, D), lambda i: (i, 0)),
            _mod_spec(l, 0),
            _const_spec(w["w_in"].shape),
            _const_spec(w["w_s"].shape),
            _const_spec(w["b_s_t"].shape),
            _const_spec(w["w_pool"].shape),
            _const_spec(w["pool_scale"].shape),
            _const_spec(w["w_out"].shape),
            _layer_spec(w["ln_g"].shape, l),
            _layer_spec(w["ln_b"].shape, l),
        ],
        out_specs=pl.BlockSpec((TILE_ROWS, D), lambda i: (i, 0)),
        out_shape=jax.ShapeDtypeStruct((n_rows, D), F32),
        scratch_shapes=[
            pltpu.VMEM((TILE_ROWS, A_WIDTH), F32),
            pltpu.VMEM((TILE_ROWS, A_WIDTH), BF16),
            pltpu.VMEM((TILE_ROWS, B_WIDTH), F32),
            pltpu.VMEM((TILE_ROWS, D), BF16),
        ],
        compiler_params=pltpu.CompilerParams(
            dimension_semantics=("arbitrary",), vmem_limit_bytes=VMEM_LIMIT),
        name=name,
    )(x, mod, w["w_in"], w["w_s"], w["b_s_t"], w["w_pool"], w["pool_scale"], w["w_out"],
      w["ln_g"], w["ln_b"])


def _conv3_block(z_ref, r0, lanes, cw, cb, seq_len):
    nv = FFN_RB // SUBLANES
    z3 = z_ref[r0:r0 + FFN_RB, lanes].reshape(nv, SUBLANES, LANES)
    sub = lax.broadcasted_iota(jnp.int32, (nv, SUBLANES, LANES), 1)
    down = pltpu.roll(z3, 1, 1)
    up = pltpu.roll(z3, SUBLANES - 1, 1)
    if r0 % seq_len == 0:
        prev = jnp.zeros((1, SUBLANES, LANES), F32)
    else:
        prev = pltpu.roll(z_ref[r0 - SUBLANES:r0, lanes], 1, 0)[None]
    if (r0 + FFN_RB) % seq_len == 0:
        nxt = jnp.zeros((1, SUBLANES, LANES), F32)
    else:
        nxt = pltpu.roll(z_ref[r0 + FFN_RB:r0 + FFN_RB + SUBLANES, lanes], SUBLANES - 1, 0)[None]
    zm = jnp.where(sub == 0, jnp.concatenate([prev, down[:-1]], axis=0), down)
    zp = jnp.where(sub == SUBLANES - 1, jnp.concatenate([up[1:], nxt], axis=0), up)
    y = cw[0][:, lanes] * zm + cw[1][:, lanes] * z3 + cw[2][:, lanes] * zp + cb[:, lanes]
    return y.reshape(FFN_RB, LANES)


def _ffn_kernel(x_ref, mod_ref, wup_ref, cw_ref, cb_ref, wd_ref, lng_ref, lnb_ref, o_ref,
                hb_ref, zg0_ref, zv0_ref, zg1_ref, zv1_ref, a_ref, *, seq_len, row0, row_step):
    T = x_ref.shape[0]
    n_chunks = wd_ref.shape[0]
    shift, scale, gate = _mod_rows(mod_ref, row0, row_step)
    z_bufs = ((zg0_ref, zv0_ref), (zg1_ref, zv1_ref))

    def row_block(rb):
        return pl.ds(pl.multiple_of(rb * ROW_BLOCK, ROW_BLOCK), ROW_BLOCK)

    def modulate(rb, carry):
        rows = row_block(rb)
        hb_ref[rows, :] = (x_ref[rows, :] * (1.0 + scale) + shift).astype(BF16)
        return carry

    lax.fori_loop(0, T // ROW_BLOCK, modulate, 0)

    n_parts = T // FFN_DOT_ROWS
    blocks_per_part = FFN_DOT_ROWS // FFN_RB

    def up_proj(c, bufs, q):
        rows = slice(q * FFN_DOT_ROWS, (q + 1) * FFN_DOT_ROWS)
        bufs[0][rows, :] = _dot(hb_ref[rows, :], wup_ref[c])
        bufs[1][rows, :] = _dot(hb_ref[rows, :], wup_ref[c + n_chunks])

    def gate_blocks(c, bufs, q):
        cwg = [cw_ref[k, pl.ds(c, 1), :] for k in range(3)]
        cwv = [cw_ref[k, pl.ds(c + n_chunks, 1), :] for k in range(3)]
        cbg, cbv = cb_ref[pl.ds(c, 1), :], cb_ref[pl.ds(c + n_chunks, 1), :]
        for rb in range(q * blocks_per_part, (q + 1) * blocks_per_part):
            r0 = rb * FFN_RB
            for lt in range(FF_CHUNK // LANES):
                lanes = slice(lt * LANES, (lt + 1) * LANES)
                g = _conv3_block(bufs[0], r0, lanes, cwg, cbg, seq_len)
                v = _conv3_block(bufs[1], r0, lanes, cwv, cbv, seq_len)
                a_ref[c, r0:r0 + FFN_RB, lanes] = (g * jax.nn.sigmoid(g) * v).astype(BF16)

    for q in range(n_parts):
        up_proj(0, z_bufs[0], q)

    def chunk_pair(k, carry):
        c = 2 * k
        for q in range(n_parts):
            up_proj(c + 1, z_bufs[1], q)
            gate_blocks(c, z_bufs[0], q)
        for q in range(n_parts):
            up_proj(c + 2, z_bufs[0], q)
            gate_blocks(c + 1, z_bufs[1], q)
        return carry

    assert n_chunks % 2 == 1
    lax.fori_loop(0, n_chunks // 2, chunk_pair, 0)
    for q in range(n_parts):
        gate_blocks(n_chunks - 1, z_bufs[0], q)

    def out(rb, carry):
        rows = row_block(rb)
        acc = _dot(a_ref[0, rows, :], wd_ref[0])
        for c in range(1, n_chunks):
            acc = acc + _dot(a_ref[c, rows, :], wd_ref[c])
        r = ALPHA * x_ref[rows, :] + gate * acc
        o_ref[rows, :] = _layernorm(r, lng_ref[...], lnb_ref[...])
        return carry

    lax.fori_loop(0, T // ROW_BLOCK, out, 0)


def _ffn_layer(x, mod, w, l, seq_len, row0, row_step, name):
    n_rows = x.shape[0]
    n_chunks = D_FF // FF_CHUNK
    kern = functools.partial(_ffn_kernel, seq_len=seq_len, row0=row0, row_step=row_step)
    params = [w["w_up"], w["conv_w"], w["conv_b"], w["w_down"], w["ln_g"], w["ln_b"]]
    return pl.pallas_call(
        kern,
        grid=(n_rows // TILE_ROWS,),
        in_specs=[pl.BlockSpec((TILE_ROWS, D), lambda i: (i, 0)), _mod_spec(l, 1)]
        + [_layer_spec(a.shape, l) for a in params],
        out_specs=pl.BlockSpec((TILE_ROWS, D), lambda i: (i, 0)),
        out_shape=jax.ShapeDtypeStruct((n_rows, D), F32),
        scratch_shapes=[
            pltpu.VMEM((TILE_ROWS, D), BF16),
            pltpu.VMEM((TILE_ROWS, FF_CHUNK), F32),
            pltpu.VMEM((TILE_ROWS, FF_CHUNK), F32),
            pltpu.VMEM((TILE_ROWS, FF_CHUNK), F32),
            pltpu.VMEM((TILE_ROWS, FF_CHUNK), F32),
            pltpu.VMEM((n_chunks, TILE_ROWS, FF_CHUNK), BF16),
        ],
        compiler_params=pltpu.CompilerParams(
            dimension_semantics=("arbitrary",), vmem_limit_bytes=VMEM_LIMIT),
        name=name,
    )(x, mod, *params)


def _prep_up_kernel(w_ref, o_ref):
    for k in range(o_ref.shape[0]):
        o_ref[k] = w_ref[:, k * FF_CHUNK:(k + 1) * FF_CHUNK].astype(BF16)


def _prep_down_kernel(w_ref, o_ref):
    o_ref[...] = w_ref[...].astype(BF16)


def _prep_ffn_weights(w_up, w_down):
    n_ch = 2 * D_FF // FF_CHUNK
    up = pl.pallas_call(
        _prep_up_kernel,
        grid=(DEPTH, 2),
        in_specs=[pl.BlockSpec((None, D, D_FF), lambda l, j: (l, 0, j))],
        out_specs=pl.BlockSpec((None, n_ch // 2, D, FF_CHUNK), lambda l, j: (l, j, 0, 0)),
        out_shape=jax.ShapeDtypeStruct((DEPTH, n_ch, D, FF_CHUNK), BF16),
        compiler_params=pltpu.CompilerParams(
            dimension_semantics=("arbitrary", "arbitrary"), vmem_limit_bytes=VMEM_LIMIT),
        name="ffn_up_bf16",
    )(w_up)
    half = D_FF // 2
    down = pl.pallas_call(
        _prep_down_kernel,
        grid=(DEPTH, 2),
        in_specs=[pl.BlockSpec((None, half, D), lambda l, j: (l, j, 0))],
        out_specs=pl.BlockSpec((None, half, D), lambda l, j: (l, j, 0)),
        out_shape=jax.ShapeDtypeStruct((DEPTH, D_FF, D), BF16),
        compiler_params=pltpu.CompilerParams(
            dimension_semantics=("arbitrary", "arbitrary"), vmem_limit_bytes=VMEM_LIMIT),
        name="ffn_down_bf16",
    )(w_down)
    return up, down.reshape(DEPTH, n_ch // 2, FF_CHUNK, D)


def _scan_groups(a, b, sub, reverse):
    n = a.shape[0]
    for d in (1, 2, 4):
        if reverse:
            valid, sh = sub < SUBLANES - d, n - d
        else:
            valid, sh = sub >= d, d
        a_prev = jnp.where(valid, pltpu.roll(a, sh, 0), 1.0)
        b_prev = jnp.where(valid, pltpu.roll(b, sh, 0), 0.0)
        b = b + a * b_prev
        a = a * a_prev
    return a, b


def _scan_ends(a, b, posg, n_groups, reverse):
    n = a.shape[0]
    d = 1
    while d < n_groups:
        if reverse:
            valid, sh = posg < n_groups - d, n - d
        else:
            valid, sh = posg >= d, d
        a_prev = jnp.where(valid, pltpu.roll(a, sh, 0), 1.0)
        b_prev = jnp.where(valid, pltpu.roll(b, sh, 0), 0.0)
        b = b + a * b_prev
        a = a * a_prev
        d *= 2
    return a, b


def _rope_pairs(x, cos, sin_signed, even_lane):
    n = x.shape[1]
    swapped = jnp.where(even_lane, pltpu.roll(x, n - 1, 1), pltpu.roll(x, 1, 1))
    return x * cos + swapped * sin_signed


def _rmsnorm(x, g):
    return x * lax.rsqrt(jnp.mean(x * x, axis=-1, keepdims=True) + RMS_EPS) * g


def _odd_kernel(*refs, seq_len, latent, past_len, row0, row_step):
    if latent:
        (x_ref, mod_ref, win_ref, cw_ref, cb_ref, wgate_ref, bgate_ref, lam_ref, qg_ref, kvg_ref,
         wuq_ref, wuk_ref, wuv_ref, wout_ref, lng_ref, lnb_ref,
         cckv_ref, ckr_ref, st_ref, cos_ref, sin_ref,
         o_ref,
         zrg_ref, xc_ref, a_ref, b_ref, hrep_ref, q_ref, k_ref, v_ref, ycat_ref) = refs
    else:
        (x_ref, mod_ref, win_ref, cw_ref, cb_ref, wgate_ref, bgate_ref, lam_ref, qg_ref, kvg_ref,
         wuq_ref, wuk_ref, wuv_ref, wout_ref, lng_ref, lnb_ref,
         o_ref, ckv_out_ref, kr_out_ref, st_out_ref,
         zrg_ref, xc_ref, a_ref, b_ref, hrep_ref, q_ref, k_ref, v_ref, ycat_ref) = refs
    T = x_ref.shape[0]
    n_seq = T // seq_len
    shift, scale, gate = _mod_rows(mod_ref, row0, row_step)
    o_q = 2 * W_C
    o_kv = o_q + Q_LORA
    o_kr = o_kv + KV_LORA
    pair_w = 2 * HEAD_PAD
    even_lane = (lax.broadcasted_iota(jnp.int32, (ROW_BLOCK, HEAD_PAD), 1) & 1) == 0

    def row_block(rb):
        return pl.ds(pl.multiple_of(rb * ROW_BLOCK, ROW_BLOCK), ROW_BLOCK)

    def proj(rb, carry):
        rows = row_block(rb)
        k_rows = pl.ds(pl.multiple_of(past_len + rb * ROW_BLOCK, ROW_BLOCK), ROW_BLOCK)
        hb = (x_ref[rows, :] * (1.0 + scale) + shift).astype(BF16)
        zrg_ref[rows, :] = _dot(hb, win_ref[:, 0:o_q])
        qn = _rmsnorm(_dot(hb, win_ref[:, o_q:o_kv]), qg_ref[...]).astype(BF16)
        ckv = _rmsnorm(_dot(hb, win_ref[:, o_kv:o_kr]), kvg_ref[...])
        kr = _dot(hb, win_ref[:, o_kr:o_kr + HEAD_PAD])
        if latent:
            cos = cos_ref[rows, :]
            sin = sin_ref[rows, :]
            kr = _rope_pairs(kr, cos, sin, even_lane)
        else:
            ckv_out_ref[rows, :] = ckv
            kr_out_ref[rows, :] = kr[:, KR_LANE:KR_LANE + QK_ROPE]
        ckvb = ckv.astype(BF16)
        v_ref[k_rows, :] = _dot(ckvb, wuv_ref[...]).astype(BF16)
        for pair in range(D_HEADS // 2):
            pcols = slice(pair * pair_w, (pair + 1) * pair_w)
            q2 = _dot(qn, wuq_ref[:, pcols])
            k2 = _dot(ckvb, wuk_ref[:, pcols])
            for hh in range(2):
                cols = slice(pair * pair_w + hh * HEAD_PAD, pair * pair_w + (hh + 1) * HEAD_PAD)
                qh = q2[:, hh * HEAD_PAD:(hh + 1) * HEAD_PAD]
                if latent:
                    qh = _rope_pairs(qh, cos, sin, even_lane)
                q_ref[rows, cols] = qh.astype(BF16)
                k_ref[k_rows, cols] = (k2[:, hh * HEAD_PAD:(hh + 1) * HEAD_PAD] + kr).astype(BF16)
        return carry

    lax.fori_loop(0, T // ROW_BLOCK, proj, 0)

    if latent:
        def cached(cbk, carry):
            rows = row_block(cbk)
            cckv = cckv_ref[rows, :].astype(BF16)
            ckr = ckr_ref[rows, :]
            v_ref[rows, :] = _dot(cckv, wuv_ref[...]).astype(BF16)
            for pair in range(D_HEADS // 2):
                pcols = slice(pair * pair_w, (pair + 1) * pair_w)
                k2 = _dot(cckv, wuk_ref[:, pcols])
                for hh in range(2):
                    cols = slice(pair * pair_w + hh * HEAD_PAD, pair * pair_w + (hh + 1) * HEAD_PAD)
                    k_ref[rows, cols] = (k2[:, hh * HEAD_PAD:(hh + 1) * HEAD_PAD] + ckr).astype(BF16)
            return carry

        lax.fori_loop(0, past_len // ROW_BLOCK, cached, 0)

    pos_l = _row_pos((T, LANES), seq_len)
    cw = cw_ref[...]
    for cbk in range(W_C // LANES):
        cols = slice(cbk * LANES, (cbk + 1) * LANES)
        xr = zrg_ref[:, cols]
        xc_ref[:, cols] = (cw[0:1, cols] * _shift_rows(xr, -1, pos_l, seq_len) + cw[1:2, cols] * xr
                           + cw[2:3, cols] * _shift_rows(xr, 1, pos_l, seq_len)
                           + cw[3:4, cols] * _shift_rows(xr, 2, pos_l, seq_len) + cb_ref[:, cols])

    half_w = W_C // 2
    n_groups = seq_len // SUBLANES
    n_ends = T // SUBLANES
    sub = lax.broadcasted_iota(jnp.int32, (T, LANES), 0) & (SUBLANES - 1)
    posg = lax.broadcasted_iota(jnp.int32, (n_ends, LANES), 0) & (n_groups - 1)
    for direction in range(2):
        reverse = direction == 1
        neg = -lam_ref[direction:direction + 1, :]
        softplus = jnp.maximum(neg, 0.0) + jnp.log1p(jnp.exp(-jnp.abs(neg)))

        def gates(rb, carry, direction=direction, softplus=softplus):
            rows = row_block(rb)
            for j in range(2):
                cols = slice(j * half_w, (j + 1) * half_w)
                xcj = xc_ref[rows, cols]
                xcb = xcj.astype(BF16)
                r = jax.nn.sigmoid(_dot(xcb, wgate_ref[direction, 0, j]) + bgate_ref[direction, 0:1, cols])
                i = jax.nn.sigmoid(_dot(xcb, wgate_ref[direction, 1, j]) + bgate_ref[direction, 1:2, cols])
                log_a = -RG_C * r * softplus[:, cols]
                a = jnp.exp(log_a)
                b = jnp.sqrt(jnp.tanh(-log_a) * (a * a + 1.0)) * i * xcj
                for k in range(half_w // LANES):
                    cbk = j * (half_w // LANES) + k
                    a_ref[cbk, rows, :] = a[:, k * LANES:(k + 1) * LANES]
                    b_ref[cbk, rows, :] = b[:, k * LANES:(k + 1) * LANES]
            return carry

        lax.fori_loop(0, T // ROW_BLOCK, gates, 0)

        end_row = 0 if reverse else SUBLANES - 1
        for cbk in range(W_C // LANES):
            cols = slice(cbk * LANES, (cbk + 1) * LANES)
            a_loc, b_loc = _scan_groups(a_ref[cbk], b_ref[cbk], sub, reverse)
            a_ref[cbk] = a_loc
            b_ref[cbk] = b_loc
            a_end = a_ref[cbk, pl.ds(end_row, n_ends, stride=SUBLANES), :]
            b_end = b_ref[cbk, pl.ds(end_row, n_ends, stride=SUBLANES), :]
            a_tot, b_tot = _scan_ends(a_end, b_end, posg, n_groups, reverse)
            if latent:
                h0 = st_ref[direction:direction + 1, cols]
                ends = a_tot * h0 + b_tot
            else:
                h0 = jnp.zeros((1, LANES), F32)
                ends = b_tot
            if reverse:
                carry_in = jnp.where(posg < n_groups - 1, pltpu.roll(ends, n_ends - 1, 0), h0)
            else:
                carry_in = jnp.where(posg >= 1, pltpu.roll(ends, 1, 0), h0)
            for rr in range(SUBLANES):
                hrep_ref[pl.ds(rr, n_ends, stride=SUBLANES), :] = carry_in
            h = a_ref[cbk] * hrep_ref[...] + b_ref[cbk]
            if reverse:
                zrg_ref[:, cols] = zrg_ref[:, cols] + h
            else:
                zrg_ref[:, cols] = h
            if not latent:
                for s in range(n_seq):
                    g_last = s * n_groups + (0 if reverse else n_groups - 1)
                    st_out_ref[s, direction:direction + 1, cols] = ends[g_last:g_last + 1, :]

    def mix(rb, carry):
        rows = row_block(rb)
        ycat_ref[rows, 0:W_C] = (zrg_ref[rows, 0:W_C] * _gelu(zrg_ref[rows, W_C:2 * W_C])).astype(BF16)
        return carry

    lax.fori_loop(0, T // ROW_BLOCK, mix, 0)

    n_keys = past_len + seq_len

    def attend(qb, carry):
        q_rows = row_block(qb)
        if latent:
            k_rows = slice(0, n_keys)
        else:
            k_rows = pl.ds(pl.multiple_of((qb * ROW_BLOCK // seq_len) * seq_len, seq_len), seq_len)
        for pair in range(D_HEADS // 2):
            o_pair = None
            for h in (2 * pair, 2 * pair + 1):
                cols = slice(h * HEAD_PAD, (h + 1) * HEAD_PAD)
                s = lax.dot_general(q_ref[q_rows, cols], k_ref[k_rows, cols],
                                    (((1,), (1,)), ((), ())), preferred_element_type=F32) * ATTN_SCALE
                m = jnp.max(s, axis=-1, keepdims=True)
                p = jnp.exp(s - m)
                l = jnp.sum(p, axis=-1, keepdims=True)
                o = _dot(p.astype(BF16), v_ref[k_rows, cols]) / l
                o_pair = o if o_pair is None else o_pair + o
            ycat_ref[q_rows, W_C + pair * HEAD_PAD:W_C + (pair + 1) * HEAD_PAD] = o_pair.astype(BF16)
        return carry

    lax.fori_loop(0, T // ROW_BLOCK, attend, 0)

    def out(rb, carry):
        rows = row_block(rb)
        d = _dot(ycat_ref[rows, :], wout_ref[...])
        r = ALPHA * x_ref[rows, :] + gate * d
        o_ref[rows, :] = _layernorm(r, lng_ref[...], lnb_ref[...])
        return carry

    lax.fori_loop(0, T // ROW_BLOCK, out, 0)


def _odd_layer(x, mod, w, l, seq_len, row0, row_step, name, latent_inputs=None):
    n_rows = x.shape[0]
    n_tiles = n_rows // TILE_ROWS
    latent = latent_inputs is not None
    past_len = latent_inputs["cache_ckv"].shape[1] if latent else 0
    n_seq = TILE_ROWS // seq_len
    kern = functools.partial(_odd_kernel, seq_len=seq_len, latent=latent, past_len=past_len,
                             row0=row0, row_step=row_step)
    weights = [w["w_in"], w["conv_w"], w["conv_b"], w["w_gate"], w["b_gate"], w["lam"], w["q_g"], w["kv_g"],
               w["w_uq"], w["w_uk"], w["w_uv"], w["w_out"]]
    in_specs = [pl.BlockSpec((TILE_ROWS, D), lambda i: (i, 0), pipeline_mode=pl.Buffered(1)),
                _mod_spec(l, 0)]
    in_specs += [_const_spec(a.shape) for a in weights]
    in_specs += [_layer_spec(w["ln_g"].shape, l), _layer_spec(w["ln_b"].shape, l)]
    args = [x, mod] + weights + [w["ln_g"], w["ln_b"]]
    out_specs = [pl.BlockSpec((TILE_ROWS, D), lambda i: (i, 0), pipeline_mode=pl.Buffered(1))]
    out_shape = [jax.ShapeDtypeStruct((n_rows, D), F32)]
    if latent:
        li = latent_inputs
        in_specs += [
            pl.BlockSpec((None, past_len, KV_LORA), lambda i: (i, 0, 0)),
            pl.BlockSpec((None, past_len, HEAD_PAD), lambda i: (i, 0, 0)),
            pl.BlockSpec((None, 2, W_C), lambda i: (i, 0, 0)),
            _const_spec(li["cos"].shape),
            _const_spec(li["sin"].shape),
        ]
        args += [li["cache_ckv"], li["cache_kr"], li["state"], li["cos"], li["sin"]]
    else:
        out_specs += [
            pl.BlockSpec((TILE_ROWS, KV_LORA), lambda i: (i, 0)),
            pl.BlockSpec((TILE_ROWS, QK_ROPE), lambda i: (i, 0)),
            pl.BlockSpec((n_seq, 2, W_C), lambda i: (i, 0, 0)),
        ]
        out_shape += [
            jax.ShapeDtypeStruct((n_rows, KV_LORA), F32),
            jax.ShapeDtypeStruct((n_rows, QK_ROPE), F32),
            jax.ShapeDtypeStruct((n_rows // seq_len, 2, W_C), F32),
        ]
    n_keys_buf = past_len + TILE_ROWS
    scratch = [
        pltpu.VMEM((TILE_ROWS, 2 * W_C), F32),
        pltpu.VMEM((TILE_ROWS, W_C), F32),
        pltpu.VMEM((W_C // LANES, TILE_ROWS, LANES), F32),
        pltpu.VMEM((W_C // LANES, TILE_ROWS, LANES), F32),
        pltpu.VMEM((TILE_ROWS, LANES), F32),
        pltpu.VMEM((TILE_ROWS, D_HEADS * HEAD_PAD), BF16),
        pltpu.VMEM((n_keys_buf, D_HEADS * HEAD_PAD), BF16),
        pltpu.VMEM((n_keys_buf, D_HEADS * HEAD_PAD), BF16),
        pltpu.VMEM((TILE_ROWS, D), BF16),
    ]
    return pl.pallas_call(
        kern,
        grid=(n_tiles,),
        in_specs=in_specs,
        out_specs=out_specs,
        out_shape=out_shape,
        scratch_shapes=scratch,
        compiler_params=pltpu.CompilerParams(
            dimension_semantics=("arbitrary",), vmem_limit_bytes=VMEM_LIMIT),
        name=name,
    )(*args)


def _axial_rope_tables(rows):
    half = QK_ROPE // 2
    inv = ROPE_BASE ** (-jnp.arange(0, half, 2, dtype=F32) / half)
    r = jnp.repeat(jnp.arange(rows, dtype=F32), GRID_W)
    col = jnp.tile(jnp.arange(GRID_W, dtype=F32), rows)
    ang = jnp.concatenate([r[:, None] * inv, col[:, None] * inv], axis=-1)
    cos = jnp.repeat(jnp.cos(ang), 2, axis=-1)
    sin = jnp.repeat(jnp.sin(ang), 2, axis=-1) * jnp.tile(jnp.array([-1.0, 1.0], F32), half)
    n = ang.shape[0]
    pad_lo, pad_hi = KR_LANE, HEAD_PAD - KR_LANE - QK_ROPE
    cos_t = jnp.concatenate([jnp.ones((n, pad_lo), F32), cos, jnp.ones((n, pad_hi), F32)], axis=-1)
    sin_t = jnp.concatenate([jnp.zeros((n, pad_lo), F32), sin, jnp.zeros((n, pad_hi), F32)], axis=-1)
    return cos_t, sin_t


def _block_diag_gates(w):
    per = (W_C // 2) // C_BLOCK
    w4 = w.reshape(2, 2, per, C_BLOCK, C_BLOCK)
    eye = jnp.eye(per, dtype=w.dtype)
    bd = jnp.einsum("dJhij,hk->dJhikj", w4, eye)
    return bd.reshape(2, 2, per * C_BLOCK, per * C_BLOCK)


def kernel(x_prompt, x_sample, cache_mla_ckv, cache_mla_krope, state_rglru, c, c_ctx, w_mod, b_mod, ln1_g, ln1_b, ln2_g, ln2_b, even_w_in, even_w_s, even_b_s, even_w_pool, even_pool_scale, even_w_out, odd_w_in, rg_conv_w, rg_conv_b, rg_w_a, rg_b_a, rg_w_x, rg_b_x, rg_lam, mla_q_g, mla_kv_g, mla_w_uq, mla_w_uk, mla_w_uv, odd_w_out, ffn_w_up, ffn_conv_w, ffn_conv_b, ffn_w_down):
    batch, seq, _ = x_prompt.shape
    dec_batch, dec_seq, _ = x_sample.shape
    assert TILE_ROWS % seq == 0 and dec_seq == TILE_ROWS and dec_batch + 1 <= SUBLANES

    cond = jnp.concatenate([c_ctx[None, :], c, jnp.zeros((SUBLANES - 1 - dec_batch, D), F32)], axis=0)
    mod = _modulation(cond, w_mod, b_mod)

    wup_bf, wdown_bf = _prep_ffn_weights(ffn_w_up, ffn_w_down)
    n_ch = 2 * D_FF // FF_CHUNK
    wf = {"ln_g": ln2_g.reshape(DEPTH, 1, D), "ln_b": ln2_b.reshape(DEPTH, 1, D),
          "w_up": wup_bf, "conv_w": ffn_conv_w.reshape(DEPTH, 3, n_ch, FF_CHUNK),
          "conv_b": ffn_conv_b.reshape(DEPTH, n_ch, FF_CHUNK), "w_down": wdown_bf}
    ln1 = {"ln_g": ln1_g.reshape(DEPTH, 1, D), "ln_b": ln1_b.reshape(DEPTH, 1, D)}

    xp = x_prompt.reshape(batch * seq, D)
    xs = x_sample.reshape(dec_batch * dec_seq, D)
    new_ckv = new_kr = new_state = None
    for l in range(DEPTH):
        if l % 2 == 0:
            e = l // 2
            w = dict(ln1,
                     w_in=even_w_in[e].astype(BF16), w_s=even_w_s[e].astype(BF16),
                     b_s_t=even_b_s[e].T, w_pool=even_w_pool[e].astype(BF16),
                     pool_scale=even_pool_scale[e][None, :], w_out=even_w_out[e].astype(BF16))
            xp = _even_layer(xp, mod, w, l, seq, 0, 0, f"even{l}_ctx")
            xs = _even_layer(xs, mod, w, l, dec_seq, 1, 1, f"even{l}_lat")
        else:
            o = l // 2
            w_in = odd_w_in[o]
            n_main = 2 * W_C + Q_LORA + KV_LORA
            w_in_pad = jnp.concatenate(
                [w_in[:, :n_main], jnp.zeros((D, KR_LANE), F32), w_in[:, n_main:],
                 jnp.zeros((D, HEAD_PAD - KR_LANE - QK_ROPE), F32)], axis=1).astype(BF16)
            w_gate = jnp.stack([_block_diag_gates(rg_w_a[o]), _block_diag_gates(rg_w_x[o])], axis=1)
            b_gate = jnp.stack([rg_b_a[o], rg_b_x[o]], axis=1)
            w_uq = jnp.pad(mla_w_uq[o], ((0, 0), (0, 0), (0, HEAD_PAD - QK_NOPE - QK_ROPE)))
            w_uk = jnp.pad(mla_w_uk[o], ((0, 0), (0, 0), (0, HEAD_PAD - QK_NOPE)))
            uv = mla_w_uv[o].reshape(KV_LORA, D_HEADS // 2, 2, V_DIM)
            zeros_v = jnp.zeros((KV_LORA, D_HEADS // 2, V_DIM), F32)
            w_uv = jnp.stack([jnp.concatenate([uv[:, :, 0], zeros_v], axis=-1),
                              jnp.concatenate([zeros_v, uv[:, :, 1]], axis=-1)], axis=2)
            w = dict(ln1,
                     w_in=w_in_pad, conv_w=rg_conv_w[o], conv_b=rg_conv_b[o][None, :],
                     w_gate=w_gate.astype(BF16), b_gate=b_gate, lam=rg_lam[o],
                     q_g=mla_q_g[o][None, :], kv_g=mla_kv_g[o][None, :],
                     w_uq=w_uq.reshape(Q_LORA, D_HEADS * HEAD_PAD).astype(BF16),
                     w_uk=w_uk.reshape(KV_LORA, D_HEADS * HEAD_PAD).astype(BF16),
                     w_uv=w_uv.reshape(KV_LORA, D_HEADS * HEAD_PAD).astype(BF16),
                     w_out=odd_w_out[o].astype(BF16))
            xp, ckv, kr, st = _odd_layer(xp, mod, w, l, seq, 0, 0, f"odd{l}_ctx")
            new_ckv = ckv.reshape(batch, 1, seq, KV_LORA)
            new_kr = kr.reshape(batch, 1, seq, QK_ROPE)
            new_state = st.reshape(batch, 1, 2, W_C)
            cos_t, sin_t = _axial_rope_tables(dec_seq // GRID_W)
            latent_inputs = {
                "cache_ckv": cache_mla_ckv[:, o],
                "cache_kr": jnp.pad(cache_mla_krope[:, o],
                                    ((0, 0), (0, 0), (KR_LANE, HEAD_PAD - KR_LANE - QK_ROPE))),
                "state": state_rglru[:, o],
                "cos": cos_t, "sin": sin_t,
            }
            (xs,) = _odd_layer(xs, mod, w, l, dec_seq, 1, 1, f"odd{l}_lat", latent_inputs)
        xp = _ffn_layer(xp, mod, wf, l, seq, 0, 0, f"ffn{l}_ctx")
        xs = _ffn_layer(xs, mod, wf, l, dec_seq, 1, 1, f"ffn{l}_lat")
    return (xp.reshape(batch, seq, D), xs.reshape(dec_batch, dec_seq, D), new_ckv, new_kr, new_state)
```

```python
import functools
import math

import jax
import jax.numpy as jnp
import numpy as np
from jax import lax
from jax.experimental import pallas as pl
from jax.experimental.pallas import tpu as pltpu

F32 = jnp.float32
BF16 = jnp.bfloat16

D = 1024
DEPTH = 2
GRID_W = 64
ALPHA = (2 * DEPTH) ** 0.25
LN_EPS = 1e-5
RMS_EPS = 1e-6
A_HEADS = 4
A_WIDTH = 512
A_GROUP_W = A_WIDTH // A_HEADS
CHUNK = 128
POOL_WINDOWS = (2, 4, 8, 16)
B_GROUP = 128
B_WIDTH = 512
W_C = 512
C_HEADS = 8
C_BLOCK = 64
RG_C = 8.0
D_HEADS = 8
Q_LORA = 384
KV_LORA = 256
QK_NOPE = 64
QK_ROPE = 32
V_DIM = 64
ROPE_BASE = 10000.0
ATTN_SCALE = 1.0 / math.sqrt(QK_NOPE + QK_ROPE)
D_FF = 2816

LANES = 128
SUBLANES = 8
HEAD_PAD = LANES
ODD_IN_PAD = 2 * W_C + Q_LORA + KV_LORA + HEAD_PAD
KR_LANE = QK_NOPE
TILE_ROWS = 1024
ROW_BLOCK = 256
FF_CHUNK = 256
FFN_DOT_ROWS = 128
FFN_RB = 64
VMEM_LIMIT = 56 * 1024 * 1024


def _dot(a, b):
    return jnp.dot(a, b, preferred_element_type=F32)


def _gelu(x):
    return x * (0.5 * (1.0 + jnp.tanh(0.7978845608028654 * (x + 0.044715 * (x * x * x)))))


def _layernorm(r, g, b):
    mu = jnp.mean(r, axis=-1, keepdims=True)
    c = r - mu
    var = jnp.mean(c * c, axis=-1, keepdims=True)
    return c * lax.rsqrt(var + LN_EPS) * g + b


def _row_pos(shape, seq_len):
    return lax.broadcasted_iota(jnp.int32, shape, 0) & (seq_len - 1)


def _shift_rows(x, d, pos, seq_len):
    n = x.shape[0]
    y = pltpu.roll(x, (-d) % n, 0)
    valid = (pos < seq_len - d) if d > 0 else (pos >= -d)
    return jnp.where(valid, y, 0.0)


def _mod_kernel(cctx_ref, c_ref, w_ref, b_ref, o_ref, s_ref):
    n_lat = c_ref.shape[0]
    s_ref[...] = jnp.zeros_like(s_ref)
    s_ref[0:1, :] = cctx_ref[...]
    s_ref[1:1 + n_lat, :] = c_ref[...]
    s = s_ref[...]
    s = s * jax.nn.sigmoid(s)
    bias = b_ref[pl.ds(pl.program_id(0), 1), :]
    o_ref[...] = _dot(s.astype(BF16), w_ref[...].astype(BF16)) + bias


def _modulation(c_ctx, c, w_mod, b_mod):
    nc = 1536
    return pl.pallas_call(
        _mod_kernel,
        grid=(DEPTH, 6 * D // nc),
        in_specs=[
            pl.BlockSpec((1, D), lambda l, j: (0, 0)),
            pl.BlockSpec(c.shape, lambda l, j: (0, 0)),
            pl.BlockSpec((None, D, nc), lambda l, j: (l, 0, j)),
            pl.BlockSpec((DEPTH, nc), lambda l, j: (0, j)),
        ],
        out_specs=pl.BlockSpec((None, SUBLANES, nc), lambda l, j: (l, 0, j)),
        out_shape=jax.ShapeDtypeStruct((DEPTH, SUBLANES, 6 * D), F32),
        scratch_shapes=[pltpu.VMEM((SUBLANES, D), F32)],
        compiler_params=pltpu.CompilerParams(
            dimension_semantics=("arbitrary", "arbitrary"), vmem_limit_bytes=VMEM_LIMIT),
        name="modulation",
    )(c_ctx.reshape(1, D), c, w_mod, b_mod)


def _mod_rows(mod_ref, row0, row_step):
    row = row0 + pl.program_id(0) * row_step
    m = mod_ref[pl.ds(row, 1), :]
    return m[:, 0:D], m[:, D:2 * D], m[:, 2 * D:3 * D]


def _const_spec(shape):
    nd = len(shape)
    return pl.BlockSpec(shape, lambda *_: (0,) * nd, pipeline_mode=pl.Buffered(1))


def _layer_spec(shape, l):
    nd = len(shape) - 1
    return pl.BlockSpec((None,) + tuple(shape[1:]), lambda *_: (l,) + (0,) * nd, pipeline_mode=pl.Buffered(1))


def _mod_spec(l, half):
    return pl.BlockSpec((None, SUBLANES, 3 * D), lambda *_: (l, 0, half))


def _even_kernel(x_ref, mod_ref, win_ref, ws_ref, bst_ref, wpool_ref, pscale_ref, wout_ref,
                 lng_ref, lnb_ref, o_ref, u_ref, vn_ref, p_ref, ycat_ref, *, layer, seq_len, row0, row_step):
    T = x_ref.shape[0]
    shift, scale, gate = _mod_rows(mod_ref, row0, row_step)

    def proj(rb, carry):
        rows = pl.ds(pl.multiple_of(rb * ROW_BLOCK, ROW_BLOCK), ROW_BLOCK)
        hb = (x_ref[rows, :] * (1.0 + scale) + shift).astype(BF16)
        u_ref[rows, :] = _gelu(_dot(hb, win_ref[:, 0:A_WIDTH]))
        v = _gelu(_dot(hb, win_ref[:, A_WIDTH:2 * A_WIDTH]))
        mu = jnp.mean(v, axis=-1, keepdims=True)
        c = v - mu
        var = jnp.mean(c * c, axis=-1, keepdims=True)
        vn_ref[rows, :] = (c * lax.rsqrt(var + LN_EPS)).astype(BF16)
        p_ref[rows, :] = _dot(hb, win_ref[:, 2 * A_WIDTH:])
        return carry

    lax.fori_loop(0, T // ROW_BLOCK, proj, 0)

    def gate_chunk(n, carry):
        rows = pl.ds(pl.multiple_of(n * CHUNK, CHUNK), CHUNK)
        for h in range(A_HEADS):
            cols = slice(h * A_GROUP_W, (h + 1) * A_GROUP_W)
            s = _dot(ws_ref[h], vn_ref[rows, cols]) + bst_ref[:, h:h + 1]
            ycat_ref[rows, cols] = (u_ref[rows, cols] * s).astype(BF16)
        return carry

    lax.fori_loop(0, T // CHUNK, gate_chunk, 0)

    pos = _row_pos((T, B_GROUP), seq_len)
    for gi, w in enumerate(POOL_WINDOWS):
        cols = slice(gi * B_GROUP, (gi + 1) * B_GROUP)
        half = w // 2
        pg = p_ref[:, cols]
        fwd = pg
        bwd = _shift_rows(pg, -1, pos, seq_len)
        k = 1
        while k < half:
            fwd = fwd + _shift_rows(fwd, k, pos, seq_len)
            bwd = bwd + _shift_rows(bwd, -k, pos, seq_len)
            k *= 2
        cnt = jnp.minimum(pos + half, seq_len) - jnp.maximum(pos - half, 0)
        pooled = (fwd + bwd) / cnt.astype(F32) - pg
        yb = _dot(pooled.astype(BF16), wpool_ref[gi]) * pscale_ref[:, cols]
        ycat_ref[:, A_WIDTH + gi * B_GROUP:A_WIDTH + (gi + 1) * B_GROUP] = yb.astype(BF16)

    def out(rb, carry):
        rows = pl.ds(pl.multiple_of(rb * ROW_BLOCK, ROW_BLOCK), ROW_BLOCK)
        d = _dot(ycat_ref[rows, :], wout_ref[...])
        r = ALPHA * x_ref[rows, :] + gate * d
        o_ref[rows, :] = _layernorm(r, lng_ref[layer:layer + 1, :], lnb_ref[layer:layer + 1, :])
        return carry

    lax.fori_loop(0, T // ROW_BLOCK, out, 0)


def _even_layer(x, mod, w, l, seq_len, row0, row_step, name):
    n_rows = x.shape[0]
    kern = functools.partial(_even_kernel, layer=l, seq_len=seq_len, row0=row0, row_step=row_step)
    return pl.pallas_call(
        kern,
        grid=(n_rows // TILE_ROWS,),
        in_specs=[
            pl.BlockSpec((TILE_ROWS, D), lambda i: (i, 0)),
            _mod_spec(l, 0),
            _const_spec(w["w_in"].shape),
            _const_spec(w["w_s"].shape),
            _const_spec(w["b_s_t"].shape),
            _const_spec(w["w_pool"].shape),
            _const_spec(w["pool_scale"].shape),
            _const_spec(w["w_out"].shape),
            _const_spec(w["ln_g"].shape),
            _const_spec(w["ln_b"].shape),
        ],
        out_specs=pl.BlockSpec((TILE_ROWS, D), lambda i: (i, 0)),
        out_shape=jax.ShapeDtypeStruct((n_rows, D), F32),
        scratch_shapes=[
            pltpu.VMEM((TILE_ROWS, A_WIDTH), F32),
            pltpu.VMEM((TILE_ROWS, A_WIDTH), BF16),
            pltpu.VMEM((TILE_ROWS, B_WIDTH), F32),
            pltpu.VMEM((TILE_ROWS, D), BF16),
        ],
        compiler_params=pltpu.CompilerParams(
            dimension_semantics=("arbitrary",), vmem_limit_bytes=VMEM_LIMIT),
        name=name,
    )(x, mod, w["w_in"], w["w_s"], w["b_s_t"], w["w_pool"], w["pool_scale"], w["w_out"],
      w["ln_g"], w["ln_b"])


def _conv3_block(z_ref, r0, lanes, cw, cb, seq_len):
    nv = FFN_RB // SUBLANES
    z3 = z_ref[r0:r0 + FFN_RB, lanes].reshape(nv, SUBLANES, LANES)
    sub = lax.broadcasted_iota(jnp.int32, (nv, SUBLANES, LANES), 1)
    down = pltpu.roll(z3, 1, 1)
    up = pltpu.roll(z3, SUBLANES - 1, 1)
    if r0 % seq_len == 0:
        prev = jnp.zeros((1, SUBLANES, LANES), F32)
    else:
        prev = pltpu.roll(z_ref[r0 - SUBLANES:r0, lanes], 1, 0)[None]
    if (r0 + FFN_RB) % seq_len == 0:
        nxt = jnp.zeros((1, SUBLANES, LANES), F32)
    else:
        nxt = pltpu.roll(z_ref[r0 + FFN_RB:r0 + FFN_RB + SUBLANES, lanes], SUBLANES - 1, 0)[None]
    zm = jnp.where(sub == 0, jnp.concatenate([prev, down[:-1]], axis=0), down)
    zp = jnp.where(sub == SUBLANES - 1, jnp.concatenate([up[1:], nxt], axis=0), up)
    y = cw[0][:, lanes] * zm + cw[1][:, lanes] * z3 + cw[2][:, lanes] * zp + cb[:, lanes]
    return y.reshape(FFN_RB, LANES)


def _ffn_kernel(x_ref, mod_ref, wup_ref, cw_ref, cb_ref, wd_ref, lng_ref, lnb_ref, o_ref,
                hb_ref, zg0_ref, zv0_ref, zg1_ref, zv1_ref, a_ref, *, layer, seq_len, row0, row_step):
    T = x_ref.shape[0]
    n_chunks = wd_ref.shape[0]
    shift, scale, gate = _mod_rows(mod_ref, row0, row_step)
    z_bufs = ((zg0_ref, zv0_ref), (zg1_ref, zv1_ref))

    def row_block(rb):
        return pl.ds(pl.multiple_of(rb * ROW_BLOCK, ROW_BLOCK), ROW_BLOCK)

    def modulate(rb, carry):
        rows = row_block(rb)
        hb_ref[rows, :] = (x_ref[rows, :] * (1.0 + scale) + shift).astype(BF16)
        return carry

    lax.fori_loop(0, T // ROW_BLOCK, modulate, 0)

    n_parts = T // FFN_DOT_ROWS
    blocks_per_part = FFN_DOT_ROWS // FFN_RB

    def up_proj(c, bufs, q):
        rows = slice(q * FFN_DOT_ROWS, (q + 1) * FFN_DOT_ROWS)
        bufs[0][rows, :] = _dot(hb_ref[rows, :], wup_ref[c])
        bufs[1][rows, :] = _dot(hb_ref[rows, :], wup_ref[c + n_chunks])

    def gate_blocks(c, bufs, q):
        cwg = [cw_ref[k, pl.ds(c, 1), :] for k in range(3)]
        cwv = [cw_ref[k, pl.ds(c + n_chunks, 1), :] for k in range(3)]
        cbg, cbv = cb_ref[pl.ds(c, 1), :], cb_ref[pl.ds(c + n_chunks, 1), :]
        for rb in range(q * blocks_per_part, (q + 1) * blocks_per_part):
            r0 = rb * FFN_RB
            for lt in range(FF_CHUNK // LANES):
                lanes = slice(lt * LANES, (lt + 1) * LANES)
                g = _conv3_block(bufs[0], r0, lanes, cwg, cbg, seq_len)
                v = _conv3_block(bufs[1], r0, lanes, cwv, cbv, seq_len)
                a_ref[c, r0:r0 + FFN_RB, lanes] = (g * jax.nn.sigmoid(g) * v).astype(BF16)

    for q in range(n_parts):
        up_proj(0, z_bufs[0], q)

    def chunk_pair(k, carry):
        c = 2 * k
        for q in range(n_parts):
            up_proj(c + 1, z_bufs[1], q)
            gate_blocks(c, z_bufs[0], q)
        for q in range(n_parts):
            up_proj(c + 2, z_bufs[0], q)
            gate_blocks(c + 1, z_bufs[1], q)
        return carry

    assert n_chunks % 2 == 1
    lax.fori_loop(0, n_chunks // 2, chunk_pair, 0)
    for q in range(n_parts):
        gate_blocks(n_chunks - 1, z_bufs[0], q)

    def out(rb, carry):
        rows = row_block(rb)
        acc = _dot(a_ref[0, rows, :], wd_ref[0])
        for c in range(1, n_chunks):
            acc = acc + _dot(a_ref[c, rows, :], wd_ref[c])
        r = ALPHA * x_ref[rows, :] + gate * acc
        o_ref[rows, :] = _layernorm(r, lng_ref[layer:layer + 1, :], lnb_ref[layer:layer + 1, :])
        return carry

    lax.fori_loop(0, T // ROW_BLOCK, out, 0)


def _ffn_layer(x, mod, w, l, seq_len, row0, row_step, name):
    n_rows = x.shape[0]
    n_chunks = D_FF // FF_CHUNK
    kern = functools.partial(_ffn_kernel, layer=l, seq_len=seq_len, row0=row0, row_step=row_step)
    params = [w["w_up"], w["conv_w"], w["conv_b"], w["w_down"]]
    return pl.pallas_call(
        kern,
        grid=(n_rows // TILE_ROWS,),
        in_specs=[pl.BlockSpec((TILE_ROWS, D), lambda i: (i, 0)), _mod_spec(l, 1)]
        + [_layer_spec(a.shape, l) for a in params]
        + [_const_spec(w["ln_g"].shape), _const_spec(w["ln_b"].shape)],
        out_specs=pl.BlockSpec((TILE_ROWS, D), lambda i: (i, 0)),
        out_shape=jax.ShapeDtypeStruct((n_rows, D), F32),
        scratch_shapes=[
            pltpu.VMEM((TILE_ROWS, D), BF16),
            pltpu.VMEM((TILE_ROWS, FF_CHUNK), F32),
            pltpu.VMEM((TILE_ROWS, FF_CHUNK), F32),
            pltpu.VMEM((TILE_ROWS, FF_CHUNK), F32),
            pltpu.VMEM((TILE_ROWS, FF_CHUNK), F32),
            pltpu.VMEM((n_chunks, TILE_ROWS, FF_CHUNK), BF16),
        ],
        compiler_params=pltpu.CompilerParams(
            dimension_semantics=("arbitrary",), vmem_limit_bytes=VMEM_LIMIT),
        name=name,
    )(x, mod, *params, w["ln_g"], w["ln_b"])


def _prep_up_kernel(w_ref, o_ref):
    for k in range(o_ref.shape[0]):
        o_ref[k] = w_ref[:, k * FF_CHUNK:(k + 1) * FF_CHUNK].astype(BF16)


def _prep_down_kernel(w_ref, o_ref):
    o_ref[...] = w_ref[...].astype(BF16)


def _prep_ffn_weights(w_up, w_down):
    n_ch = 2 * D_FF // FF_CHUNK
    up = pl.pallas_call(
        _prep_up_kernel,
        grid=(DEPTH, 2),
        in_specs=[pl.BlockSpec((None, D, D_FF), lambda l, j: (l, 0, j))],
        out_specs=pl.BlockSpec((None, n_ch // 2, D, FF_CHUNK), lambda l, j: (l, j, 0, 0)),
        out_shape=jax.ShapeDtypeStruct((DEPTH, n_ch, D, FF_CHUNK), BF16),
        compiler_params=pltpu.CompilerParams(
            dimension_semantics=("arbitrary", "arbitrary"), vmem_limit_bytes=VMEM_LIMIT),
        name="ffn_up_bf16",
    )(w_up)
    half = D_FF // 2
    down = pl.pallas_call(
        _prep_down_kernel,
        grid=(DEPTH, 2),
        in_specs=[pl.BlockSpec((None, half, D), lambda l, j: (l, j, 0))],
        out_specs=pl.BlockSpec((None, half, D), lambda l, j: (l, j, 0)),
        out_shape=jax.ShapeDtypeStruct((DEPTH, D_FF, D), BF16),
        compiler_params=pltpu.CompilerParams(
            dimension_semantics=("arbitrary", "arbitrary"), vmem_limit_bytes=VMEM_LIMIT),
        name="ffn_down_bf16",
    )(w_down)
    return up, down.reshape(DEPTH, n_ch // 2, FF_CHUNK, D)


def _scan_groups(a, b, sub, reverse):
    n = a.shape[0]
    for d in (1, 2, 4):
        if reverse:
            valid, sh = sub < SUBLANES - d, n - d
        else:
            valid, sh = sub >= d, d
        a_prev = jnp.where(valid, pltpu.roll(a, sh, 0), 1.0)
        b_prev = jnp.where(valid, pltpu.roll(b, sh, 0), 0.0)
        b = b + a * b_prev
        a = a * a_prev
    return a, b


def _scan_ends(a, b, posg, n_groups, reverse):
    n = a.shape[0]
    d = 1
    while d < n_groups:
        if reverse:
            valid, sh = posg < n_groups - d, n - d
        else:
            valid, sh = posg >= d, d
        a_prev = jnp.where(valid, pltpu.roll(a, sh, 0), 1.0)
        b_prev = jnp.where(valid, pltpu.roll(b, sh, 0), 0.0)
        b = b + a * b_prev
        a = a * a_prev
        d *= 2
    return a, b


def _rope_pairs(x, cos, sin_signed, even_lane):
    n = x.shape[1]
    swapped = jnp.where(even_lane, pltpu.roll(x, n - 1, 1), pltpu.roll(x, 1, 1))
    return x * cos + swapped * sin_signed


def _rmsnorm(x, g):
    return x * lax.rsqrt(jnp.mean(x * x, axis=-1, keepdims=True) + RMS_EPS) * g


def _odd_kernel(*refs, layer, seq_len, latent, past_len, row0, row_step):
    if latent:
        (x_ref, mod_ref, win_ref, cw_ref, cb_ref, wgate_ref, bgate_ref, lam_ref, qg_ref, kvg_ref,
         wuq_ref, wuk_ref, wuv_ref, wout_ref, lng_ref, lnb_ref,
         cckv_ref, ckr_ref, st_ref, cos_ref, sin_ref,
         o_ref,
         zrg_ref, xc_ref, a_ref, b_ref, hrep_ref, q_ref, k_ref, v_ref, ycat_ref) = refs
    else:
        (x_ref, mod_ref, win_ref, cw_ref, cb_ref, wgate_ref, bgate_ref, lam_ref, qg_ref, kvg_ref,
         wuq_ref, wuk_ref, wuv_ref, wout_ref, lng_ref, lnb_ref,
         o_ref, ckv_out_ref, kr_out_ref, st_out_ref,
         zrg_ref, xc_ref, a_ref, b_ref, hrep_ref, q_ref, k_ref, v_ref, ycat_ref) = refs
    T = x_ref.shape[0]
    n_seq = T // seq_len
    shift, scale, gate = _mod_rows(mod_ref, row0, row_step)
    o_q = 2 * W_C
    o_kv = o_q + Q_LORA
    o_kr = o_kv + KV_LORA
    pair_w = 2 * HEAD_PAD
    even_lane = (lax.broadcasted_iota(jnp.int32, (ROW_BLOCK, HEAD_PAD), 1) & 1) == 0

    def row_block(rb):
        return pl.ds(pl.multiple_of(rb * ROW_BLOCK, ROW_BLOCK), ROW_BLOCK)

    def proj(rb, carry):
        rows = row_block(rb)
        k_rows = pl.ds(pl.multiple_of(past_len + rb * ROW_BLOCK, ROW_BLOCK), ROW_BLOCK)
        hb = (x_ref[rows, :] * (1.0 + scale) + shift).astype(BF16)
        zrg_ref[rows, :] = _dot(hb, win_ref[:, 0:o_q])
        qn = _rmsnorm(_dot(hb, win_ref[:, o_q:o_kv]), qg_ref[...]).astype(BF16)
        ckv = _rmsnorm(_dot(hb, win_ref[:, o_kv:o_kr]), kvg_ref[...])
        kr = _dot(hb, win_ref[:, o_kr:o_kr + HEAD_PAD])
        if latent:
            cos = cos_ref[rows, :]
            sin = sin_ref[rows, :]
            kr = _rope_pairs(kr, cos, sin, even_lane)
        else:
            ckv_out_ref[rows, :] = ckv
            kr_out_ref[rows, :] = kr[:, KR_LANE:KR_LANE + QK_ROPE]
        ckvb = ckv.astype(BF16)
        v_ref[k_rows, :] = _dot(ckvb, wuv_ref[...]).astype(BF16)
        for pair in range(D_HEADS // 2):
            pcols = slice(pair * pair_w, (pair + 1) * pair_w)
            q2 = _dot(qn, wuq_ref[:, pcols])
            k2 = _dot(ckvb, wuk_ref[:, pcols])
            for hh in range(2):
                cols = slice(pair * pair_w + hh * HEAD_PAD, pair * pair_w + (hh + 1) * HEAD_PAD)
                qh = q2[:, hh * HEAD_PAD:(hh + 1) * HEAD_PAD]
                if latent:
                    qh = _rope_pairs(qh, cos, sin, even_lane)
                q_ref[rows, cols] = qh.astype(BF16)
                k_ref[k_rows, cols] = (k2[:, hh * HEAD_PAD:(hh + 1) * HEAD_PAD] + kr).astype(BF16)
        return carry

    lax.fori_loop(0, T // ROW_BLOCK, proj, 0)

    if latent:
        def cached(cbk, carry):
            rows = row_block(cbk)
            cckv = cckv_ref[rows, :].astype(BF16)
            ckr = ckr_ref[rows, :]
            v_ref[rows, :] = _dot(cckv, wuv_ref[...]).astype(BF16)
            for pair in range(D_HEADS // 2):
                pcols = slice(pair * pair_w, (pair + 1) * pair_w)
                k2 = _dot(cckv, wuk_ref[:, pcols])
                for hh in range(2):
                    cols = slice(pair * pair_w + hh * HEAD_PAD, pair * pair_w + (hh + 1) * HEAD_PAD)
                    k_ref[rows, cols] = (k2[:, hh * HEAD_PAD:(hh + 1) * HEAD_PAD] + ckr).astype(BF16)
            return carry

        lax.fori_loop(0, past_len // ROW_BLOCK, cached, 0)

    pos_l = _row_pos((T, LANES), seq_len)
    cw = cw_ref[...]
    for cbk in range(W_C // LANES):
        cols = slice(cbk * LANES, (cbk + 1) * LANES)
        xr = zrg_ref[:, cols]
        xc_ref[:, cols] = (cw[0:1, cols] * _shift_rows(xr, -1, pos_l, seq_len) + cw[1:2, cols] * xr
                           + cw[2:3, cols] * _shift_rows(xr, 1, pos_l, seq_len)
                           + cw[3:4, cols] * _shift_rows(xr, 2, pos_l, seq_len) + cb_ref[:, cols])

    half_w = W_C // 2
    n_groups = seq_len // SUBLANES
    n_ends = T // SUBLANES
    sub = lax.broadcasted_iota(jnp.int32, (T, LANES), 0) & (SUBLANES - 1)
    posg = lax.broadcasted_iota(jnp.int32, (n_ends, LANES), 0) & (n_groups - 1)
    for direction in range(2):
        reverse = direction == 1
        neg = -lam_ref[direction:direction + 1, :]
        softplus = jnp.maximum(neg, 0.0) + jnp.log1p(jnp.exp(-jnp.abs(neg)))

        def gates(rb, carry, direction=direction, softplus=softplus):
            rows = row_block(rb)
            for j in range(2):
                cols = slice(j * half_w, (j + 1) * half_w)
                xcj = xc_ref[rows, cols]
                xcb = xcj.astype(BF16)
                r = jax.nn.sigmoid(_dot(xcb, wgate_ref[direction, 0, j]) + bgate_ref[direction, 0:1, cols])
                i = jax.nn.sigmoid(_dot(xcb, wgate_ref[direction, 1, j]) + bgate_ref[direction, 1:2, cols])
                log_a = -RG_C * r * softplus[:, cols]
                a = jnp.exp(log_a)
                b = jnp.sqrt(jnp.tanh(-log_a) * (a * a + 1.0)) * i * xcj
                for k in range(half_w // LANES):
                    cbk = j * (half_w // LANES) + k
                    a_ref[cbk, rows, :] = a[:, k * LANES:(k + 1) * LANES]
                    b_ref[cbk, rows, :] = b[:, k * LANES:(k + 1) * LANES]
            return carry

        lax.fori_loop(0, T // ROW_BLOCK, gates, 0)

        end_row = 0 if reverse else SUBLANES - 1
        for cbk in range(W_C // LANES):
            cols = slice(cbk * LANES, (cbk + 1) * LANES)
            a_loc, b_loc = _scan_groups(a_ref[cbk], b_ref[cbk], sub, reverse)
            a_ref[cbk] = a_loc
            b_ref[cbk] = b_loc
            a_end = a_ref[cbk, pl.ds(end_row, n_ends, stride=SUBLANES), :]
            b_end = b_ref[cbk, pl.ds(end_row, n_ends, stride=SUBLANES), :]
            a_tot, b_tot = _scan_ends(a_end, b_end, posg, n_groups, reverse)
            if latent:
                h0 = st_ref[direction:direction + 1, cols]
                ends = a_tot * h0 + b_tot
            else:
                h0 = jnp.zeros((1, LANES), F32)
                ends = b_tot
            if reverse:
                carry_in = jnp.where(posg < n_groups - 1, pltpu.roll(ends, n_ends - 1, 0), h0)
            else:
                carry_in = jnp.where(posg >= 1, pltpu.roll(ends, 1, 0), h0)
            for rr in range(SUBLANES):
                hrep_ref[pl.ds(rr, n_ends, stride=SUBLANES), :] = carry_in
            h = a_ref[cbk] * hrep_ref[...] + b_ref[cbk]
            if reverse:
                zrg_ref[:, cols] = zrg_ref[:, cols] + h
            else:
                zrg_ref[:, cols] = h
            if not latent:
                for s in range(n_seq):
                    g_last = s * n_groups + (0 if reverse else n_groups - 1)
                    st_out_ref[s, direction:direction + 1, cols] = ends[g_last:g_last + 1, :]

    def mix(rb, carry):
        rows = row_block(rb)
        ycat_ref[rows, 0:W_C] = (zrg_ref[rows, 0:W_C] * _gelu(zrg_ref[rows, W_C:2 * W_C])).astype(BF16)
        return carry

    lax.fori_loop(0, T // ROW_BLOCK, mix, 0)

    n_keys = past_len + seq_len

    def attend(qb, carry):
        q_rows = row_block(qb)
        if latent:
            k_rows = slice(0, n_keys)
        else:
            k_rows = pl.ds(pl.multiple_of((qb * ROW_BLOCK // seq_len) * seq_len, seq_len), seq_len)
        for pair in range(D_HEADS // 2):
            o_pair = None
            for h in (2 * pair, 2 * pair + 1):
                cols = slice(h * HEAD_PAD, (h + 1) * HEAD_PAD)
                s = lax.dot_general(q_ref[q_rows, cols], k_ref[k_rows, cols],
                                    (((1,), (1,)), ((), ())), preferred_element_type=F32) * ATTN_SCALE
                m = jnp.max(s, axis=-1, keepdims=True)
                p = jnp.exp(s - m)
                l = jnp.sum(p, axis=-1, keepdims=True)
                o = _dot(p.astype(BF16), v_ref[k_rows, cols]) / l
                o_pair = o if o_pair is None else o_pair + o
            ycat_ref[q_rows, W_C + pair * HEAD_PAD:W_C + (pair + 1) * HEAD_PAD] = o_pair.astype(BF16)
        return carry

    lax.fori_loop(0, T // ROW_BLOCK, attend, 0)

    def out(rb, carry):
        rows = row_block(rb)
        d = _dot(ycat_ref[rows, :], wout_ref[...])
        r = ALPHA * x_ref[rows, :] + gate * d
        o_ref[rows, :] = _layernorm(r, lng_ref[layer:layer + 1, :], lnb_ref[layer:layer + 1, :])
        return carry

    lax.fori_loop(0, T // ROW_BLOCK, out, 0)


def _odd_layer(x, mod, w, l, seq_len, row0, row_step, name, latent_inputs=None):
    n_rows = x.shape[0]
    n_tiles = n_rows // TILE_ROWS
    latent = latent_inputs is not None
    past_len = latent_inputs["cache_ckv"].shape[1] if latent else 0
    n_seq = TILE_ROWS // seq_len
    kern = functools.partial(_odd_kernel, layer=l, seq_len=seq_len, latent=latent, past_len=past_len,
                             row0=row0, row_step=row_step)
    weights = [w["w_in"], w["conv_w"], w["conv_b"], w["w_gate"], w["b_gate"], w["lam"], w["q_g"], w["kv_g"],
               w["w_uq"], w["w_uk"], w["w_uv"], w["w_out"]]
    in_specs = [pl.BlockSpec((TILE_ROWS, D), lambda i: (i, 0), pipeline_mode=pl.Buffered(1)),
                _mod_spec(l, 0)]
    in_specs += [_const_spec(a.shape) for a in weights]
    in_specs += [_const_spec(w["ln_g"].shape), _const_spec(w["ln_b"].shape)]
    args = [x, mod] + weights + [w["ln_g"], w["ln_b"]]
    out_specs = [pl.BlockSpec((TILE_ROWS, D), lambda i: (i, 0), pipeline_mode=pl.Buffered(1))]
    out_shape = [jax.ShapeDtypeStruct((n_rows, D), F32)]
    if latent:
        li = latent_inputs
        in_specs += [
            pl.BlockSpec((None, past_len, KV_LORA), lambda i: (i, 0, 0)),
            pl.BlockSpec((None, past_len, HEAD_PAD), lambda i: (i, 0, 0)),
            pl.BlockSpec((None, 2, W_C), lambda i: (i, 0, 0)),
            _const_spec(li["cos"].shape),
            _const_spec(li["sin"].shape),
        ]
        args += [li["cache_ckv"], li["cache_kr"], li["state"], li["cos"], li["sin"]]
    else:
        out_specs += [
            pl.BlockSpec((TILE_ROWS, KV_LORA), lambda i: (i, 0)),
            pl.BlockSpec((TILE_ROWS, QK_ROPE), lambda i: (i, 0)),
            pl.BlockSpec((n_seq, 2, W_C), lambda i: (i, 0, 0)),
        ]
        out_shape += [
            jax.ShapeDtypeStruct((n_rows, KV_LORA), F32),
            jax.ShapeDtypeStruct((n_rows, QK_ROPE), F32),
            jax.ShapeDtypeStruct((n_rows // seq_len, 2, W_C), F32),
        ]
    n_keys_buf = past_len + TILE_ROWS
    scratch = [
        pltpu.VMEM((TILE_ROWS, 2 * W_C), F32),
        pltpu.VMEM((TILE_ROWS, W_C), F32),
        pltpu.VMEM((W_C // LANES, TILE_ROWS, LANES), F32),
        pltpu.VMEM((W_C // LANES, TILE_ROWS, LANES), F32),
        pltpu.VMEM((TILE_ROWS, LANES), F32),
        pltpu.VMEM((TILE_ROWS, D_HEADS * HEAD_PAD), BF16),
        pltpu.VMEM((n_keys_buf, D_HEADS * HEAD_PAD), BF16),
        pltpu.VMEM((n_keys_buf, D_HEADS * HEAD_PAD), BF16),
        pltpu.VMEM((TILE_ROWS, D), BF16),
    ]
    return pl.pallas_call(
        kern,
        grid=(n_tiles,),
        in_specs=in_specs,
        out_specs=out_specs,
        out_shape=out_shape,
        scratch_shapes=scratch,
        compiler_params=pltpu.CompilerParams(
            dimension_semantics=("arbitrary",), vmem_limit_bytes=VMEM_LIMIT),
        name=name,
    )(*args)


def _axial_rope_tables(rows):
    half = QK_ROPE // 2
    inv = (ROPE_BASE ** (-np.arange(0, half, 2, dtype=np.float32) / half)).astype(np.float32)
    r = np.repeat(np.arange(rows, dtype=np.float32), GRID_W)
    col = np.tile(np.arange(GRID_W, dtype=np.float32), rows)
    ang = np.concatenate([r[:, None] * inv, col[:, None] * inv], axis=-1).astype(np.float32)
    cos = np.repeat(np.cos(ang), 2, axis=-1)
    sin = np.repeat(np.sin(ang), 2, axis=-1) * np.tile(np.array([-1.0, 1.0], np.float32), half)
    n = ang.shape[0]
    cos_t = np.ones((n, HEAD_PAD), np.float32)
    sin_t = np.zeros((n, HEAD_PAD), np.float32)
    cos_t[:, KR_LANE:KR_LANE + QK_ROPE] = cos
    sin_t[:, KR_LANE:KR_LANE + QK_ROPE] = sin
    return jnp.asarray(cos_t), jnp.asarray(sin_t)


def _block_diag_gates(w):
    per = (W_C // 2) // C_BLOCK
    w4 = w.reshape(2, 2, per, C_BLOCK, C_BLOCK)
    eye = jnp.eye(per, dtype=w.dtype)
    bd = jnp.einsum("dJhij,hk->dJhikj", w4, eye)
    return bd.reshape(2, 2, per * C_BLOCK, per * C_BLOCK)


def kernel(x_prompt, x_sample, cache_mla_ckv, cache_mla_krope, state_rglru, c, c_ctx, w_mod, b_mod, ln1_g, ln1_b, ln2_g, ln2_b, even_w_in, even_w_s, even_b_s, even_w_pool, even_pool_scale, even_w_out, odd_w_in, rg_conv_w, rg_conv_b, rg_w_a, rg_b_a, rg_w_x, rg_b_x, rg_lam, mla_q_g, mla_kv_g, mla_w_uq, mla_w_uk, mla_w_uv, odd_w_out, ffn_w_up, ffn_conv_w, ffn_conv_b, ffn_w_down):
    batch, seq, _ = x_prompt.shape
    dec_batch, dec_seq, _ = x_sample.shape
    assert TILE_ROWS % seq == 0 and dec_seq == TILE_ROWS and dec_batch + 1 <= SUBLANES

    mod = _modulation(c_ctx, c, w_mod, b_mod)

    wup_bf, wdown_bf = _prep_ffn_weights(ffn_w_up, ffn_w_down)
    n_ch = 2 * D_FF // FF_CHUNK
    wf = {"ln_g": ln2_g, "ln_b": ln2_b,
          "w_up": wup_bf, "conv_w": ffn_conv_w.reshape(DEPTH, 3, n_ch, FF_CHUNK),
          "conv_b": ffn_conv_b.reshape(DEPTH, n_ch, FF_CHUNK), "w_down": wdown_bf}
    ln1 = {"ln_g": ln1_g, "ln_b": ln1_b}

    xp = x_prompt.reshape(batch * seq, D)
    xs = x_sample.reshape(dec_batch * dec_seq, D)
    new_ckv = new_kr = new_state = None
    for l in range(DEPTH):
        if l % 2 == 0:
            e = l // 2
            w = dict(ln1,
                     w_in=even_w_in[e].astype(BF16), w_s=even_w_s[e].astype(BF16),
                     b_s_t=even_b_s[e].T, w_pool=even_w_pool[e].astype(BF16),
                     pool_scale=even_pool_scale[e][None, :], w_out=even_w_out[e].astype(BF16))
            xp = _even_layer(xp, mod, w, l, seq, 0, 0, f"even{l}_ctx")
            xs = _even_layer(xs, mod, w, l, dec_seq, 1, 1, f"even{l}_lat")
        else:
            o = l // 2
            w_in = odd_w_in[o]
            n_main = 2 * W_C + Q_LORA + KV_LORA
            w_in_pad = jnp.concatenate(
                [w_in[:, :n_main], jnp.zeros((D, KR_LANE), F32), w_in[:, n_main:],
                 jnp.zeros((D, HEAD_PAD - KR_LANE - QK_ROPE), F32)], axis=1).astype(BF16)
            w_gate = jnp.stack([_block_diag_gates(rg_w_a[o]), _block_diag_gates(rg_w_x[o])], axis=1)
            b_gate = jnp.stack([rg_b_a[o], rg_b_x[o]], axis=1)
            w_uq = jnp.pad(mla_w_uq[o], ((0, 0), (0, 0), (0, HEAD_PAD - QK_NOPE - QK_ROPE)))
            w_uk = jnp.pad(mla_w_uk[o], ((0, 0), (0, 0), (0, HEAD_PAD - QK_NOPE)))
            uv = mla_w_uv[o].reshape(KV_LORA, D_HEADS // 2, 2, V_DIM)
            zeros_v = jnp.zeros((KV_LORA, D_HEADS // 2, V_DIM), F32)
            w_uv = jnp.stack([jnp.concatenate([uv[:, :, 0], zeros_v], axis=-1),
                              jnp.concatenate([zeros_v, uv[:, :, 1]], axis=-1)], axis=2)
            w = dict(ln1,
                     w_in=w_in_pad, conv_w=rg_conv_w[o], conv_b=rg_conv_b[o][None, :],
                     w_gate=w_gate.astype(BF16), b_gate=b_gate, lam=rg_lam[o],
                     q_g=mla_q_g[o][None, :], kv_g=mla_kv_g[o][None, :],
                     w_uq=w_uq.reshape(Q_LORA, D_HEADS * HEAD_PAD).astype(BF16),
                     w_uk=w_uk.reshape(KV_LORA, D_HEADS * HEAD_PAD).astype(BF16),
                     w_uv=w_uv.reshape(KV_LORA, D_HEADS * HEAD_PAD).astype(BF16),
                     w_out=odd_w_out[o].astype(BF16))
            xp, ckv, kr, st = _odd_layer(xp, mod, w, l, seq, 0, 0, f"odd{l}_ctx")
            new_ckv = ckv.reshape(batch, 1, seq, KV_LORA)
            new_kr = kr.reshape(batch, 1, seq, QK_ROPE)
            new_state = st.reshape(batch, 1, 2, W_C)
            cos_t, sin_t = _axial_rope_tables(dec_seq // GRID_W)
            latent_inputs = {
                "cache_ckv": cache_mla_ckv[:, o],
                "cache_kr": jnp.pad(cache_mla_krope[:, o],
                                    ((0, 0), (0, 0), (KR_LANE, HEAD_PAD - KR_LANE - QK_ROPE))),
                "state": state_rglru[:, o],
                "cos": cos_t, "sin": sin_t,
            }
            (xs,) = _odd_layer(xs, mod, w, l, dec_seq, 1, 1, f"odd{l}_lat", latent_inputs)
        xp = _ffn_layer(xp, mod, wf, l, seq, 0, 0, f"ffn{l}_ctx")
        xs = _ffn_layer(xs, mod, wf, l, dec_seq, 1, 1, f"ffn{l}_lat")
    return (xp.reshape(batch, seq, D), xs.reshape(dec_batch, dec_seq, D), new_ckv, new_kr, new_state)
```

```python
import functools
import math

import jax
import jax.numpy as jnp
import numpy as np
from jax import lax
from jax.experimental import pallas as pl
from jax.experimental.pallas import tpu as pltpu

F32 = jnp.float32
BF16 = jnp.bfloat16

D = 1024
DEPTH = 2
GRID_W = 64
ALPHA = (2 * DEPTH) ** 0.25
LN_EPS = 1e-5
RMS_EPS = 1e-6
A_HEADS = 4
A_WIDTH = 512
A_GROUP_W = A_WIDTH // A_HEADS
CHUNK = 128
POOL_WINDOWS = (2, 4, 8, 16)
B_GROUP = 128
B_WIDTH = 512
W_C = 512
C_HEADS = 8
C_BLOCK = 64
RG_C = 8.0
D_HEADS = 8
Q_LORA = 384
KV_LORA = 256
QK_NOPE = 64
QK_ROPE = 32
V_DIM = 64
ROPE_BASE = 10000.0
ATTN_SCALE = 1.0 / math.sqrt(QK_NOPE + QK_ROPE)
D_FF = 2816

LANES = 128
SUBLANES = 8
HEAD_PAD = LANES
ODD_IN_PAD = 2 * W_C + Q_LORA + KV_LORA + HEAD_PAD
KR_LANE = QK_NOPE
TILE_ROWS = 1024
ROW_BLOCK = 256
FF_CHUNK = 256
FFN_DOT_ROWS = 128
FFN_RB = 64
ODD_RB = 64
VMEM_LIMIT = 58 * 1024 * 1024


def _dot(a, b):
    return jnp.dot(a, b, preferred_element_type=F32)


def _gelu(x):
    return x * (0.5 * (1.0 + jnp.tanh(0.7978845608028654 * (x + 0.044715 * (x * x * x)))))


def _layernorm(r, g, b):
    mu = jnp.mean(r, axis=-1, keepdims=True)
    c = r - mu
    var = jnp.mean(c * c, axis=-1, keepdims=True)
    return c * lax.rsqrt(var + LN_EPS) * g + b


def _row_pos(shape, seq_len):
    return lax.broadcasted_iota(jnp.int32, shape, 0) & (seq_len - 1)


def _shift_rows(x, d, pos, seq_len):
    n = x.shape[0]
    y = pltpu.roll(x, (-d) % n, 0)
    valid = (pos < seq_len - d) if d > 0 else (pos >= -d)
    return jnp.where(valid, y, 0.0)


def _mod_kernel(cctx_ref, c_ref, w_ref, b_ref, o_ref, s_ref):
    n_lat = c_ref.shape[0]
    s_ref[...] = jnp.zeros_like(s_ref)
    s_ref[0:1, :] = cctx_ref[...]
    s_ref[1:1 + n_lat, :] = c_ref[...]
    s = s_ref[...]
    s = s * jax.nn.sigmoid(s)
    bias = b_ref[pl.ds(pl.program_id(0), 1), :]
    o_ref[...] = _dot(s.astype(BF16), w_ref[...].astype(BF16)) + bias


def _modulation(c_ctx, c, w_mod, b_mod):
    nc = 1536
    return pl.pallas_call(
        _mod_kernel,
        grid=(DEPTH, 6 * D // nc),
        in_specs=[
            pl.BlockSpec((1, D), lambda l, j: (0, 0)),
            pl.BlockSpec(c.shape, lambda l, j: (0, 0)),
            pl.BlockSpec((None, D, nc), lambda l, j: (l, 0, j)),
            pl.BlockSpec((DEPTH, nc), lambda l, j: (0, j)),
        ],
        out_specs=pl.BlockSpec((None, SUBLANES, nc), lambda l, j: (l, 0, j)),
        out_shape=jax.ShapeDtypeStruct((DEPTH, SUBLANES, 6 * D), F32),
        scratch_shapes=[pltpu.VMEM((SUBLANES, D), F32)],
        compiler_params=pltpu.CompilerParams(
            dimension_semantics=("arbitrary", "arbitrary"), vmem_limit_bytes=VMEM_LIMIT),
        name="modulation",
    )(c_ctx.reshape(1, D), c, w_mod, b_mod)


def _mod_rows(mod_ref, row0, row_step):
    row = row0 + pl.program_id(0) * row_step
    m = mod_ref[pl.ds(row, 1), :]
    return m[:, 0:D], m[:, D:2 * D], m[:, 2 * D:3 * D]


def _const_spec(shape):
    nd = len(shape)
    return pl.BlockSpec(shape, lambda *_: (0,) * nd, pipeline_mode=pl.Buffered(1))


def _layer_spec(shape, l):
    nd = len(shape) - 1
    return pl.BlockSpec((None,) + tuple(shape[1:]), lambda *_: (l,) + (0,) * nd, pipeline_mode=pl.Buffered(1))


def _mod_spec(l, half):
    return pl.BlockSpec((None, SUBLANES, 3 * D), lambda *_: (l, 0, half))


def _even_kernel(x_ref, mod_ref, win_ref, ws_ref, bst_ref, wpool_ref, pscale_ref, wout_ref,
                 lng_ref, lnb_ref, o_ref, u_ref, vn_ref, p_ref, ycat_ref, *, layer, seq_len, row0, row_step):
    T = x_ref.shape[0]
    shift, scale, gate = _mod_rows(mod_ref, row0, row_step)

    def proj(rb, carry):
        rows = pl.ds(pl.multiple_of(rb * ROW_BLOCK, ROW_BLOCK), ROW_BLOCK)
        hb = (x_ref[rows, :] * (1.0 + scale) + shift).astype(BF16)
        u_ref[rows, :] = _gelu(_dot(hb, win_ref[:, 0:A_WIDTH]))
        v = _gelu(_dot(hb, win_ref[:, A_WIDTH:2 * A_WIDTH]))
        mu = jnp.mean(v, axis=-1, keepdims=True)
        c = v - mu
        var = jnp.mean(c * c, axis=-1, keepdims=True)
        vn_ref[rows, :] = (c * lax.rsqrt(var + LN_EPS)).astype(BF16)
        p_ref[rows, :] = _dot(hb, win_ref[:, 2 * A_WIDTH:])
        return carry

    lax.fori_loop(0, T // ROW_BLOCK, proj, 0)

    def gate_chunk(n, carry):
        rows = pl.ds(pl.multiple_of(n * CHUNK, CHUNK), CHUNK)
        for h in range(A_HEADS):
            cols = slice(h * A_GROUP_W, (h + 1) * A_GROUP_W)
            s = _dot(ws_ref[h], vn_ref[rows, cols]) + bst_ref[:, h:h + 1]
            ycat_ref[rows, cols] = (u_ref[rows, cols] * s).astype(BF16)
        return carry

    lax.fori_loop(0, T // CHUNK, gate_chunk, 0)

    pos = _row_pos((T, B_GROUP), seq_len)
    for gi, w in enumerate(POOL_WINDOWS):
        cols = slice(gi * B_GROUP, (gi + 1) * B_GROUP)
        half = w // 2
        pg = p_ref[:, cols]
        fwd = pg
        bwd = _shift_rows(pg, -1, pos, seq_len)
        k = 1
        while k < half:
            fwd = fwd + _shift_rows(fwd, k, pos, seq_len)
            bwd = bwd + _shift_rows(bwd, -k, pos, seq_len)
            k *= 2
        cnt = jnp.minimum(pos + half, seq_len) - jnp.maximum(pos - half, 0)
        pooled = (fwd + bwd) / cnt.astype(F32) - pg
        yb = _dot(pooled.astype(BF16), wpool_ref[gi]) * pscale_ref[:, cols]
        ycat_ref[:, A_WIDTH + gi * B_GROUP:A_WIDTH + (gi + 1) * B_GROUP] = yb.astype(BF16)

    def out(rb, carry):
        rows = pl.ds(pl.multiple_of(rb * ROW_BLOCK, ROW_BLOCK), ROW_BLOCK)
        d = _dot(ycat_ref[rows, :], wout_ref[...])
        r = ALPHA * x_ref[rows, :] + gate * d
        o_ref[rows, :] = _layernorm(r, lng_ref[layer:layer + 1, :], lnb_ref[layer:layer + 1, :])
        return carry

    lax.fori_loop(0, T // ROW_BLOCK, out, 0)


def _even_layer(x, mod, w, l, seq_len, row0, row_step, name):
    n_rows = x.shape[0]
    kern = functools.partial(_even_kernel, layer=l, seq_len=seq_len, row0=row0, row_step=row_step)
    return pl.pallas_call(
        kern,
        grid=(n_rows // TILE_ROWS,),
        in_specs=[
            pl.BlockSpec((TILE_ROWS, D), lambda i: (i, 0)),
            _mod_spec(l, 0),
            _const_spec(w["w_in"].shape),
            _const_spec(w["w_s"].shape),
            _const_spec(w["b_s_t"].shape),
            _const_spec(w["w_pool"].shape),
            _const_spec(w["pool_scale"].shape),
            _const_spec(w["w_out"].shape),
            _const_spec(w["ln_g"].shape),
            _const_spec(w["ln_b"].shape),
        ],
        out_specs=pl.BlockSpec((TILE_ROWS, D), lambda i: (i, 0)),
        out_shape=jax.ShapeDtypeStruct((n_rows, D), F32),
        scratch_shapes=[
            pltpu.VMEM((TILE_ROWS, A_WIDTH), F32),
            pltpu.VMEM((TILE_ROWS, A_WIDTH), BF16),
            pltpu.VMEM((TILE_ROWS, B_WIDTH), F32),
            pltpu.VMEM((TILE_ROWS, D), BF16),
        ],
        compiler_params=pltpu.CompilerParams(
            dimension_semantics=("arbitrary",), vmem_limit_bytes=VMEM_LIMIT),
        name=name,
    )(x, mod, w["w_in"], w["w_s"], w["b_s_t"], w["w_pool"], w["pool_scale"], w["w_out"],
      w["ln_g"], w["ln_b"])


def _conv3_block(z_ref, r0, lanes, cw, cb, seq_len):
    nv = FFN_RB // SUBLANES
    z3 = z_ref[r0:r0 + FFN_RB, lanes].reshape(nv, SUBLANES, LANES)
    sub = lax.broadcasted_iota(jnp.int32, (nv, SUBLANES, LANES), 1)
    down = pltpu.roll(z3, 1, 1)
    up = pltpu.roll(z3, SUBLANES - 1, 1)
    if r0 % seq_len == 0:
        prev = jnp.zeros((1, SUBLANES, LANES), F32)
    else:
        prev = pltpu.roll(z_ref[r0 - SUBLANES:r0, lanes], 1, 0)[None]
    if (r0 + FFN_RB) % seq_len == 0:
        nxt = jnp.zeros((1, SUBLANES, LANES), F32)
    else:
        nxt = pltpu.roll(z_ref[r0 + FFN_RB:r0 + FFN_RB + SUBLANES, lanes], SUBLANES - 1, 0)[None]
    zm = jnp.where(sub == 0, jnp.concatenate([prev, down[:-1]], axis=0), down)
    zp = jnp.where(sub == SUBLANES - 1, jnp.concatenate([up[1:], nxt], axis=0), up)
    y = cw[0][:, lanes] * zm + cw[1][:, lanes] * z3 + cw[2][:, lanes] * zp + cb[:, lanes]
    return y.reshape(FFN_RB, LANES)


def _ffn_kernel(x_ref, mod_ref, wup_ref, cw_ref, cb_ref, wd_ref, lng_ref, lnb_ref, o_ref,
                hb_ref, zg0_ref, zv0_ref, zg1_ref, zv1_ref, a_ref, *, layer, seq_len, row0, row_step):
    T = x_ref.shape[0]
    n_chunks = wd_ref.shape[0]
    shift, scale, gate = _mod_rows(mod_ref, row0, row_step)
    z_bufs = ((zg0_ref, zv0_ref), (zg1_ref, zv1_ref))

    def row_block(rb):
        return pl.ds(pl.multiple_of(rb * ROW_BLOCK, ROW_BLOCK), ROW_BLOCK)

    def modulate(rb, carry):
        rows = row_block(rb)
        hb_ref[rows, :] = (x_ref[rows, :] * (1.0 + scale) + shift).astype(BF16)
        return carry

    lax.fori_loop(0, T // ROW_BLOCK, modulate, 0)

    n_parts = T // FFN_DOT_ROWS
    blocks_per_part = FFN_DOT_ROWS // FFN_RB

    def up_proj(c, bufs, q):
        rows = slice(q * FFN_DOT_ROWS, (q + 1) * FFN_DOT_ROWS)
        bufs[0][rows, :] = _dot(hb_ref[rows, :], wup_ref[c])
        bufs[1][rows, :] = _dot(hb_ref[rows, :], wup_ref[c + n_chunks])

    def gate_blocks(c, bufs, q):
        cwg = [cw_ref[k, pl.ds(c, 1), :] for k in range(3)]
        cwv = [cw_ref[k, pl.ds(c + n_chunks, 1), :] for k in range(3)]
        cbg, cbv = cb_ref[pl.ds(c, 1), :], cb_ref[pl.ds(c + n_chunks, 1), :]
        for rb in range(q * blocks_per_part, (q + 1) * blocks_per_part):
            r0 = rb * FFN_RB
            for lt in range(FF_CHUNK // LANES):
                lanes = slice(lt * LANES, (lt + 1) * LANES)
                g = _conv3_block(bufs[0], r0, lanes, cwg, cbg, seq_len)
                v = _conv3_block(bufs[1], r0, lanes, cwv, cbv, seq_len)
                a_ref[c, r0:r0 + FFN_RB, lanes] = (g * jax.nn.sigmoid(g) * v).astype(BF16)

    for q in range(n_parts):
        up_proj(0, z_bufs[0], q)

    def chunk_pair(k, carry):
        c = 2 * k
        for q in range(n_parts):
            up_proj(c + 1, z_bufs[1], q)
            gate_blocks(c, z_bufs[0], q)
        for q in range(n_parts):
            up_proj(c + 2, z_bufs[0], q)
            gate_blocks(c + 1, z_bufs[1], q)
        return carry

    assert n_chunks % 2 == 1
    lax.fori_loop(0, n_chunks // 2, chunk_pair, 0)
    for q in range(n_parts):
        gate_blocks(n_chunks - 1, z_bufs[0], q)

    def out(rb, carry):
        rows = row_block(rb)
        acc = _dot(a_ref[0, rows, :], wd_ref[0])
        for c in range(1, n_chunks):
            acc = acc + _dot(a_ref[c, rows, :], wd_ref[c])
        r = ALPHA * x_ref[rows, :] + gate * acc
        o_ref[rows, :] = _layernorm(r, lng_ref[layer:layer + 1, :], lnb_ref[layer:layer + 1, :])
        return carry

    lax.fori_loop(0, T // ROW_BLOCK, out, 0)


def _ffn_layer(x, mod, w, l, seq_len, row0, row_step, name):
    n_rows = x.shape[0]
    n_chunks = D_FF // FF_CHUNK
    kern = functools.partial(_ffn_kernel, layer=l, seq_len=seq_len, row0=row0, row_step=row_step)
    params = [w["w_up"], w["conv_w"], w["conv_b"], w["w_down"]]
    return pl.pallas_call(
        kern,
        grid=(n_rows // TILE_ROWS,),
        in_specs=[pl.BlockSpec((TILE_ROWS, D), lambda i: (i, 0)), _mod_spec(l, 1)]
        + [_layer_spec(a.shape, l) for a in params]
        + [_const_spec(w["ln_g"].shape), _const_spec(w["ln_b"].shape)],
        out_specs=pl.BlockSpec((TILE_ROWS, D), lambda i: (i, 0)),
        out_shape=jax.ShapeDtypeStruct((n_rows, D), F32),
        scratch_shapes=[
            pltpu.VMEM((TILE_ROWS, D), BF16),
            pltpu.VMEM((TILE_ROWS, FF_CHUNK), F32),
            pltpu.VMEM((TILE_ROWS, FF_CHUNK), F32),
            pltpu.VMEM((TILE_ROWS, FF_CHUNK), F32),
            pltpu.VMEM((TILE_ROWS, FF_CHUNK), F32),
            pltpu.VMEM((n_chunks, TILE_ROWS, FF_CHUNK), BF16),
        ],
        compiler_params=pltpu.CompilerParams(
            dimension_semantics=("arbitrary",), vmem_limit_bytes=VMEM_LIMIT),
        name=name,
    )(x, mod, *params, w["ln_g"], w["ln_b"])


def _prep_up_kernel(w_ref, o_ref):
    for k in range(o_ref.shape[0]):
        o_ref[k] = w_ref[:, k * FF_CHUNK:(k + 1) * FF_CHUNK].astype(BF16)


def _prep_down_kernel(w_ref, o_ref):
    o_ref[...] = w_ref[...].astype(BF16)


def _prep_ffn_weights(w_up, w_down):
    n_ch = 2 * D_FF // FF_CHUNK
    up = pl.pallas_call(
        _prep_up_kernel,
        grid=(DEPTH, 2),
        in_specs=[pl.BlockSpec((None, D, D_FF), lambda l, j: (l, 0, j))],
        out_specs=pl.BlockSpec((None, n_ch // 2, D, FF_CHUNK), lambda l, j: (l, j, 0, 0)),
        out_shape=jax.ShapeDtypeStruct((DEPTH, n_ch, D, FF_CHUNK), BF16),
        compiler_params=pltpu.CompilerParams(
            dimension_semantics=("arbitrary", "arbitrary"), vmem_limit_bytes=VMEM_LIMIT),
        name="ffn_up_bf16",
    )(w_up)
    half = D_FF // 2
    down = pl.pallas_call(
        _prep_down_kernel,
        grid=(DEPTH, 2),
        in_specs=[pl.BlockSpec((None, half, D), lambda l, j: (l, j, 0))],
        out_specs=pl.BlockSpec((None, half, D), lambda l, j: (l, j, 0)),
        out_shape=jax.ShapeDtypeStruct((DEPTH, D_FF, D), BF16),
        compiler_params=pltpu.CompilerParams(
            dimension_semantics=("arbitrary", "arbitrary"), vmem_limit_bytes=VMEM_LIMIT),
        name="ffn_down_bf16",
    )(w_down)
    return up, down.reshape(DEPTH, n_ch // 2, FF_CHUNK, D)


def _conv4_block(xr_ref, cbk, r0, cw, cb, seq_len):
    nv = ODD_RB // SUBLANES
    z3 = xr_ref[cbk, r0:r0 + ODD_RB, :].reshape(nv, SUBLANES, LANES)
    sub = lax.broadcasted_iota(jnp.int32, (nv, SUBLANES, LANES), 1)
    down = pltpu.roll(z3, 1, 1)
    up1 = pltpu.roll(z3, SUBLANES - 1, 1)
    up2 = pltpu.roll(z3, SUBLANES - 2, 1)
    zero = jnp.zeros((1, SUBLANES, LANES), F32)
    if r0 % seq_len == 0:
        prev = zero
    else:
        prev = pltpu.roll(xr_ref[cbk, r0 - SUBLANES:r0, :], 1, 0)[None]
    if (r0 + ODD_RB) % seq_len == 0:
        nxt1 = nxt2 = zero
    else:
        nx = xr_ref[cbk, r0 + ODD_RB:r0 + ODD_RB + SUBLANES, :]
        nxt1 = pltpu.roll(nx, SUBLANES - 1, 0)[None]
        nxt2 = pltpu.roll(nx, SUBLANES - 2, 0)[None]
    zm = jnp.where(sub == 0, jnp.concatenate([prev, down[:-1]], axis=0), down)
    zp1 = jnp.where(sub == SUBLANES - 1, jnp.concatenate([up1[1:], nxt1], axis=0), up1)
    zp2 = jnp.where(sub >= SUBLANES - 2, jnp.concatenate([up2[1:], nxt2], axis=0), up2)
    y = cw[0:1, :] * zm + cw[1:2, :] * z3 + cw[2:3, :] * zp1 + cw[3:4, :] * zp2 + cb
    return y.reshape(ODD_RB, LANES)


def _scan_slabs(a, b, reverse):
    nv = a.shape[0] // SUBLANES
    a = a.reshape(nv, SUBLANES, LANES)
    b = b.reshape(nv, SUBLANES, LANES)
    sub = lax.broadcasted_iota(jnp.int32, (nv, SUBLANES, LANES), 1)
    for d in (1, 2, 4):
        if reverse:
            valid, sh = sub < SUBLANES - d, SUBLANES - d
        else:
            valid, sh = sub >= d, d
        a_prev = jnp.where(valid, pltpu.roll(a, sh, 1), 1.0)
        b_prev = jnp.where(valid, pltpu.roll(b, sh, 1), 0.0)
        b = b + a * b_prev
        a = a * a_prev
    return a.reshape(nv * SUBLANES, LANES), b.reshape(nv * SUBLANES, LANES)


def _scan_ends(a, b, posg, n_groups, reverse):
    n = a.shape[0]
    d = 1
    while d < n_groups:
        if reverse:
            valid, sh = posg < n_groups - d, n - d
        else:
            valid, sh = posg >= d, d
        a_prev = jnp.where(valid, pltpu.roll(a, sh, 0), 1.0)
        b_prev = jnp.where(valid, pltpu.roll(b, sh, 0), 0.0)
        b = b + a * b_prev
        a = a * a_prev
        d *= 2
    return a, b


def _rope_pairs(x, cos, sin_signed, even_lane):
    n = x.shape[1]
    swapped = jnp.where(even_lane, pltpu.roll(x, n - 1, 1), pltpu.roll(x, 1, 1))
    return x * cos + swapped * sin_signed


def _rmsnorm(x, g):
    return x * lax.rsqrt(jnp.mean(x * x, axis=-1, keepdims=True) + RMS_EPS) * g


def _interleave(primary, secondary):
    n_p, n_s = len(primary), len(secondary)
    done = 0
    for k, thunk in enumerate(primary):
        thunk()
        want = ((k + 1) * n_s) // n_p
        while done < want:
            secondary[done]()
            done += 1
    while done < n_s:
        secondary[done]()
        done += 1


def _odd_kernel(*refs, layer, seq_len, latent, past_len, row0, row_step):
    n_in = 21 if latent else 16
    n_out = 1 if latent else 4
    (x_ref, mod_ref, win_ref, cw_ref, cb_ref, wgate_ref, bgate_ref, lam_ref, qg_ref, kvg_ref,
     wuq_ref, wuk_ref, wuv_ref, wout_ref, lng_ref, lnb_ref) = refs[:16]
    if latent:
        cckv_ref, ckr_ref, st_ref, cos_ref, sin_ref = refs[16:21]
        (o_ref,) = refs[n_in:n_in + n_out]
    else:
        o_ref, ckv_out_ref, kr_out_ref, st_out_ref = refs[n_in:n_in + n_out]
    (xr_ref, gr_ref, xc_ref, pre_ref, a_ref, b_ref, hrep_ref, q_ref, k_ref, v_ref, yd_ref) = refs[n_in + n_out:]

    T = x_ref.shape[0]
    n_blocks = T // ROW_BLOCK
    n_tiles = W_C // LANES
    half_w = W_C // 2
    shift, scale, gate = _mod_rows(mod_ref, row0, row_step)
    o_q = 2 * W_C
    o_kv = o_q + Q_LORA
    o_kr = o_kv + KV_LORA
    pair_w = 2 * HEAD_PAD
    even_lane = (lax.broadcasted_iota(jnp.int32, (ROW_BLOCK, HEAD_PAD), 1) & 1) == 0
    softplus = []
    for direction in range(2):
        neg = -lam_ref[direction:direction + 1, :]
        softplus.append(jnp.maximum(neg, 0.0) + jnp.log1p(jnp.exp(-jnp.abs(neg))))

    def stage_a(rb):
        rows = slice(rb * ROW_BLOCK, (rb + 1) * ROW_BLOCK)
        k_rows = slice(past_len + rb * ROW_BLOCK, past_len + (rb + 1) * ROW_BLOCK)
        st = {}

        def t_xr():
            st["hb"] = (x_ref[rows, :] * (1.0 + scale) + shift).astype(BF16)
            z = _dot(st["hb"], win_ref[:, 0:W_C])
            for cbk in range(n_tiles):
                xr_ref[cbk, rows, :] = z[:, cbk * LANES:(cbk + 1) * LANES]

        def t_gr():
            gr_ref[rows, :] = _dot(st["hb"], win_ref[:, W_C:o_q])

        def t_lat():
            hb = st["hb"]
            st["qn"] = _rmsnorm(_dot(hb, win_ref[:, o_q:o_kv]), qg_ref[...]).astype(BF16)
            ckv = _rmsnorm(_dot(hb, win_ref[:, o_kv:o_kr]), kvg_ref[...])
            kr = _dot(hb, win_ref[:, o_kr:o_kr + HEAD_PAD])
            if latent:
                st["cos"] = cos_ref[rows, :]
                st["sin"] = sin_ref[rows, :]
                kr = _rope_pairs(kr, st["cos"], st["sin"], even_lane)
            else:
                ckv_out_ref[rows, :] = ckv
                kr_out_ref[rows, :] = kr[:, KR_LANE:KR_LANE + QK_ROPE]
            st["kr"] = kr
            st["ckvb"] = ckv.astype(BF16)

        def t_v():
            v_ref[k_rows, :] = _dot(st["ckvb"], wuv_ref[...]).astype(BF16)

        def t_pair(pair):
            def run():
                pcols = slice(pair * pair_w, (pair + 1) * pair_w)
                q2 = _dot(st["qn"], wuq_ref[:, pcols])
                k2 = _dot(st["ckvb"], wuk_ref[:, pcols])
                for hh in range(2):
                    cols = slice(pair * pair_w + hh * HEAD_PAD, pair * pair_w + (hh + 1) * HEAD_PAD)
                    qh = q2[:, hh * HEAD_PAD:(hh + 1) * HEAD_PAD]
                    if latent:
                        qh = _rope_pairs(qh, st["cos"], st["sin"], even_lane)
                    q_ref[rows, cols] = qh.astype(BF16)
                    k_ref[k_rows, cols] = (k2[:, hh * HEAD_PAD:(hh + 1) * HEAD_PAD] + st["kr"]).astype(BF16)
            return run

        return [t_xr, t_gr, t_lat, t_v] + [t_pair(p) for p in range(D_HEADS // 2)]

    def conv_thunks(rb):
        par = rb % 2

        def t_conv(cbk):
            def run():
                cw = cw_ref[:, cbk * LANES:(cbk + 1) * LANES]
                cb = cb_ref[:, cbk * LANES:(cbk + 1) * LANES]
                for rs in range(ROW_BLOCK // ODD_RB):
                    r0 = rb * ROW_BLOCK + rs * ODD_RB
                    xc_ref[par, rs * ODD_RB:(rs + 1) * ODD_RB, cbk * LANES:(cbk + 1) * LANES] = _conv4_block(
                        xr_ref, cbk, r0, cw, cb, seq_len)
            return run

        return [t_conv(cbk) for cbk in range(n_tiles)]

    def gate_thunks(rb):
        par = rb % 2
        thunks = []
        for direction in range(2):
            for j in range(2):
                slot = (2 * direction + j) % 2

                def t_dot(direction=direction, j=j, slot=slot):
                    xcb = xc_ref[par, :, j * half_w:(j + 1) * half_w].astype(BF16)
                    pre_ref[slot, 0] = _dot(xcb, wgate_ref[direction, 0, j])
                    pre_ref[slot, 1] = _dot(xcb, wgate_ref[direction, 1, j])

                def t_post(direction=direction, j=j, slot=slot):
                    for rs in range(ROW_BLOCK // ODD_RB):
                        lrows = slice(rs * ODD_RB, (rs + 1) * ODD_RB)
                        grows = slice(rb * ROW_BLOCK + rs * ODD_RB, rb * ROW_BLOCK + (rs + 1) * ODD_RB)
                        for lt in range(half_w // LANES):
                            cbk = j * (half_w // LANES) + lt
                            lanes = slice(lt * LANES, (lt + 1) * LANES)
                            glanes = slice(cbk * LANES, (cbk + 1) * LANES)
                            xcv = xc_ref[par, lrows, glanes]
                            r = jax.nn.sigmoid(pre_ref[slot, 0, lrows, lanes] + bgate_ref[direction, 0:1, glanes])
                            i = jax.nn.sigmoid(pre_ref[slot, 1, lrows, lanes] + bgate_ref[direction, 1:2, glanes])
                            log_a = -RG_C * r * softplus[direction][:, glanes]
                            a = jnp.exp(log_a)
                            b = jnp.sqrt(jnp.tanh(-log_a) * (a * a + 1.0)) * i * xcv
                            a, b = _scan_slabs(a, b, direction == 1)
                            a_ref[direction, cbk, grows, :] = a
                            b_ref[direction, cbk, grows, :] = b

                thunks += [t_dot, t_post]
        return thunks

    def combine(direction, cbk, t0, n_rows, h0):
        reverse = direction == 1
        n_ends = n_rows // SUBLANES
        end_row = 0 if reverse else SUBLANES - 1
        posg = lax.broadcasted_iota(jnp.int32, (n_ends, LANES), 0)
        a_end = a_ref[direction, cbk, pl.ds(t0 + end_row, n_ends, stride=SUBLANES), :]
        b_end = b_ref[direction, cbk, pl.ds(t0 + end_row, n_ends, stride=SUBLANES), :]
        a_tot, b_tot = _scan_ends(a_end, b_end, posg, n_ends, reverse)
        ends = b_tot if h0 is None else a_tot * h0 + b_tot
        first = jnp.zeros((1, LANES), F32) if h0 is None else h0
        if reverse:
            carry_in = jnp.where(posg < n_ends - 1, pltpu.roll(ends, n_ends - 1, 0), first)
        else:
            carry_in = jnp.where(posg >= 1, pltpu.roll(ends, 1, 0), first)
        for rr in range(SUBLANES):
            hrep_ref[pl.ds(t0 + rr, n_ends, stride=SUBLANES), :] = carry_in
        for rs in range(n_rows // ODD_RB):
            rows = slice(t0 + rs * ODD_RB, t0 + (rs + 1) * ODD_RB)
            h = a_ref[direction, cbk, rows, :] * hrep_ref[rows, :] + b_ref[direction, cbk, rows, :]
            if direction == 0:
                xr_ref[cbk, rows, :] = h
            else:
                xr_ref[cbk, rows, :] = xr_ref[cbk, rows, :] + h
        return ends

    def attend_pair(q_rows, k_rows, pair):
        o_pair = None
        for h in (2 * pair, 2 * pair + 1):
            cols = slice(h * HEAD_PAD, (h + 1) * HEAD_PAD)
            s = lax.dot_general(q_ref[q_rows, cols], k_ref[k_rows, cols],
                                (((1,), (1,)), ((), ())), preferred_element_type=F32) * ATTN_SCALE
            m = jnp.max(s, axis=-1, keepdims=True)
            p = jnp.exp(s - m)
            l = jnp.sum(p, axis=-1, keepdims=True)
            o = _dot(p.astype(BF16), v_ref[k_rows, cols]) / l
            o_pair = o if o_pair is None else o_pair + o
        yd_ref[q_rows, pair * HEAD_PAD:(pair + 1) * HEAD_PAD] = o_pair.astype(BF16)

    def out_block(rows):
        yc = jnp.concatenate(
            [xr_ref[cbk, rows, :] * _gelu(gr_ref[rows, cbk * LANES:(cbk + 1) * LANES]) for cbk in range(n_tiles)],
            axis=1).astype(BF16)
        d = _dot(yc, wout_ref[0:W_C, :]) + _dot(yd_ref[rows, :], wout_ref[W_C:, :])
        r = ALPHA * x_ref[rows, :] + gate * d
        o_ref[rows, :] = _layernorm(r, lng_ref[layer:layer + 1, :], lnb_ref[layer:layer + 1, :])

    if not latent:
        def stage_b(s):
            t0 = s * seq_len
            rows = slice(t0, t0 + seq_len)
            thunks = conv_thunks(s) + gate_thunks(s)
            for direction in range(2):
                for cbk in range(n_tiles):
                    def t_comb(direction=direction, cbk=cbk):
                        ends = combine(direction, cbk, t0, seq_len, None)
                        last = 0 if direction == 1 else seq_len // SUBLANES - 1
                        st_out_ref[s, direction:direction + 1, cbk * LANES:(cbk + 1) * LANES] = ends[last:last + 1, :]
                    thunks.append(t_comb)
            thunks += [functools.partial(attend_pair, rows, rows, pair) for pair in range(D_HEADS // 2)]
            thunks.append(functools.partial(out_block, rows))
            return thunks

        assert seq_len == ROW_BLOCK
        for t in stage_a(0):
            t()
        for s in range(n_blocks):
            _interleave(stage_b(s), stage_a(s + 1) if s + 1 < n_blocks else [])
        return

    assert seq_len == T
    for t in stage_a(0):
        t()
    for rb in range(n_blocks):
        nxt = stage_a(rb + 1) if rb + 1 < n_blocks else []
        if nxt:
            nxt[0]()
        _interleave(conv_thunks(rb) + gate_thunks(rb), nxt[1:])

    def cached(cbk, carry):
        rows = pl.ds(pl.multiple_of(cbk * ROW_BLOCK, ROW_BLOCK), ROW_BLOCK)
        cckv = cckv_ref[rows, :].astype(BF16)
        ckr = ckr_ref[rows, :]
        v_ref[rows, :] = _dot(cckv, wuv_ref[...]).astype(BF16)
        for pair in range(D_HEADS // 2):
            pcols = slice(pair * pair_w, (pair + 1) * pair_w)
            k2 = _dot(cckv, wuk_ref[:, pcols])
            for hh in range(2):
                cols = slice(pair * pair_w + hh * HEAD_PAD, pair * pair_w + (hh + 1) * HEAD_PAD)
                k_ref[rows, cols] = (k2[:, hh * HEAD_PAD:(hh + 1) * HEAD_PAD] + ckr).astype(BF16)
        return carry

    lax.fori_loop(0, past_len // ROW_BLOCK, cached, 0)

    n_keys = past_len + seq_len

    def tail(qb, carry):
        q_rows = pl.ds(pl.multiple_of(qb * ROW_BLOCK, ROW_BLOCK), ROW_BLOCK)
        k_rows = slice(0, n_keys)
        rg = []
        for direction in range(2):
            def t_comb(direction=direction):
                h0 = st_ref[direction, pl.ds(qb, 1), :]
                combine(direction, qb, 0, T, h0)
            rg.append(t_comb)
        _interleave([functools.partial(attend_pair, q_rows, k_rows, pair) for pair in range(D_HEADS // 2)], rg)
        return carry

    assert n_tiles == n_blocks
    lax.fori_loop(0, n_blocks, tail, 0)

    def out(rb, carry):
        out_block(pl.ds(pl.multiple_of(rb * ROW_BLOCK, ROW_BLOCK), ROW_BLOCK))
        return carry

    lax.fori_loop(0, n_blocks, out, 0)


def _odd_layer(x, mod, w, l, seq_len, row0, row_step, name, latent_inputs=None):
    n_rows = x.shape[0]
    n_tiles = n_rows // TILE_ROWS
    latent = latent_inputs is not None
    past_len = latent_inputs["cache_ckv"].shape[1] if latent else 0
    n_seq = TILE_ROWS // seq_len
    kern = functools.partial(_odd_kernel, layer=l, seq_len=seq_len, latent=latent, past_len=past_len,
                             row0=row0, row_step=row_step)
    weights = [w["w_in"], w["conv_w"], w["conv_b"], w["w_gate"], w["b_gate"], w["lam"], w["q_g"], w["kv_g"],
               w["w_uq"], w["w_uk"], w["w_uv"], w["w_out"]]
    in_specs = [pl.BlockSpec((TILE_ROWS, D), lambda i: (i, 0), pipeline_mode=pl.Buffered(1)),
                _mod_spec(l, 0)]
    in_specs += [_const_spec(a.shape) for a in weights]
    in_specs += [_const_spec(w["ln_g"].shape), _const_spec(w["ln_b"].shape)]
    args = [x, mod] + weights + [w["ln_g"], w["ln_b"]]
    out_specs = [pl.BlockSpec((TILE_ROWS, D), lambda i: (i, 0), pipeline_mode=pl.Buffered(1))]
    out_shape = [jax.ShapeDtypeStruct((n_rows, D), F32)]
    if latent:
        li = latent_inputs
        in_specs += [
            pl.BlockSpec((None, past_len, KV_LORA), lambda i: (i, 0, 0)),
            pl.BlockSpec((None, past_len, HEAD_PAD), lambda i: (i, 0, 0)),
            pl.BlockSpec((None, 2, W_C // LANES, LANES), lambda i: (i, 0, 0, 0)),
            _const_spec(li["cos"].shape),
            _const_spec(li["sin"].shape),
        ]
        args += [li["cache_ckv"], li["cache_kr"], li["state"], li["cos"], li["sin"]]
    else:
        out_specs += [
            pl.BlockSpec((TILE_ROWS, KV_LORA), lambda i: (i, 0)),
            pl.BlockSpec((TILE_ROWS, QK_ROPE), lambda i: (i, 0)),
            pl.BlockSpec((n_seq, 2, W_C), lambda i: (i, 0, 0)),
        ]
        out_shape += [
            jax.ShapeDtypeStruct((n_rows, KV_LORA), F32),
            jax.ShapeDtypeStruct((n_rows, QK_ROPE), F32),
            jax.ShapeDtypeStruct((n_rows // seq_len, 2, W_C), F32),
        ]
    n_keys_buf = past_len + TILE_ROWS
    scratch = [
        pltpu.VMEM((W_C // LANES, TILE_ROWS, LANES), F32),
        pltpu.VMEM((TILE_ROWS, W_C), F32),
        pltpu.VMEM((2, ROW_BLOCK, W_C), F32),
        pltpu.VMEM((2, 2, ROW_BLOCK, W_C // 2), F32),
        pltpu.VMEM((2, W_C // LANES, TILE_ROWS, LANES), F32),
        pltpu.VMEM((2, W_C // LANES, TILE_ROWS, LANES), F32),
        pltpu.VMEM((TILE_ROWS, LANES), F32),
        pltpu.VMEM((TILE_ROWS, D_HEADS * HEAD_PAD), BF16),
        pltpu.VMEM((n_keys_buf, D_HEADS * HEAD_PAD), BF16),
        pltpu.VMEM((n_keys_buf, D_HEADS * HEAD_PAD), BF16),
        pltpu.VMEM((TILE_ROWS, W_C), BF16),
    ]
    return pl.pallas_call(
        kern,
        grid=(n_tiles,),
        in_specs=in_specs,
        out_specs=out_specs,
        out_shape=out_shape,
        scratch_shapes=scratch,
        compiler_params=pltpu.CompilerParams(
            dimension_semantics=("arbitrary",), vmem_limit_bytes=VMEM_LIMIT),
        name=name,
    )(*args)


def _axial_rope_tables(rows):
    half = QK_ROPE // 2
    inv = (ROPE_BASE ** (-np.arange(0, half, 2, dtype=np.float32) / half)).astype(np.float32)
    r = np.repeat(np.arange(rows, dtype=np.float32), GRID_W)
    col = np.tile(np.arange(GRID_W, dtype=np.float32), rows)
    ang = np.concatenate([r[:, None] * inv, col[:, None] * inv], axis=-1).astype(np.float32)
    cos = np.repeat(np.cos(ang), 2, axis=-1)
    sin = np.repeat(np.sin(ang), 2, axis=-1) * np.tile(np.array([-1.0, 1.0], np.float32), half)
    n = ang.shape[0]
    cos_t = np.ones((n, HEAD_PAD), np.float32)
    sin_t = np.zeros((n, HEAD_PAD), np.float32)
    cos_t[:, KR_LANE:KR_LANE + QK_ROPE] = cos
    sin_t[:, KR_LANE:KR_LANE + QK_ROPE] = sin
    return jnp.asarray(cos_t), jnp.asarray(sin_t)


def _block_diag_gates(w):
    per = (W_C // 2) // C_BLOCK
    w4 = w.reshape(2, 2, per, C_BLOCK, C_BLOCK)
    eye = jnp.eye(per, dtype=w.dtype)
    bd = jnp.einsum("dJhij,hk->dJhikj", w4, eye)
    return bd.reshape(2, 2, per * C_BLOCK, per * C_BLOCK)


def kernel(x_prompt, x_sample, cache_mla_ckv, cache_mla_krope, state_rglru, c, c_ctx, w_mod, b_mod, ln1_g, ln1_b, ln2_g, ln2_b, even_w_in, even_w_s, even_b_s, even_w_pool, even_pool_scale, even_w_out, odd_w_in, rg_conv_w, rg_conv_b, rg_w_a, rg_b_a, rg_w_x, rg_b_x, rg_lam, mla_q_g, mla_kv_g, mla_w_uq, mla_w_uk, mla_w_uv, odd_w_out, ffn_w_up, ffn_conv_w, ffn_conv_b, ffn_w_down):
    batch, seq, _ = x_prompt.shape
    dec_batch, dec_seq, _ = x_sample.shape
    assert TILE_ROWS % seq == 0 and dec_seq == TILE_ROWS and dec_batch + 1 <= SUBLANES

    mod = _modulation(c_ctx, c, w_mod, b_mod)

    wup_bf, wdown_bf = _prep_ffn_weights(ffn_w_up, ffn_w_down)
    n_ch = 2 * D_FF // FF_CHUNK
    wf = {"ln_g": ln2_g, "ln_b": ln2_b,
          "w_up": wup_bf, "conv_w": ffn_conv_w.reshape(DEPTH, 3, n_ch, FF_CHUNK),
          "conv_b": ffn_conv_b.reshape(DEPTH, n_ch, FF_CHUNK), "w_down": wdown_bf}
    ln1 = {"ln_g": ln1_g, "ln_b": ln1_b}

    xp = x_prompt.reshape(batch * seq, D)
    xs = x_sample.reshape(dec_batch * dec_seq, D)
    new_ckv = new_kr = new_state = None
    for l in range(DEPTH):
        if l % 2 == 0:
            e = l // 2
            w = dict(ln1,
                     w_in=even_w_in[e].astype(BF16), w_s=even_w_s[e].astype(BF16),
                     b_s_t=even_b_s[e].T, w_pool=even_w_pool[e].astype(BF16),
                     pool_scale=even_pool_scale[e][None, :], w_out=even_w_out[e].astype(BF16))
            xp = _even_layer(xp, mod, w, l, seq, 0, 0, f"even{l}_ctx")
            xs = _even_layer(xs, mod, w, l, dec_seq, 1, 1, f"even{l}_lat")
        else:
            o = l // 2
            w_in = odd_w_in[o]
            n_main = 2 * W_C + Q_LORA + KV_LORA
            w_in_pad = jnp.concatenate(
                [w_in[:, :n_main], jnp.zeros((D, KR_LANE), F32), w_in[:, n_main:],
                 jnp.zeros((D, HEAD_PAD - KR_LANE - QK_ROPE), F32)], axis=1).astype(BF16)
            w_gate = jnp.stack([_block_diag_gates(rg_w_a[o]), _block_diag_gates(rg_w_x[o])], axis=1)
            b_gate = jnp.stack([rg_b_a[o], rg_b_x[o]], axis=1)
            w_uq = jnp.pad(mla_w_uq[o], ((0, 0), (0, 0), (0, HEAD_PAD - QK_NOPE - QK_ROPE)))
            w_uk = jnp.pad(mla_w_uk[o], ((0, 0), (0, 0), (0, HEAD_PAD - QK_NOPE)))
            uv = mla_w_uv[o].reshape(KV_LORA, D_HEADS // 2, 2, V_DIM)
            zeros_v = jnp.zeros((KV_LORA, D_HEADS // 2, V_DIM), F32)
            w_uv = jnp.stack([jnp.concatenate([uv[:, :, 0], zeros_v], axis=-1),
                              jnp.concatenate([zeros_v, uv[:, :, 1]], axis=-1)], axis=2)
            w = dict(ln1,
                     w_in=w_in_pad, conv_w=rg_conv_w[o], conv_b=rg_conv_b[o][None, :],
                     w_gate=w_gate.astype(BF16), b_gate=b_gate, lam=rg_lam[o],
                     q_g=mla_q_g[o][None, :], kv_g=mla_kv_g[o][None, :],
                     w_uq=w_uq.reshape(Q_LORA, D_HEADS * HEAD_PAD).astype(BF16),
                     w_uk=w_uk.reshape(KV_LORA, D_HEADS * HEAD_PAD).astype(BF16),
                     w_uv=w_uv.reshape(KV_LORA, D_HEADS * HEAD_PAD).astype(BF16),
                     w_out=odd_w_out[o].astype(BF16))
            xp, ckv, kr, st = _odd_layer(xp, mod, w, l, seq, 0, 0, f"odd{l}_ctx")
            new_ckv = ckv.reshape(batch, 1, seq, KV_LORA)
            new_kr = kr.reshape(batch, 1, seq, QK_ROPE)
            new_state = st.reshape(batch, 1, 2, W_C)
            cos_t, sin_t = _axial_rope_tables(dec_seq // GRID_W)
            latent_inputs = {
                "cache_ckv": cache_mla_ckv[:, o],
                "cache_kr": jnp.pad(cache_mla_krope[:, o],
                                    ((0, 0), (0, 0), (KR_LANE, HEAD_PAD - KR_LANE - QK_ROPE))),
                "state": state_rglru[:, o].reshape(dec_batch, 2, W_C // LANES, LANES),
                "cos": cos_t, "sin": sin_t,
            }
            (xs,) = _odd_layer(xs, mod, w, l, dec_seq, 1, 1, f"odd{l}_lat", latent_inputs)
        xp = _ffn_layer(xp, mod, wf, l, seq, 0, 0, f"ffn{l}_ctx")
        xs = _ffn_layer(xs, mod, wf, l, dec_seq, 1, 1, f"ffn{l}_lat")
    return (xp.reshape(batch, seq, D), xs.reshape(dec_batch, dec_seq, D), new_ckv, new_kr, new_state)
```

```python
import functools
import math

import jax
import jax.numpy as jnp
import numpy as np
from jax import lax
from jax.experimental import pallas as pl
from jax.experimental.pallas import tpu as pltpu

F32 = jnp.float32
BF16 = jnp.bfloat16

D = 1024
DEPTH = 2
GRID_W = 64
ALPHA = (2 * DEPTH) ** 0.25
LN_EPS = 1e-5
RMS_EPS = 1e-6
A_HEADS = 4
A_WIDTH = 512
A_GROUP_W = A_WIDTH // A_HEADS
CHUNK = 128
POOL_WINDOWS = (2, 4, 8, 16)
B_GROUP = 128
B_WIDTH = 512
W_C = 512
C_HEADS = 8
C_BLOCK = 64
RG_C = 8.0
D_HEADS = 8
Q_LORA = 384
KV_LORA = 256
QK_NOPE = 64
QK_ROPE = 32
V_DIM = 64
ROPE_BASE = 10000.0
ATTN_SCALE = 1.0 / math.sqrt(QK_NOPE + QK_ROPE)
D_FF = 2816

LANES = 128
SUBLANES = 8
HEAD_PAD = LANES
ODD_IN_PAD = 2 * W_C + Q_LORA + KV_LORA + HEAD_PAD
KR_LANE = QK_NOPE
TILE_ROWS = 1024
ROW_BLOCK = 256
FF_CHUNK = 256
FFN_DOT_ROWS = 128
FFN_RB = 64
ODD_RB = 64
VMEM_LIMIT = 58 * 1024 * 1024


def _dot(a, b):
    return jnp.dot(a, b, preferred_element_type=F32)


def _gelu(x):
    return x * (0.5 * (1.0 + jnp.tanh(0.7978845608028654 * (x + 0.044715 * (x * x * x)))))


def _layernorm(r, g, b):
    mu = jnp.mean(r, axis=-1, keepdims=True)
    c = r - mu
    var = jnp.mean(c * c, axis=-1, keepdims=True)
    return c * lax.rsqrt(var + LN_EPS) * g + b


def _row_pos(shape, seq_len):
    return lax.broadcasted_iota(jnp.int32, shape, 0) & (seq_len - 1)


def _shift_rows(x, d, pos, seq_len):
    n = x.shape[0]
    y = pltpu.roll(x, (-d) % n, 0)
    valid = (pos < seq_len - d) if d > 0 else (pos >= -d)
    return jnp.where(valid, y, 0.0)


def _mod_kernel(cctx_ref, c_ref, w_ref, b_ref, o_ref, s_ref):
    n_lat = c_ref.shape[0]
    s_ref[...] = jnp.zeros_like(s_ref)
    s_ref[0:1, :] = cctx_ref[...]
    s_ref[1:1 + n_lat, :] = c_ref[...]
    s = s_ref[...]
    s = s * jax.nn.sigmoid(s)
    bias = b_ref[pl.ds(pl.program_id(0), 1), :]
    o_ref[...] = _dot(s.astype(BF16), w_ref[...].astype(BF16)) + bias


def _modulation(c_ctx, c, w_mod, b_mod):
    nc = 1536
    return pl.pallas_call(
        _mod_kernel,
        grid=(DEPTH, 6 * D // nc),
        in_specs=[
            pl.BlockSpec((1, D), lambda l, j: (0, 0)),
            pl.BlockSpec(c.shape, lambda l, j: (0, 0)),
            pl.BlockSpec((None, D, nc), lambda l, j: (l, 0, j)),
            pl.BlockSpec((DEPTH, nc), lambda l, j: (0, j)),
        ],
        out_specs=pl.BlockSpec((None, SUBLANES, nc), lambda l, j: (l, 0, j)),
        out_shape=jax.ShapeDtypeStruct((DEPTH, SUBLANES, 6 * D), F32),
        scratch_shapes=[pltpu.VMEM((SUBLANES, D), F32)],
        compiler_params=pltpu.CompilerParams(
            dimension_semantics=("arbitrary", "arbitrary"), vmem_limit_bytes=VMEM_LIMIT),
        name="modulation",
    )(c_ctx.reshape(1, D), c, w_mod, b_mod)


def _mod_rows(mod_ref, row0, row_step):
    row = row0 + pl.program_id(0) * row_step
    m = mod_ref[pl.ds(row, 1), :]
    return m[:, 0:D], m[:, D:2 * D], m[:, 2 * D:3 * D]


def _const_spec(shape):
    nd = len(shape)
    return pl.BlockSpec(shape, lambda *_: (0,) * nd, pipeline_mode=pl.Buffered(1))


def _layer_spec(shape, l):
    nd = len(shape) - 1
    return pl.BlockSpec((None,) + tuple(shape[1:]), lambda *_: (l,) + (0,) * nd, pipeline_mode=pl.Buffered(1))


def _mod_spec(l, half):
    return pl.BlockSpec((None, SUBLANES, 3 * D), lambda *_: (l, 0, half))


def _even_kernel(x_ref, mod_ref, win_f32_ref, ws_ref, bst_ref, wpool_ref, pscale_ref, wout_f32_ref,
                 lng_ref, lnb_ref, o_ref, u_ref, vn_ref, p_ref, ycat_ref, win_ref, wout_ref,
                 *, layer, seq_len, row0, row_step):
    T = x_ref.shape[0]
    shift, scale, gate = _mod_rows(mod_ref, row0, row_step)

    @pl.when(pl.program_id(0) == 0)
    def _():
        def cast(rb, carry):
            rows = pl.ds(pl.multiple_of(rb * ROW_BLOCK, ROW_BLOCK), ROW_BLOCK)
            win_ref[rows, :] = win_f32_ref[rows, :].astype(BF16)
            wout_ref[rows, :] = wout_f32_ref[rows, :].astype(BF16)
            return carry

        lax.fori_loop(0, D // ROW_BLOCK, cast, 0)

    def proj(rb, carry):
        rows = pl.ds(pl.multiple_of(rb * ROW_BLOCK, ROW_BLOCK), ROW_BLOCK)
        hb = (x_ref[rows, :] * (1.0 + scale) + shift).astype(BF16)
        u_ref[rows, :] = _gelu(_dot(hb, win_ref[:, 0:A_WIDTH]))
        v = _gelu(_dot(hb, win_ref[:, A_WIDTH:2 * A_WIDTH]))
        mu = jnp.mean(v, axis=-1, keepdims=True)
        c = v - mu
        var = jnp.mean(c * c, axis=-1, keepdims=True)
        vn_ref[rows, :] = (c * lax.rsqrt(var + LN_EPS)).astype(BF16)
        p_ref[rows, :] = _dot(hb, win_ref[:, 2 * A_WIDTH:])
        return carry

    lax.fori_loop(0, T // ROW_BLOCK, proj, 0)

    def gate_chunk(n, carry):
        rows = pl.ds(pl.multiple_of(n * CHUNK, CHUNK), CHUNK)
        for h in range(A_HEADS):
            cols = slice(h * A_GROUP_W, (h + 1) * A_GROUP_W)
            s = _dot(ws_ref[h].astype(BF16), vn_ref[rows, cols]) + bst_ref[:, h:h + 1]
            ycat_ref[rows, cols] = (u_ref[rows, cols] * s).astype(BF16)
        return carry

    lax.fori_loop(0, T // CHUNK, gate_chunk, 0)

    pos = _row_pos((T, B_GROUP), seq_len)
    for gi, w in enumerate(POOL_WINDOWS):
        cols = slice(gi * B_GROUP, (gi + 1) * B_GROUP)
        half = w // 2
        pg = p_ref[:, cols]
        fwd = pg
        bwd = _shift_rows(pg, -1, pos, seq_len)
        k = 1
        while k < half:
            fwd = fwd + _shift_rows(fwd, k, pos, seq_len)
            bwd = bwd + _shift_rows(bwd, -k, pos, seq_len)
            k *= 2
        cnt = jnp.minimum(pos + half, seq_len) - jnp.maximum(pos - half, 0)
        pooled = (fwd + bwd) / cnt.astype(F32) - pg
        yb = _dot(pooled.astype(BF16), wpool_ref[gi].astype(BF16)) * pscale_ref[:, cols]
        ycat_ref[:, A_WIDTH + gi * B_GROUP:A_WIDTH + (gi + 1) * B_GROUP] = yb.astype(BF16)

    def out(rb, carry):
        rows = pl.ds(pl.multiple_of(rb * ROW_BLOCK, ROW_BLOCK), ROW_BLOCK)
        d = _dot(ycat_ref[rows, :], wout_ref[...])
        r = ALPHA * x_ref[rows, :] + gate * d
        o_ref[rows, :] = _layernorm(r, lng_ref[layer:layer + 1, :], lnb_ref[layer:layer + 1, :])
        return carry

    lax.fori_loop(0, T // ROW_BLOCK, out, 0)


def _even_layer(x, mod, w, l, seq_len, row0, row_step, name):
    n_rows = x.shape[0]
    e = l // 2
    kern = functools.partial(_even_kernel, layer=l, seq_len=seq_len, row0=row0, row_step=row_step)
    return pl.pallas_call(
        kern,
        grid=(n_rows // TILE_ROWS,),
        in_specs=[
            pl.BlockSpec((TILE_ROWS, D), lambda i: (i, 0)),
            _mod_spec(l, 0),
            _layer_spec(w["w_in"].shape, e),
            _layer_spec(w["w_s"].shape, e),
            _const_spec(w["b_s_t"].shape),
            _layer_spec(w["w_pool"].shape, e),
            _const_spec(w["pool_scale"].shape),
            _layer_spec(w["w_out"].shape, e),
            _const_spec(w["ln_g"].shape),
            _const_spec(w["ln_b"].shape),
        ],
        out_specs=pl.BlockSpec((TILE_ROWS, D), lambda i: (i, 0)),
        out_shape=jax.ShapeDtypeStruct((n_rows, D), F32),
        scratch_shapes=[
            pltpu.VMEM((TILE_ROWS, A_WIDTH), F32),
            pltpu.VMEM((TILE_ROWS, A_WIDTH), BF16),
            pltpu.VMEM((TILE_ROWS, B_WIDTH), F32),
            pltpu.VMEM((TILE_ROWS, D), BF16),
            pltpu.VMEM((D, 2 * A_WIDTH + B_WIDTH), BF16),
            pltpu.VMEM((D, D), BF16),
        ],
        compiler_params=pltpu.CompilerParams(
            dimension_semantics=("arbitrary",), vmem_limit_bytes=VMEM_LIMIT),
        name=name,
    )(x, mod, w["w_in"], w["w_s"], w["b_s_t"], w["w_pool"], w["pool_scale"], w["w_out"],
      w["ln_g"], w["ln_b"])


def _conv3_block(z_ref, r0, lanes, cw, cb, seq_len):
    nv = FFN_RB // SUBLANES
    z3 = z_ref[r0:r0 + FFN_RB, lanes].reshape(nv, SUBLANES, LANES)
    sub = lax.broadcasted_iota(jnp.int32, (nv, SUBLANES, LANES), 1)
    down = pltpu.roll(z3, 1, 1)
    up = pltpu.roll(z3, SUBLANES - 1, 1)
    if r0 % seq_len == 0:
        prev = jnp.zeros((1, SUBLANES, LANES), F32)
    else:
        prev = pltpu.roll(z_ref[r0 - SUBLANES:r0, lanes], 1, 0)[None]
    if (r0 + FFN_RB) % seq_len == 0:
        nxt = jnp.zeros((1, SUBLANES, LANES), F32)
    else:
        nxt = pltpu.roll(z_ref[r0 + FFN_RB:r0 + FFN_RB + SUBLANES, lanes], SUBLANES - 1, 0)[None]
    zm = jnp.where(sub == 0, jnp.concatenate([prev, down[:-1]], axis=0), down)
    zp = jnp.where(sub == SUBLANES - 1, jnp.concatenate([up[1:], nxt], axis=0), up)
    y = cw[0][:, lanes] * zm + cw[1][:, lanes] * z3 + cw[2][:, lanes] * zp + cb[:, lanes]
    return y.reshape(FFN_RB, LANES)


def _ffn_kernel(x_ref, mod_ref, wup_ref, cw_ref, cb_ref, wd_ref, lng_ref, lnb_ref, o_ref,
                hb_ref, zg0_ref, zv0_ref, zg1_ref, zv1_ref, a_ref, *, layer, seq_len, row0, row_step):
    T = x_ref.shape[0]
    n_chunks = wd_ref.shape[0]
    shift, scale, gate = _mod_rows(mod_ref, row0, row_step)
    z_bufs = ((zg0_ref, zv0_ref), (zg1_ref, zv1_ref))

    def row_block(rb):
        return pl.ds(pl.multiple_of(rb * ROW_BLOCK, ROW_BLOCK), ROW_BLOCK)

    def modulate(rb, carry):
        rows = row_block(rb)
        hb_ref[rows, :] = (x_ref[rows, :] * (1.0 + scale) + shift).astype(BF16)
        return carry

    lax.fori_loop(0, T // ROW_BLOCK, modulate, 0)

    n_parts = T // FFN_DOT_ROWS
    blocks_per_part = FFN_DOT_ROWS // FFN_RB

    def up_proj(c, bufs, q):
        rows = slice(q * FFN_DOT_ROWS, (q + 1) * FFN_DOT_ROWS)
        bufs[0][rows, :] = _dot(hb_ref[rows, :], wup_ref[c])
        bufs[1][rows, :] = _dot(hb_ref[rows, :], wup_ref[c + n_chunks])

    def gate_blocks(c, bufs, q):
        cwg = [cw_ref[k, pl.ds(c, 1), :] for k in range(3)]
        cwv = [cw_ref[k, pl.ds(c + n_chunks, 1), :] for k in range(3)]
        cbg, cbv = cb_ref[pl.ds(c, 1), :], cb_ref[pl.ds(c + n_chunks, 1), :]
        for rb in range(q * blocks_per_part, (q + 1) * blocks_per_part):
            r0 = rb * FFN_RB
            for lt in range(FF_CHUNK // LANES):
                lanes = slice(lt * LANES, (lt + 1) * LANES)
                g = _conv3_block(bufs[0], r0, lanes, cwg, cbg, seq_len)
                v = _conv3_block(bufs[1], r0, lanes, cwv, cbv, seq_len)
                a_ref[c, r0:r0 + FFN_RB, lanes] = (g * jax.nn.sigmoid(g) * v).astype(BF16)

    for q in range(n_parts):
        up_proj(0, z_bufs[0], q)

    def chunk_pair(k, carry):
        c = 2 * k
        for q in range(n_parts):
            up_proj(c + 1, z_bufs[1], q)
            gate_blocks(c, z_bufs[0], q)
        for q in range(n_parts):
            up_proj(c + 2, z_bufs[0], q)
            gate_blocks(c + 1, z_bufs[1], q)
        return carry

    assert n_chunks % 2 == 1
    lax.fori_loop(0, n_chunks // 2, chunk_pair, 0)
    for q in range(n_parts):
        gate_blocks(n_chunks - 1, z_bufs[0], q)

    def out(rb, carry):
        rows = row_block(rb)
        acc = _dot(a_ref[0, rows, :], wd_ref[0])
        for c in range(1, n_chunks):
            acc = acc + _dot(a_ref[c, rows, :], wd_ref[c])
        r = ALPHA * x_ref[rows, :] + gate * acc
        o_ref[rows, :] = _layernorm(r, lng_ref[layer:layer + 1, :], lnb_ref[layer:layer + 1, :])
        return carry

    lax.fori_loop(0, T // ROW_BLOCK, out, 0)


def _ffn_layer(x, mod, w, l, seq_len, row0, row_step, name):
    n_rows = x.shape[0]
    n_chunks = D_FF // FF_CHUNK
    kern = functools.partial(_ffn_kernel, layer=l, seq_len=seq_len, row0=row0, row_step=row_step)
    params = [w["w_up"], w["conv_w"], w["conv_b"], w["w_down"]]
    return pl.pallas_call(
        kern,
        grid=(n_rows // TILE_ROWS,),
        in_specs=[pl.BlockSpec((TILE_ROWS, D), lambda i: (i, 0)), _mod_spec(l, 1)]
        + [_layer_spec(a.shape, l) for a in params]
        + [_const_spec(w["ln_g"].shape), _const_spec(w["ln_b"].shape)],
        out_specs=pl.BlockSpec((TILE_ROWS, D), lambda i: (i, 0)),
        out_shape=jax.ShapeDtypeStruct((n_rows, D), F32),
        scratch_shapes=[
            pltpu.VMEM((TILE_ROWS, D), BF16),
            pltpu.VMEM((TILE_ROWS, FF_CHUNK), F32),
            pltpu.VMEM((TILE_ROWS, FF_CHUNK), F32),
            pltpu.VMEM((TILE_ROWS, FF_CHUNK), F32),
            pltpu.VMEM((TILE_ROWS, FF_CHUNK), F32),
            pltpu.VMEM((n_chunks, TILE_ROWS, FF_CHUNK), BF16),
        ],
        compiler_params=pltpu.CompilerParams(
            dimension_semantics=("arbitrary",), vmem_limit_bytes=VMEM_LIMIT),
        name=name,
    )(x, mod, *params, w["ln_g"], w["ln_b"])


def _prep_up_kernel(w_ref, o_ref):
    for k in range(o_ref.shape[0]):
        o_ref[k] = w_ref[:, k * FF_CHUNK:(k + 1) * FF_CHUNK].astype(BF16)


def _prep_down_kernel(w_ref, o_ref):
    o_ref[...] = w_ref[...].astype(BF16)


def _prep_ffn_weights(w_up, w_down):
    n_ch = 2 * D_FF // FF_CHUNK
    up = pl.pallas_call(
        _prep_up_kernel,
        grid=(DEPTH, 2),
        in_specs=[pl.BlockSpec((None, D, D_FF), lambda l, j: (l, 0, j))],
        out_specs=pl.BlockSpec((None, n_ch // 2, D, FF_CHUNK), lambda l, j: (l, j, 0, 0)),
        out_shape=jax.ShapeDtypeStruct((DEPTH, n_ch, D, FF_CHUNK), BF16),
        compiler_params=pltpu.CompilerParams(
            dimension_semantics=("arbitrary", "arbitrary"), vmem_limit_bytes=VMEM_LIMIT),
        name="ffn_up_bf16",
    )(w_up)
    half = D_FF // 2
    down = pl.pallas_call(
        _prep_down_kernel,
        grid=(DEPTH, 2),
        in_specs=[pl.BlockSpec((None, half, D), lambda l, j: (l, j, 0))],
        out_specs=pl.BlockSpec((None, half, D), lambda l, j: (l, j, 0)),
        out_shape=jax.ShapeDtypeStruct((DEPTH, D_FF, D), BF16),
        compiler_params=pltpu.CompilerParams(
            dimension_semantics=("arbitrary", "arbitrary"), vmem_limit_bytes=VMEM_LIMIT),
        name="ffn_down_bf16",
    )(w_down)
    return up, down.reshape(DEPTH, n_ch // 2, FF_CHUNK, D)


def _conv4_block(xr_ref, cbk, r0, cw, cb, seq_len):
    nv = ODD_RB // SUBLANES
    z3 = xr_ref[cbk, r0:r0 + ODD_RB, :].reshape(nv, SUBLANES, LANES)
    sub = lax.broadcasted_iota(jnp.int32, (nv, SUBLANES, LANES), 1)
    down = pltpu.roll(z3, 1, 1)
    up1 = pltpu.roll(z3, SUBLANES - 1, 1)
    up2 = pltpu.roll(z3, SUBLANES - 2, 1)
    zero = jnp.zeros((1, SUBLANES, LANES), F32)
    if r0 % seq_len == 0:
        prev = zero
    else:
        prev = pltpu.roll(xr_ref[cbk, r0 - SUBLANES:r0, :], 1, 0)[None]
    if (r0 + ODD_RB) % seq_len == 0:
        nxt1 = nxt2 = zero
    else:
        nx = xr_ref[cbk, r0 + ODD_RB:r0 + ODD_RB + SUBLANES, :]
        nxt1 = pltpu.roll(nx, SUBLANES - 1, 0)[None]
        nxt2 = pltpu.roll(nx, SUBLANES - 2, 0)[None]
    zm = jnp.where(sub == 0, jnp.concatenate([prev, down[:-1]], axis=0), down)
    zp1 = jnp.where(sub == SUBLANES - 1, jnp.concatenate([up1[1:], nxt1], axis=0), up1)
    zp2 = jnp.where(sub >= SUBLANES - 2, jnp.concatenate([up2[1:], nxt2], axis=0), up2)
    y = cw[0:1, :] * zm + cw[1:2, :] * z3 + cw[2:3, :] * zp1 + cw[3:4, :] * zp2 + cb
    return y.reshape(ODD_RB, LANES)


def _scan_slabs(a, b, reverse):
    nv = a.shape[0] // SUBLANES
    a = a.reshape(nv, SUBLANES, LANES)
    b = b.reshape(nv, SUBLANES, LANES)
    sub = lax.broadcasted_iota(jnp.int32, (nv, SUBLANES, LANES), 1)
    for d in (1, 2, 4):
        if reverse:
            valid, sh = sub < SUBLANES - d, SUBLANES - d
        else:
            valid, sh = sub >= d, d
        a_prev = jnp.where(valid, pltpu.roll(a, sh, 1), 1.0)
        b_prev = jnp.where(valid, pltpu.roll(b, sh, 1), 0.0)
        b = b + a * b_prev
        a = a * a_prev
    return a.reshape(nv * SUBLANES, LANES), b.reshape(nv * SUBLANES, LANES)


def _scan_ends(a, b, posg, n_groups, reverse):
    n = a.shape[0]
    d = 1
    while d < n_groups:
        if reverse:
            valid, sh = posg < n_groups - d, n - d
        else:
            valid, sh = posg >= d, d
        a_prev = jnp.where(valid, pltpu.roll(a, sh, 0), 1.0)
        b_prev = jnp.where(valid, pltpu.roll(b, sh, 0), 0.0)
        b = b + a * b_prev
        a = a * a_prev
        d *= 2
    return a, b


def _rope_pairs(x, cos, sin_signed, even_lane):
    n = x.shape[1]
    swapped = jnp.where(even_lane, pltpu.roll(x, n - 1, 1), pltpu.roll(x, 1, 1))
    return x * cos + swapped * sin_signed


def _rmsnorm(x, g):
    return x * lax.rsqrt(jnp.mean(x * x, axis=-1, keepdims=True) + RMS_EPS) * g


def _interleave(primary, secondary):
    n_p, n_s = len(primary), len(secondary)
    done = 0
    for k, thunk in enumerate(primary):
        thunk()
        want = ((k + 1) * n_s) // n_p
        while done < want:
            secondary[done]()
            done += 1
    while done < n_s:
        secondary[done]()
        done += 1


def _odd_kernel(*refs, layer, seq_len, latent, past_len, row0, row_step):
    n_in = 21 if latent else 16
    n_out = 1 if latent else 4
    (x_ref, mod_ref, win_ref, cw_ref, cb_ref, wgate_ref, bgate_ref, lam_ref, qg_ref, kvg_ref,
     wuq_ref, wuk_ref, wuv_ref, wout_ref, lng_ref, lnb_ref) = refs[:16]
    if latent:
        cckv_ref, ckr_ref, st_ref, cos_ref, sin_ref = refs[16:21]
        (o_ref,) = refs[n_in:n_in + n_out]
    else:
        o_ref, ckv_out_ref, kr_out_ref, st_out_ref = refs[n_in:n_in + n_out]
    (xr_ref, gr_ref, xc_ref, pre_ref, a_ref, b_ref, hrep_ref, q_ref, k_ref, v_ref, yd_ref) = refs[n_in + n_out:]

    T = x_ref.shape[0]
    n_blocks = T // ROW_BLOCK
    n_tiles = W_C // LANES
    half_w = W_C // 2
    shift, scale, gate = _mod_rows(mod_ref, row0, row_step)
    o_q = 2 * W_C
    o_kv = o_q + Q_LORA
    o_kr = o_kv + KV_LORA
    pair_w = 2 * HEAD_PAD
    even_lane = (lax.broadcasted_iota(jnp.int32, (ROW_BLOCK, HEAD_PAD), 1) & 1) == 0
    softplus = []
    for direction in range(2):
        neg = -lam_ref[direction:direction + 1, :]
        softplus.append(jnp.maximum(neg, 0.0) + jnp.log1p(jnp.exp(-jnp.abs(neg))))

    def stage_a(rb):
        rows = slice(rb * ROW_BLOCK, (rb + 1) * ROW_BLOCK)
        k_rows = slice(past_len + rb * ROW_BLOCK, past_len + (rb + 1) * ROW_BLOCK)
        st = {}

        def t_xr():
            st["hb"] = (x_ref[rows, :] * (1.0 + scale) + shift).astype(BF16)
            z = _dot(st["hb"], win_ref[:, 0:W_C])
            for cbk in range(n_tiles):
                xr_ref[cbk, rows, :] = z[:, cbk * LANES:(cbk + 1) * LANES]

        def t_gr():
            gr_ref[rows, :] = _dot(st["hb"], win_ref[:, W_C:o_q])

        def t_lat():
            hb = st["hb"]
            st["qn"] = _rmsnorm(_dot(hb, win_ref[:, o_q:o_kv]), qg_ref[...]).astype(BF16)
            ckv = _rmsnorm(_dot(hb, win_ref[:, o_kv:o_kr]), kvg_ref[...])
            kr = _dot(hb, win_ref[:, o_kr:o_kr + HEAD_PAD])
            if latent:
                st["cos"] = cos_ref[rows, :]
                st["sin"] = sin_ref[rows, :]
                kr = _rope_pairs(kr, st["cos"], st["sin"], even_lane)
            else:
                ckv_out_ref[rows, :] = ckv
                kr_out_ref[rows, :] = kr[:, KR_LANE:KR_LANE + QK_ROPE]
            st["kr"] = kr
            st["ckvb"] = ckv.astype(BF16)

        def t_v():
            v_ref[k_rows, :] = _dot(st["ckvb"], wuv_ref[...]).astype(BF16)

        def t_pair(pair):
            def run():
                pcols = slice(pair * pair_w, (pair + 1) * pair_w)
                q2 = _dot(st["qn"], wuq_ref[:, pcols])
                k2 = _dot(st["ckvb"], wuk_ref[:, pcols])
                for hh in range(2):
                    cols = slice(pair * pair_w + hh * HEAD_PAD, pair * pair_w + (hh + 1) * HEAD_PAD)
                    qh = q2[:, hh * HEAD_PAD:(hh + 1) * HEAD_PAD]
                    if latent:
                        qh = _rope_pairs(qh, st["cos"], st["sin"], even_lane)
                    q_ref[rows, cols] = qh.astype(BF16)
                    k_ref[k_rows, cols] = (k2[:, hh * HEAD_PAD:(hh + 1) * HEAD_PAD] + st["kr"]).astype(BF16)
            return run

        return [t_xr, t_gr, t_lat, t_v] + [t_pair(p) for p in range(D_HEADS // 2)]

    def conv_thunks(rb):
        par = rb % 2

        def t_conv(cbk):
            def run():
                cw = cw_ref[:, cbk * LANES:(cbk + 1) * LANES]
                cb = cb_ref[:, cbk * LANES:(cbk + 1) * LANES]
                for rs in range(ROW_BLOCK // ODD_RB):
                    r0 = rb * ROW_BLOCK + rs * ODD_RB
                    xc_ref[par, rs * ODD_RB:(rs + 1) * ODD_RB, cbk * LANES:(cbk + 1) * LANES] = _conv4_block(
                        xr_ref, cbk, r0, cw, cb, seq_len)
            return run

        return [t_conv(cbk) for cbk in range(n_tiles)]

    def gate_thunks(rb):
        par = rb % 2
        thunks = []
        for direction in range(2):
            for j in range(2):
                slot = (2 * direction + j) % 2

                def t_dot(direction=direction, j=j, slot=slot):
                    xcb = xc_ref[par, :, j * half_w:(j + 1) * half_w].astype(BF16)
                    pre_ref[slot, 0] = _dot(xcb, wgate_ref[direction, 0, j])
                    pre_ref[slot, 1] = _dot(xcb, wgate_ref[direction, 1, j])

                def t_post(direction=direction, j=j, slot=slot):
                    for rs in range(ROW_BLOCK // ODD_RB):
                        lrows = slice(rs * ODD_RB, (rs + 1) * ODD_RB)
                        grows = slice(rb * ROW_BLOCK + rs * ODD_RB, rb * ROW_BLOCK + (rs + 1) * ODD_RB)
                        for lt in range(half_w // LANES):
                            cbk = j * (half_w // LANES) + lt
                            lanes = slice(lt * LANES, (lt + 1) * LANES)
                            glanes = slice(cbk * LANES, (cbk + 1) * LANES)
                            xcv = xc_ref[par, lrows, glanes]
                            r = jax.nn.sigmoid(pre_ref[slot, 0, lrows, lanes] + bgate_ref[direction, 0:1, glanes])
                            i = jax.nn.sigmoid(pre_ref[slot, 1, lrows, lanes] + bgate_ref[direction, 1:2, glanes])
                            log_a = -RG_C * r * softplus[direction][:, glanes]
                            a = jnp.exp(log_a)
                            b = jnp.sqrt(jnp.tanh(-log_a) * (a * a + 1.0)) * i * xcv
                            a, b = _scan_slabs(a, b, direction == 1)
                            a_ref[direction, cbk, grows, :] = a
                            b_ref[direction, cbk, grows, :] = b

                thunks += [t_dot, t_post]
        return thunks

    def combine(direction, cbk, t0, n_rows, h0):
        reverse = direction == 1
        n_ends = n_rows // SUBLANES
        end_row = 0 if reverse else SUBLANES - 1
        posg = lax.broadcasted_iota(jnp.int32, (n_ends, LANES), 0)
        a_end = a_ref[direction, cbk, pl.ds(t0 + end_row, n_ends, stride=SUBLANES), :]
        b_end = b_ref[direction, cbk, pl.ds(t0 + end_row, n_ends, stride=SUBLANES), :]
        a_tot, b_tot = _scan_ends(a_end, b_end, posg, n_ends, reverse)
        ends = b_tot if h0 is None else a_tot * h0 + b_tot
        first = jnp.zeros((1, LANES), F32) if h0 is None else h0
        if reverse:
            carry_in = jnp.where(posg < n_ends - 1, pltpu.roll(ends, n_ends - 1, 0), first)
        else:
            carry_in = jnp.where(posg >= 1, pltpu.roll(ends, 1, 0), first)
        for rr in range(SUBLANES):
            hrep_ref[pl.ds(t0 + rr, n_ends, stride=SUBLANES), :] = carry_in
        for rs in range(n_rows // ODD_RB):
            rows = slice(t0 + rs * ODD_RB, t0 + (rs + 1) * ODD_RB)
            h = a_ref[direction, cbk, rows, :] * hrep_ref[rows, :] + b_ref[direction, cbk, rows, :]
            if direction == 0:
                xr_ref[cbk, rows, :] = h
            else:
                xr_ref[cbk, rows, :] = xr_ref[cbk, rows, :] + h
        return ends

    def attend_pair(q_rows, k_rows, pair):
        o_pair = None
        for h in (2 * pair, 2 * pair + 1):
            cols = slice(h * HEAD_PAD, (h + 1) * HEAD_PAD)
            s = lax.dot_general(q_ref[q_rows, cols], k_ref[k_rows, cols],
                                (((1,), (1,)), ((), ())), preferred_element_type=F32) * ATTN_SCALE
            m = jnp.max(s, axis=-1, keepdims=True)
            p = jnp.exp(s - m)
            l = jnp.sum(p, axis=-1, keepdims=True)
            o = _dot(p.astype(BF16), v_ref[k_rows, cols]) / l
            o_pair = o if o_pair is None else o_pair + o
        yd_ref[q_rows, pair * HEAD_PAD:(pair + 1) * HEAD_PAD] = o_pair.astype(BF16)

    def out_block(rows):
        yc = jnp.concatenate(
            [xr_ref[cbk, rows, :] * _gelu(gr_ref[rows, cbk * LANES:(cbk + 1) * LANES]) for cbk in range(n_tiles)],
            axis=1).astype(BF16)
        d = _dot(yc, wout_ref[0:W_C, :]) + _dot(yd_ref[rows, :], wout_ref[W_C:, :])
        r = ALPHA * x_ref[rows, :] + gate * d
        o_ref[rows, :] = _layernorm(r, lng_ref[layer:layer + 1, :], lnb_ref[layer:layer + 1, :])

    if not latent:
        def stage_b(s):
            t0 = s * seq_len
            rows = slice(t0, t0 + seq_len)
            thunks = conv_thunks(s) + gate_thunks(s)
            for direction in range(2):
                for cbk in range(n_tiles):
                    def t_comb(direction=direction, cbk=cbk):
                        ends = combine(direction, cbk, t0, seq_len, None)
                        last = 0 if direction == 1 else seq_len // SUBLANES - 1
                        st_out_ref[s, direction:direction + 1, cbk * LANES:(cbk + 1) * LANES] = ends[last:last + 1, :]
                    thunks.append(t_comb)
            return thunks

        assert seq_len == ROW_BLOCK
        for t in stage_a(0):
            t()
        for s in range(n_blocks):
            rows = slice(s * seq_len, (s + 1) * seq_len)
            attn = [functools.partial(attend_pair, rows, rows, pair) for pair in range(D_HEADS // 2)]
            nxt = stage_a(s + 1) if s + 1 < n_blocks else []
            mxu = []
            _interleave([functools.partial(mxu.append, t) for t in nxt] or [lambda: None],
                        [functools.partial(mxu.append, t) for t in attn])
            _interleave(stage_b(s), mxu)
            out_block(rows)
        return

    assert seq_len == T
    for t in stage_a(0):
        t()
    for rb in range(n_blocks):
        nxt = stage_a(rb + 1) if rb + 1 < n_blocks else []
        if nxt:
            nxt[0]()
        _interleave(conv_thunks(rb) + gate_thunks(rb), nxt[1:])

    def cached(cbk, carry):
        rows = pl.ds(pl.multiple_of(cbk * ROW_BLOCK, ROW_BLOCK), ROW_BLOCK)
        cckv = cckv_ref[rows, :].astype(BF16)
        ckr = ckr_ref[rows, :]
        v_ref[rows, :] = _dot(cckv, wuv_ref[...]).astype(BF16)
        for pair in range(D_HEADS // 2):
            pcols = slice(pair * pair_w, (pair + 1) * pair_w)
            k2 = _dot(cckv, wuk_ref[:, pcols])
            for hh in range(2):
                cols = slice(pair * pair_w + hh * HEAD_PAD, pair * pair_w + (hh + 1) * HEAD_PAD)
                k_ref[rows, cols] = (k2[:, hh * HEAD_PAD:(hh + 1) * HEAD_PAD] + ckr).astype(BF16)
        return carry

    lax.fori_loop(0, past_len // ROW_BLOCK, cached, 0)

    n_keys = past_len + seq_len

    def tail(qb, carry):
        q_rows = pl.ds(pl.multiple_of(qb * ROW_BLOCK, ROW_BLOCK), ROW_BLOCK)
        k_rows = slice(0, n_keys)
        rg = []
        for direction in range(2):
            def t_comb(direction=direction):
                h0 = st_ref[direction, pl.ds(qb, 1), :]
                combine(direction, qb, 0, T, h0)
            rg.append(t_comb)
        _interleave([functools.partial(attend_pair, q_rows, k_rows, pair) for pair in range(D_HEADS // 2)], rg)
        return carry

    assert n_tiles == n_blocks
    lax.fori_loop(0, n_blocks, tail, 0)

    def out(rb, carry):
        out_block(pl.ds(pl.multiple_of(rb * ROW_BLOCK, ROW_BLOCK), ROW_BLOCK))
        return carry

    lax.fori_loop(0, n_blocks, out, 0)


def _odd_layer(x, mod, w, l, seq_len, row0, row_step, name, latent_inputs=None):
    n_rows = x.shape[0]
    n_tiles = n_rows // TILE_ROWS
    latent = latent_inputs is not None
    past_len = latent_inputs["cache_ckv"].shape[1] if latent else 0
    n_seq = TILE_ROWS // seq_len
    kern = functools.partial(_odd_kernel, layer=l, seq_len=seq_len, latent=latent, past_len=past_len,
                             row0=row0, row_step=row_step)
    weights = [w["w_in"], w["conv_w"], w["conv_b"], w["w_gate"], w["b_gate"], w["lam"], w["q_g"], w["kv_g"],
               w["w_uq"], w["w_uk"], w["w_uv"], w["w_out"]]
    in_specs = [pl.BlockSpec((TILE_ROWS, D), lambda i: (i, 0), pipeline_mode=pl.Buffered(1)),
                _mod_spec(l, 0)]
    in_specs += [_const_spec(a.shape) for a in weights]
    in_specs += [_const_spec(w["ln_g"].shape), _const_spec(w["ln_b"].shape)]
    args = [x, mod] + weights + [w["ln_g"], w["ln_b"]]
    out_specs = [pl.BlockSpec((TILE_ROWS, D), lambda i: (i, 0), pipeline_mode=pl.Buffered(1))]
    out_shape = [jax.ShapeDtypeStruct((n_rows, D), F32)]
    if latent:
        li = latent_inputs
        in_specs += [
            pl.BlockSpec((None, past_len, KV_LORA), lambda i: (i, 0, 0)),
            pl.BlockSpec((None, past_len, HEAD_PAD), lambda i: (i, 0, 0)),
            pl.BlockSpec((None, 2, W_C // LANES, LANES), lambda i: (i, 0, 0, 0)),
            _const_spec(li["cos"].shape),
            _const_spec(li["sin"].shape),
        ]
        args += [li["cache_ckv"], li["cache_kr"], li["state"], li["cos"], li["sin"]]
    else:
        out_specs += [
            pl.BlockSpec((TILE_ROWS, KV_LORA), lambda i: (i, 0)),
            pl.BlockSpec((TILE_ROWS, QK_ROPE), lambda i: (i, 0)),
            pl.BlockSpec((n_seq, 2, W_C), lambda i: (i, 0, 0)),
        ]
        out_shape += [
            jax.ShapeDtypeStruct((n_rows, KV_LORA), F32),
            jax.ShapeDtypeStruct((n_rows, QK_ROPE), F32),
            jax.ShapeDtypeStruct((n_rows // seq_len, 2, W_C), F32),
        ]
    n_keys_buf = past_len + TILE_ROWS
    scratch = [
        pltpu.VMEM((W_C // LANES, TILE_ROWS, LANES), F32),
        pltpu.VMEM((TILE_ROWS, W_C), F32),
        pltpu.VMEM((2, ROW_BLOCK, W_C), F32),
        pltpu.VMEM((2, 2, ROW_BLOCK, W_C // 2), F32),
        pltpu.VMEM((2, W_C // LANES, TILE_ROWS, LANES), F32),
        pltpu.VMEM((2, W_C // LANES, TILE_ROWS, LANES), F32),
        pltpu.VMEM((TILE_ROWS, LANES), F32),
        pltpu.VMEM((TILE_ROWS, D_HEADS * HEAD_PAD), BF16),
        pltpu.VMEM((n_keys_buf, D_HEADS * HEAD_PAD), BF16),
        pltpu.VMEM((n_keys_buf, D_HEADS * HEAD_PAD), BF16),
        pltpu.VMEM((TILE_ROWS, W_C), BF16),
    ]
    return pl.pallas_call(
        kern,
        grid=(n_tiles,),
        in_specs=in_specs,
        out_specs=out_specs,
        out_shape=out_shape,
        scratch_shapes=scratch,
        compiler_params=pltpu.CompilerParams(
            dimension_semantics=("arbitrary",), vmem_limit_bytes=VMEM_LIMIT),
        name=name,
    )(*args)


def _axial_rope_tables(rows):
    half = QK_ROPE // 2
    inv = (ROPE_BASE ** (-np.arange(0, half, 2, dtype=np.float32) / half)).astype(np.float32)
    r = np.repeat(np.arange(rows, dtype=np.float32), GRID_W)
    col = np.tile(np.arange(GRID_W, dtype=np.float32), rows)
    ang = np.concatenate([r[:, None] * inv, col[:, None] * inv], axis=-1).astype(np.float32)
    cos = np.repeat(np.cos(ang), 2, axis=-1)
    sin = np.repeat(np.sin(ang), 2, axis=-1) * np.tile(np.array([-1.0, 1.0], np.float32), half)
    n = ang.shape[0]
    cos_t = np.ones((n, HEAD_PAD), np.float32)
    sin_t = np.zeros((n, HEAD_PAD), np.float32)
    cos_t[:, KR_LANE:KR_LANE + QK_ROPE] = cos
    sin_t[:, KR_LANE:KR_LANE + QK_ROPE] = sin
    return jnp.asarray(cos_t), jnp.asarray(sin_t)


def _odd_prep_kernel(win_ref, wa_ref, wx_ref, wuq_ref, wuk_ref, wuv_ref, wout_ref,
                     win_o, gate_o, wuq_o, wuk_o, wuv_o, wout_o):
    n_main = 2 * W_C + Q_LORA + KV_LORA
    win_o[:, 0:n_main] = win_ref[:, 0:n_main].astype(BF16)
    win_o[:, n_main:] = jnp.zeros((D, HEAD_PAD), BF16)
    win_o[:, n_main + KR_LANE:n_main + KR_LANE + QK_ROPE] = win_ref[:, n_main:].astype(BF16)
    wout_o[...] = wout_ref[...].astype(BF16)
    wuq_o[...] = jnp.zeros_like(wuq_o)
    wuk_o[...] = jnp.zeros_like(wuk_o)
    wuv_o[...] = jnp.zeros_like(wuv_o)
    for h in range(D_HEADS):
        wuq_o[:, h * HEAD_PAD:h * HEAD_PAD + QK_NOPE + QK_ROPE] = wuq_ref[:, h, :].astype(BF16)
        wuk_o[:, h * HEAD_PAD:h * HEAD_PAD + QK_NOPE] = wuk_ref[:, h, :].astype(BF16)
        v0 = h * HEAD_PAD + (h % 2) * V_DIM
        wuv_o[:, v0:v0 + V_DIM] = wuv_ref[:, h, :].astype(BF16)
    gate_o[...] = jnp.zeros_like(gate_o)
    per = (W_C // 2) // C_BLOCK
    for direction in range(2):
        for g, w_ref in enumerate((wa_ref, wx_ref)):
            for h in range(C_HEADS):
                j, hh = divmod(h, per)
                blk = slice(hh * C_BLOCK, (hh + 1) * C_BLOCK)
                gate_o[direction, g, j, blk, blk] = w_ref[direction, h].astype(BF16)


def _prep_odd_weights(w_in, w_a, w_x, w_uq, w_uk, w_uv, w_out):
    ins = [w_in, w_a, w_x, w_uq, w_uk, w_uv, w_out]
    outs = [
        jax.ShapeDtypeStruct((D, ODD_IN_PAD), BF16),
        jax.ShapeDtypeStruct((2, 2, 2, W_C // 2, W_C // 2), BF16),
        jax.ShapeDtypeStruct((Q_LORA, D_HEADS * HEAD_PAD), BF16),
        jax.ShapeDtypeStruct((KV_LORA, D_HEADS * HEAD_PAD), BF16),
        jax.ShapeDtypeStruct((KV_LORA, D_HEADS * HEAD_PAD), BF16),
        jax.ShapeDtypeStruct((D, D), BF16),
    ]
    return pl.pallas_call(
        _odd_prep_kernel,
        in_specs=[pl.BlockSpec(a.shape, lambda n=a.ndim: (0,) * n) for a in ins],
        out_specs=[pl.BlockSpec(o.shape, lambda n=len(o.shape): (0,) * n) for o in outs],
        out_shape=outs,
        compiler_params=pltpu.CompilerParams(vmem_limit_bytes=VMEM_LIMIT),
        name="odd_weights_bf16",
    )(*ins)


def kernel(x_prompt, x_sample, cache_mla_ckv, cache_mla_krope, state_rglru, c, c_ctx, w_mod, b_mod, ln1_g, ln1_b, ln2_g, ln2_b, even_w_in, even_w_s, even_b_s, even_w_pool, even_pool_scale, even_w_out, odd_w_in, rg_conv_w, rg_conv_b, rg_w_a, rg_b_a, rg_w_x, rg_b_x, rg_lam, mla_q_g, mla_kv_g, mla_w_uq, mla_w_uk, mla_w_uv, odd_w_out, ffn_w_up, ffn_conv_w, ffn_conv_b, ffn_w_down):
    batch, seq, _ = x_prompt.shape
    dec_batch, dec_seq, _ = x_sample.shape
    assert TILE_ROWS % seq == 0 and dec_seq == TILE_ROWS and dec_batch + 1 <= SUBLANES

    mod = _modulation(c_ctx, c, w_mod, b_mod)

    wup_bf, wdown_bf = _prep_ffn_weights(ffn_w_up, ffn_w_down)
    n_ch = 2 * D_FF // FF_CHUNK
    wf = {"ln_g": ln2_g, "ln_b": ln2_b,
          "w_up": wup_bf, "conv_w": ffn_conv_w.reshape(DEPTH, 3, n_ch, FF_CHUNK),
          "conv_b": ffn_conv_b.reshape(DEPTH, n_ch, FF_CHUNK), "w_down": wdown_bf}
    ln1 = {"ln_g": ln1_g, "ln_b": ln1_b}

    xp = x_prompt.reshape(batch * seq, D)
    xs = x_sample.reshape(dec_batch * dec_seq, D)
    new_ckv = new_kr = new_state = None
    for l in range(DEPTH):
        if l % 2 == 0:
            e = l // 2
            w = dict(ln1, w_in=even_w_in, w_s=even_w_s, b_s_t=even_b_s[e].T, w_pool=even_w_pool,
                     pool_scale=even_pool_scale[e][None, :], w_out=even_w_out)
            xp = _even_layer(xp, mod, w, l, seq, 0, 0, f"even{l}_ctx")
            xs = _even_layer(xs, mod, w, l, dec_seq, 1, 1, f"even{l}_lat")
        else:
            o = l // 2
            w_in_bf, w_gate_bf, w_uq_bf, w_uk_bf, w_uv_bf, w_out_bf = _prep_odd_weights(
                odd_w_in[o], rg_w_a[o], rg_w_x[o], mla_w_uq[o], mla_w_uk[o], mla_w_uv[o], odd_w_out[o])
            b_gate = jnp.stack([rg_b_a[o], rg_b_x[o]], axis=1)
            w = dict(ln1,
                     w_in=w_in_bf, conv_w=rg_conv_w[o], conv_b=rg_conv_b[o][None, :],
                     w_gate=w_gate_bf, b_gate=b_gate, lam=rg_lam[o],
                     q_g=mla_q_g[o][None, :], kv_g=mla_kv_g[o][None, :],
                     w_uq=w_uq_bf, w_uk=w_uk_bf, w_uv=w_uv_bf, w_out=w_out_bf)
            xp, ckv, kr, st = _odd_layer(xp, mod, w, l, seq, 0, 0, f"odd{l}_ctx")
            new_ckv = ckv.reshape(batch, 1, seq, KV_LORA)
            new_kr = kr.reshape(batch, 1, seq, QK_ROPE)
            new_state = st.reshape(batch, 1, 2, W_C)
            cos_t, sin_t = _axial_rope_tables(dec_seq // GRID_W)
            latent_inputs = {
                "cache_ckv": cache_mla_ckv[:, o],
                "cache_kr": jnp.pad(cache_mla_krope[:, o],
                                    ((0, 0), (0, 0), (KR_LANE, HEAD_PAD - KR_LANE - QK_ROPE))),
                "state": state_rglru[:, o].reshape(dec_batch, 2, W_C // LANES, LANES),
                "cos": cos_t, "sin": sin_t,
            }
            (xs,) = _odd_layer(xs, mod, w, l, dec_seq, 1, 1, f"odd{l}_lat", latent_inputs)
        xp = _ffn_layer(xp, mod, wf, l, seq, 0, 0, f"ffn{l}_ctx")
        xs = _ffn_layer(xs, mod, wf, l, dec_seq, 1, 1, f"ffn{l}_lat")
    return (xp.reshape(batch, seq, D), xs.reshape(dec_batch, dec_seq, D), new_ckv, new_kr, new_state)
```

```python
import functools
import math

import jax
import jax.numpy as jnp
import numpy as np
from jax import lax
from jax.experimental import pallas as pl
from jax.experimental.pallas import tpu as pltpu

F32 = jnp.float32
BF16 = jnp.bfloat16

D = 1024
DEPTH = 2
GRID_W = 64
ALPHA = (2 * DEPTH) ** 0.25
LN_EPS = 1e-5
RMS_EPS = 1e-6
A_HEADS = 4
A_WIDTH = 512
A_GROUP_W = A_WIDTH // A_HEADS
CHUNK = 128
POOL_WINDOWS = (2, 4, 8, 16)
B_GROUP = 128
B_WIDTH = 512
W_C = 512
C_HEADS = 8
C_BLOCK = 64
RG_C = 8.0
D_HEADS = 8
Q_LORA = 384
KV_LORA = 256
QK_NOPE = 64
QK_ROPE = 32
V_DIM = 64
ROPE_BASE = 10000.0
ATTN_SCALE = 1.0 / math.sqrt(QK_NOPE + QK_ROPE)
D_FF = 2816

LANES = 128
SUBLANES = 8
HEAD_PAD = LANES
ODD_IN_PAD = 2 * W_C + Q_LORA + KV_LORA + HEAD_PAD
KR_LANE = QK_NOPE
TILE_ROWS = 1024
ROW_BLOCK = 256
FF_CHUNK = 256
FFN_DOT_ROWS = 128
FFN_RB = 64
EVEN_RB = 64
ODD_RB = 64
VMEM_LIMIT = 58 * 1024 * 1024


def _dot(a, b):
    return jnp.dot(a, b, preferred_element_type=F32)


def _gelu(x):
    return x * (0.5 * (1.0 + jnp.tanh(0.7978845608028654 * (x + 0.044715 * (x * x * x)))))


def _layernorm(r, g, b):
    mu = jnp.mean(r, axis=-1, keepdims=True)
    c = r - mu
    var = jnp.mean(c * c, axis=-1, keepdims=True)
    return c * lax.rsqrt(var + LN_EPS) * g + b


def _shift_rows(x, d, pos, seq_len):
    n = x.shape[0]
    y = pltpu.roll(x, (-d) % n, 0)
    valid = (pos < seq_len - d) if d > 0 else (pos >= -d)
    return jnp.where(valid, y, 0.0)


def _mod_kernel(cctx_ref, c_ref, w_ref, b_ref, o_ref, s_ref):
    n_lat = c_ref.shape[0]
    s_ref[...] = jnp.zeros_like(s_ref)
    s_ref[0:1, :] = cctx_ref[...]
    s_ref[1:1 + n_lat, :] = c_ref[...]
    s = s_ref[...]
    s = s * jax.nn.sigmoid(s)
    bias = b_ref[pl.ds(pl.program_id(0), 1), :]
    o_ref[...] = _dot(s.astype(BF16), w_ref[...].astype(BF16)) + bias


def _modulation(c_ctx, c, w_mod, b_mod):
    nc = 1536
    return pl.pallas_call(
        _mod_kernel,
        grid=(DEPTH, 6 * D // nc),
        in_specs=[
            pl.BlockSpec((1, D), lambda l, j: (0, 0)),
            pl.BlockSpec(c.shape, lambda l, j: (0, 0)),
            pl.BlockSpec((None, D, nc), lambda l, j: (l, 0, j)),
            pl.BlockSpec((DEPTH, nc), lambda l, j: (0, j)),
        ],
        out_specs=pl.BlockSpec((None, SUBLANES, nc), lambda l, j: (l, 0, j)),
        out_shape=jax.ShapeDtypeStruct((DEPTH, SUBLANES, 6 * D), F32),
        scratch_shapes=[pltpu.VMEM((SUBLANES, D), F32)],
        compiler_params=pltpu.CompilerParams(
            dimension_semantics=("arbitrary", "arbitrary"), vmem_limit_bytes=VMEM_LIMIT),
        name="modulation",
    )(c_ctx.reshape(1, D), c, w_mod, b_mod)


def _mod_rows(mod_ref, row0, row_step):
    row = row0 + pl.program_id(0) * row_step
    m = mod_ref[pl.ds(row, 1), :]
    return m[:, 0:D], m[:, D:2 * D], m[:, 2 * D:3 * D]


def _const_spec(shape):
    nd = len(shape)
    return pl.BlockSpec(shape, lambda *_: (0,) * nd, pipeline_mode=pl.Buffered(1))


def _layer_spec(shape, l):
    nd = len(shape) - 1
    return pl.BlockSpec((None,) + tuple(shape[1:]), lambda *_: (l,) + (0,) * nd, pipeline_mode=pl.Buffered(1))


def _mod_spec(l, half):
    return pl.BlockSpec((None, SUBLANES, 3 * D), lambda *_: (l, 0, half))


def _interleave(primary, secondary):
    n_p, n_s = len(primary), len(secondary)
    done = 0
    for k, thunk in enumerate(primary):
        thunk()
        want = ((k + 1) * n_s) // n_p
        while done < want:
            secondary[done]()
            done += 1
    while done < n_s:
        secondary[done]()
        done += 1


def _pool_block(p_ref, gi, r0, half, seq_len, n_rows):
    cols = slice(gi * B_GROUP, (gi + 1) * B_GROUP)
    zero = jnp.zeros((SUBLANES, B_GROUP), F32)
    top = zero if r0 == 0 else p_ref[r0 - SUBLANES:r0, cols]
    bot = zero if r0 + EVEN_RB == n_rows else p_ref[r0 + EVEN_RB:r0 + EVEN_RB + SUBLANES, cols]
    pg = jnp.concatenate([top, p_ref[r0:r0 + EVEN_RB, cols], bot], axis=0)
    pos = (lax.broadcasted_iota(jnp.int32, pg.shape, 0) + (r0 - SUBLANES)) & (seq_len - 1)
    fwd = pg
    bwd = _shift_rows(pg, -1, pos, seq_len)
    k = 1
    while k < half:
        fwd = fwd + _shift_rows(fwd, k, pos, seq_len)
        bwd = bwd + _shift_rows(bwd, -k, pos, seq_len)
        k *= 2
    cnt = jnp.minimum(pos + half, seq_len) - jnp.maximum(pos - half, 0)
    pooled = (fwd + bwd) / cnt.astype(F32) - pg
    return pooled[SUBLANES:SUBLANES + EVEN_RB]


def _even_kernel(x_ref, mod_ref, win_f32_ref, ws_ref, bst_ref, wpool_ref, pscale_ref, wout_f32_ref,
                 lng_ref, lnb_ref, o_ref, u_ref, vn_ref, p_ref, ycat_ref, win_ref, wout_ref,
                 *, layer, seq_len, row0, row_step):
    T = x_ref.shape[0]
    n_blocks = T // ROW_BLOCK
    shift, scale, gate = _mod_rows(mod_ref, row0, row_step)

    @pl.when(pl.program_id(0) == 0)
    def _():
        def cast(rb, carry):
            rows = pl.ds(pl.multiple_of(rb * ROW_BLOCK, ROW_BLOCK), ROW_BLOCK)
            win_ref[rows, :] = win_f32_ref[rows, :].astype(BF16)
            wout_ref[rows, :] = wout_f32_ref[rows, :].astype(BF16)
            return carry

        lax.fori_loop(0, D // ROW_BLOCK, cast, 0)

    def stage_a(rb):
        rows = slice(rb * ROW_BLOCK, (rb + 1) * ROW_BLOCK)
        st = {}

        def t_p():
            st["hb"] = (x_ref[rows, :] * (1.0 + scale) + shift).astype(BF16)
            p_ref[rows, :] = _dot(st["hb"], win_ref[:, 2 * A_WIDTH:])

        def t_u():
            u_ref[rows, :] = _gelu(_dot(st["hb"], win_ref[:, 0:A_WIDTH]))

        def t_v():
            v = _gelu(_dot(st["hb"], win_ref[:, A_WIDTH:2 * A_WIDTH]))
            mu = jnp.mean(v, axis=-1, keepdims=True)
            c = v - mu
            var = jnp.mean(c * c, axis=-1, keepdims=True)
            vn_ref[rows, :] = (c * lax.rsqrt(var + LN_EPS)).astype(BF16)

        return [t_p, t_u, t_v]

    def stage_b(rb):
        def t_gate(n):
            def run():
                rows = slice(n * CHUNK, (n + 1) * CHUNK)
                for h in range(A_HEADS):
                    cols = slice(h * A_GROUP_W, (h + 1) * A_GROUP_W)
                    s = _dot(ws_ref[h].astype(BF16), vn_ref[rows, cols]) + bst_ref[:, h:h + 1]
                    ycat_ref[rows, cols] = (u_ref[rows, cols] * s).astype(BF16)
            return run

        def t_pool(gi):
            def run():
                cols = slice(gi * B_GROUP, (gi + 1) * B_GROUP)
                pooled = jnp.concatenate(
                    [_pool_block(p_ref, gi, rb * ROW_BLOCK + rs * EVEN_RB, POOL_WINDOWS[gi] // 2, seq_len, T)
                     for rs in range(ROW_BLOCK // EVEN_RB)], axis=0)
                yb = _dot(pooled.astype(BF16), wpool_ref[gi].astype(BF16)) * pscale_ref[:, cols]
                ycat_ref[rb * ROW_BLOCK:(rb + 1) * ROW_BLOCK, A_WIDTH + gi * B_GROUP:A_WIDTH + (gi + 1) * B_GROUP] = (
                    yb.astype(BF16))
            return run

        chunks = range(rb * ROW_BLOCK // CHUNK, (rb + 1) * ROW_BLOCK // CHUNK)
        return [t_gate(n) for n in chunks] + [t_pool(gi) for gi in range(len(POOL_WINDOWS))]

    def out_block(rb):
        rows = slice(rb * ROW_BLOCK, (rb + 1) * ROW_BLOCK)
        d = _dot(ycat_ref[rows, :], wout_ref[...])
        r = ALPHA * x_ref[rows, :] + gate * d
        o_ref[rows, :] = _layernorm(r, lng_ref[layer:layer + 1, :], lnb_ref[layer:layer + 1, :])

    for t in stage_a(0):
        t()
    for rb in range(n_blocks):
        nxt = stage_a(rb + 1) if rb + 1 < n_blocks else []
        if nxt:
            nxt[0]()
        _interleave(stage_b(rb), nxt[1:])
        out_block(rb)


def _even_layer(x, mod, w, l, seq_len, row0, row_step, name):
    n_rows = x.shape[0]
    e = l // 2
    kern = functools.partial(_even_kernel, layer=l, seq_len=seq_len, row0=row0, row_step=row_step)
    return pl.pallas_call(
        kern,
        grid=(n_rows // TILE_ROWS,),
        in_specs=[
            pl.BlockSpec((TILE_ROWS, D), lambda i: (i, 0)),
            _mod_spec(l, 0),
            _layer_spec(w["w_in"].shape, e),
            _layer_spec(w["w_s"].shape, e),
            _const_spec(w["b_s_t"].shape),
            _layer_spec(w["w_pool"].shape, e),
            _const_spec(w["pool_scale"].shape),
            _layer_spec(w["w_out"].shape, e),
            _const_spec(w["ln_g"].shape),
            _const_spec(w["ln_b"].shape),
        ],
        out_specs=pl.BlockSpec((TILE_ROWS, D), lambda i: (i, 0)),
        out_shape=jax.ShapeDtypeStruct((n_rows, D), F32),
        scratch_shapes=[
            pltpu.VMEM((TILE_ROWS, A_WIDTH), F32),
            pltpu.VMEM((TILE_ROWS, A_WIDTH), BF16),
            pltpu.VMEM((TILE_ROWS, B_WIDTH), F32),
            pltpu.VMEM((TILE_ROWS, D), BF16),
            pltpu.VMEM((D, 2 * A_WIDTH + B_WIDTH), BF16),
            pltpu.VMEM((D, D), BF16),
        ],
        compiler_params=pltpu.CompilerParams(
            dimension_semantics=("arbitrary",), vmem_limit_bytes=VMEM_LIMIT),
        name=name,
    )(x, mod, w["w_in"], w["w_s"], w["b_s_t"], w["w_pool"], w["pool_scale"], w["w_out"],
      w["ln_g"], w["ln_b"])


def _conv3_block(z_ref, r0, lanes, cw, cb, seq_len):
    nv = FFN_RB // SUBLANES
    z3 = z_ref[r0:r0 + FFN_RB, lanes].reshape(nv, SUBLANES, LANES)
    sub = lax.broadcasted_iota(jnp.int32, (nv, SUBLANES, LANES), 1)
    down = pltpu.roll(z3, 1, 1)
    up = pltpu.roll(z3, SUBLANES - 1, 1)
    if r0 % seq_len == 0:
        prev = jnp.zeros((1, SUBLANES, LANES), F32)
    else:
        prev = pltpu.roll(z_ref[r0 - SUBLANES:r0, lanes], 1, 0)[None]
    if (r0 + FFN_RB) % seq_len == 0:
        nxt = jnp.zeros((1, SUBLANES, LANES), F32)
    else:
        nxt = pltpu.roll(z_ref[r0 + FFN_RB:r0 + FFN_RB + SUBLANES, lanes], SUBLANES - 1, 0)[None]
    zm = jnp.where(sub == 0, jnp.concatenate([prev, down[:-1]], axis=0), down)
    zp = jnp.where(sub == SUBLANES - 1, jnp.concatenate([up[1:], nxt], axis=0), up)
    y = cw[0][:, lanes] * zm + cw[1][:, lanes] * z3 + cw[2][:, lanes] * zp + cb[:, lanes]
    return y.reshape(FFN_RB, LANES)


def _ffn_kernel(x_ref, mod_ref, wup_ref, cw_ref, cb_ref, wd_ref, lng_ref, lnb_ref, o_ref,
                hb_ref, zg0_ref, zv0_ref, zg1_ref, zv1_ref, a_ref, *, layer, seq_len, row0, row_step):
    T = x_ref.shape[0]
    n_chunks = wd_ref.shape[0]
    shift, scale, gate = _mod_rows(mod_ref, row0, row_step)
    z_bufs = ((zg0_ref, zv0_ref), (zg1_ref, zv1_ref))

    def row_block(rb):
        return pl.ds(pl.multiple_of(rb * ROW_BLOCK, ROW_BLOCK), ROW_BLOCK)

    def modulate(rb, carry):
        rows = row_block(rb)
        hb_ref[rows, :] = (x_ref[rows, :] * (1.0 + scale) + shift).astype(BF16)
        return carry

    lax.fori_loop(0, T // ROW_BLOCK, modulate, 0)

    n_parts = T // FFN_DOT_ROWS
    blocks_per_part = FFN_DOT_ROWS // FFN_RB

    def up_proj(c, bufs, q):
        rows = slice(q * FFN_DOT_ROWS, (q + 1) * FFN_DOT_ROWS)
        bufs[0][rows, :] = _dot(hb_ref[rows, :], wup_ref[c])
        bufs[1][rows, :] = _dot(hb_ref[rows, :], wup_ref[c + n_chunks])

    def gate_blocks(c, bufs, q):
        cwg = [cw_ref[k, pl.ds(c, 1), :] for k in range(3)]
        cwv = [cw_ref[k, pl.ds(c + n_chunks, 1), :] for k in range(3)]
        cbg, cbv = cb_ref[pl.ds(c, 1), :], cb_ref[pl.ds(c + n_chunks, 1), :]
        for rb in range(q * blocks_per_part, (q + 1) * blocks_per_part):
            r0 = rb * FFN_RB
            for lt in range(FF_CHUNK // LANES):
                lanes = slice(lt * LANES, (lt + 1) * LANES)
                g = _conv3_block(bufs[0], r0, lanes, cwg, cbg, seq_len)
                v = _conv3_block(bufs[1], r0, lanes, cwv, cbv, seq_len)
                a_ref[c, r0:r0 + FFN_RB, lanes] = (g * jax.nn.sigmoid(g) * v).astype(BF16)

    for q in range(n_parts):
        up_proj(0, z_bufs[0], q)

    def chunk_pair(k, carry):
        c = 2 * k
        for q in range(n_parts):
            up_proj(c + 1, z_bufs[1], q)
            gate_blocks(c, z_bufs[0], q)
        for q in range(n_parts):
            up_proj(c + 2, z_bufs[0], q)
            gate_blocks(c + 1, z_bufs[1], q)
        return carry

    assert n_chunks % 2 == 1
    lax.fori_loop(0, n_chunks // 2, chunk_pair, 0)
    for q in range(n_parts):
        gate_blocks(n_chunks - 1, z_bufs[0], q)

    def out(rb, carry):
        rows = row_block(rb)
        acc = _dot(a_ref[0, rows, :], wd_ref[0])
        for c in range(1, n_chunks):
            acc = acc + _dot(a_ref[c, rows, :], wd_ref[c])
        r = ALPHA * x_ref[rows, :] + gate * acc
        o_ref[rows, :] = _layernorm(r, lng_ref[layer:layer + 1, :], lnb_ref[layer:layer + 1, :])
        return carry

    lax.fori_loop(0, T // ROW_BLOCK, out, 0)


def _ffn_layer(x, mod, w, l, seq_len, row0, row_step, name):
    n_rows = x.shape[0]
    n_chunks = D_FF // FF_CHUNK
    kern = functools.partial(_ffn_kernel, layer=l, seq_len=seq_len, row0=row0, row_step=row_step)
    params = [w["w_up"], w["conv_w"], w["conv_b"], w["w_down"]]
    return pl.pallas_call(
        kern,
        grid=(n_rows // TILE_ROWS,),
        in_specs=[pl.BlockSpec((TILE_ROWS, D), lambda i: (i, 0)), _mod_spec(l, 1)]
        + [_layer_spec(a.shape, l) for a in params]
        + [_const_spec(w["ln_g"].shape), _const_spec(w["ln_b"].shape)],
        out_specs=pl.BlockSpec((TILE_ROWS, D), lambda i: (i, 0)),
        out_shape=jax.ShapeDtypeStruct((n_rows, D), F32),
        scratch_shapes=[
            pltpu.VMEM((TILE_ROWS, D), BF16),
            pltpu.VMEM((TILE_ROWS, FF_CHUNK), F32),
            pltpu.VMEM((TILE_ROWS, FF_CHUNK), F32),
            pltpu.VMEM((TILE_ROWS, FF_CHUNK), F32),
            pltpu.VMEM((TILE_ROWS, FF_CHUNK), F32),
            pltpu.VMEM((n_chunks, TILE_ROWS, FF_CHUNK), BF16),
        ],
        compiler_params=pltpu.CompilerParams(
            dimension_semantics=("arbitrary",), vmem_limit_bytes=VMEM_LIMIT),
        name=name,
    )(x, mod, *params, w["ln_g"], w["ln_b"])


def _prep_up_kernel(w_ref, o_ref):
    for k in range(o_ref.shape[0]):
        o_ref[k] = w_ref[:, k * FF_CHUNK:(k + 1) * FF_CHUNK].astype(BF16)


def _prep_down_kernel(w_ref, o_ref):
    o_ref[...] = w_ref[...].astype(BF16)


def _prep_ffn_weights(w_up, w_down):
    n_ch = 2 * D_FF // FF_CHUNK
    up = pl.pallas_call(
        _prep_up_kernel,
        grid=(DEPTH, 2),
        in_specs=[pl.BlockSpec((None, D, D_FF), lambda l, j: (l, 0, j))],
        out_specs=pl.BlockSpec((None, n_ch // 2, D, FF_CHUNK), lambda l, j: (l, j, 0, 0)),
        out_shape=jax.ShapeDtypeStruct((DEPTH, n_ch, D, FF_CHUNK), BF16),
        compiler_params=pltpu.CompilerParams(
            dimension_semantics=("arbitrary", "arbitrary"), vmem_limit_bytes=VMEM_LIMIT),
        name="ffn_up_bf16",
    )(w_up)
    half = D_FF // 2
    down = pl.pallas_call(
        _prep_down_kernel,
        grid=(DEPTH, 2),
        in_specs=[pl.BlockSpec((None, half, D), lambda l, j: (l, j, 0))],
        out_specs=pl.BlockSpec((None, half, D), lambda l, j: (l, j, 0)),
        out_shape=jax.ShapeDtypeStruct((DEPTH, D_FF, D), BF16),
        compiler_params=pltpu.CompilerParams(
            dimension_semantics=("arbitrary", "arbitrary"), vmem_limit_bytes=VMEM_LIMIT),
        name="ffn_down_bf16",
    )(w_down)
    return up, down.reshape(DEPTH, n_ch // 2, FF_CHUNK, D)


def _conv4_block(xr_ref, cbk, r0, cw, cb, seq_len):
    nv = ODD_RB // SUBLANES
    z3 = xr_ref[cbk, r0:r0 + ODD_RB, :].reshape(nv, SUBLANES, LANES)
    sub = lax.broadcasted_iota(jnp.int32, (nv, SUBLANES, LANES), 1)
    down = pltpu.roll(z3, 1, 1)
    up1 = pltpu.roll(z3, SUBLANES - 1, 1)
    up2 = pltpu.roll(z3, SUBLANES - 2, 1)
    zero = jnp.zeros((1, SUBLANES, LANES), F32)
    if r0 % seq_len == 0:
        prev = zero
    else:
        prev = pltpu.roll(xr_ref[cbk, r0 - SUBLANES:r0, :], 1, 0)[None]
    if (r0 + ODD_RB) % seq_len == 0:
        nxt1 = nxt2 = zero
    else:
        nx = xr_ref[cbk, r0 + ODD_RB:r0 + ODD_RB + SUBLANES, :]
        nxt1 = pltpu.roll(nx, SUBLANES - 1, 0)[None]
        nxt2 = pltpu.roll(nx, SUBLANES - 2, 0)[None]
    zm = jnp.where(sub == 0, jnp.concatenate([prev, down[:-1]], axis=0), down)
    zp1 = jnp.where(sub == SUBLANES - 1, jnp.concatenate([up1[1:], nxt1], axis=0), up1)
    zp2 = jnp.where(sub >= SUBLANES - 2, jnp.concatenate([up2[1:], nxt2], axis=0), up2)
    y = cw[0:1, :] * zm + cw[1:2, :] * z3 + cw[2:3, :] * zp1 + cw[3:4, :] * zp2 + cb
    return y.reshape(ODD_RB, LANES)


def _scan_slabs(a, b, reverse):
    nv = a.shape[0] // SUBLANES
    a = a.reshape(nv, SUBLANES, LANES)
    b = b.reshape(nv, SUBLANES, LANES)
    sub = lax.broadcasted_iota(jnp.int32, (nv, SUBLANES, LANES), 1)
    for d in (1, 2, 4):
        if reverse:
            valid, sh = sub < SUBLANES - d, SUBLANES - d
        else:
            valid, sh = sub >= d, d
        a_prev = jnp.where(valid, pltpu.roll(a, sh, 1), 1.0)
        b_prev = jnp.where(valid, pltpu.roll(b, sh, 1), 0.0)
        b = b + a * b_prev
        a = a * a_prev
    return a.reshape(nv * SUBLANES, LANES), b.reshape(nv * SUBLANES, LANES)


def _scan_ends(a, b, posg, n_groups, reverse):
    n = a.shape[0]
    d = 1
    while d < n_groups:
        if reverse:
            valid, sh = posg < n_groups - d, n - d
        else:
            valid, sh = posg >= d, d
        a_prev = jnp.where(valid, pltpu.roll(a, sh, 0), 1.0)
        b_prev = jnp.where(valid, pltpu.roll(b, sh, 0), 0.0)
        b = b + a * b_prev
        a = a * a_prev
        d *= 2
    return a, b


def _rope_pairs(x, cos, sin_signed, even_lane):
    n = x.shape[1]
    swapped = jnp.where(even_lane, pltpu.roll(x, n - 1, 1), pltpu.roll(x, 1, 1))
    return x * cos + swapped * sin_signed


def _rmsnorm(x, g):
    return x * lax.rsqrt(jnp.mean(x * x, axis=-1, keepdims=True) + RMS_EPS) * g


def _odd_kernel(*refs, layer, seq_len, latent, past_len, row0, row_step):
    n_in = 21 if latent else 16
    n_out = 1 if latent else 4
    (x_ref, mod_ref, win_ref, cw_ref, cb_ref, wgate_ref, bgate_ref, lam_ref, qg_ref, kvg_ref,
     wuq_ref, wuk_ref, wuv_ref, wout_ref, lng_ref, lnb_ref) = refs[:16]
    if latent:
        cckv_ref, ckr_ref, st_ref, cos_ref, sin_ref = refs[16:21]
        (o_ref,) = refs[n_in:n_in + n_out]
    else:
        o_ref, ckv_out_ref, kr_out_ref, st_out_ref = refs[n_in:n_in + n_out]
    (xr_ref, gr_ref, xc_ref, pre_ref, a_ref, b_ref, hrep_ref, q_ref, k_ref, v_ref, yd_ref) = refs[n_in + n_out:]

    T = x_ref.shape[0]
    n_blocks = T // ROW_BLOCK
    n_tiles = W_C // LANES
    half_w = W_C // 2
    shift, scale, gate = _mod_rows(mod_ref, row0, row_step)
    o_q = 2 * W_C
    o_kv = o_q + Q_LORA
    o_kr = o_kv + KV_LORA
    pair_w = 2 * HEAD_PAD
    even_lane = (lax.broadcasted_iota(jnp.int32, (ROW_BLOCK, HEAD_PAD), 1) & 1) == 0
    softplus = []
    for direction in range(2):
        neg = -lam_ref[direction:direction + 1, :]
        softplus.append(jnp.maximum(neg, 0.0) + jnp.log1p(jnp.exp(-jnp.abs(neg))))

    def stage_a(rb):
        rows = slice(rb * ROW_BLOCK, (rb + 1) * ROW_BLOCK)
        k_rows = slice(past_len + rb * ROW_BLOCK, past_len + (rb + 1) * ROW_BLOCK)
        st = {}

        def t_xr():
            st["hb"] = (x_ref[rows, :] * (1.0 + scale) + shift).astype(BF16)
            z = _dot(st["hb"], win_ref[:, 0:W_C])
            for cbk in range(n_tiles):
                xr_ref[cbk, rows, :] = z[:, cbk * LANES:(cbk + 1) * LANES]

        def t_gr():
            gr_ref[rows, :] = _dot(st["hb"], win_ref[:, W_C:o_q])

        def t_lat():
            hb = st["hb"]
            st["qn"] = _rmsnorm(_dot(hb, win_ref[:, o_q:o_kv]), qg_ref[...]).astype(BF16)
            ckv = _rmsnorm(_dot(hb, win_ref[:, o_kv:o_kr]), kvg_ref[...])
            kr = _dot(hb, win_ref[:, o_kr:o_kr + HEAD_PAD])
            if latent:
                st["cos"] = cos_ref[rows, :]
                st["sin"] = sin_ref[rows, :]
                kr = _rope_pairs(kr, st["cos"], st["sin"], even_lane)
            else:
                ckv_out_ref[rows, :] = ckv
                kr_out_ref[rows, :] = kr[:, KR_LANE:KR_LANE + QK_ROPE]
            st["kr"] = kr
            st["ckvb"] = ckv.astype(BF16)

        def t_v():
            v_ref[k_rows, :] = _dot(st["ckvb"], wuv_ref[...]).astype(BF16)

        def t_pair(pair):
            def run():
                pcols = slice(pair * pair_w, (pair + 1) * pair_w)
                q2 = _dot(st["qn"], wuq_ref[:, pcols])
                k2 = _dot(st["ckvb"], wuk_ref[:, pcols])
                for hh in range(2):
                    cols = slice(pair * pair_w + hh * HEAD_PAD, pair * pair_w + (hh + 1) * HEAD_PAD)
                    qh = q2[:, hh * HEAD_PAD:(hh + 1) * HEAD_PAD]
                    if latent:
                        qh = _rope_pairs(qh, st["cos"], st["sin"], even_lane)
                    q_ref[rows, cols] = qh.astype(BF16)
                    k_ref[k_rows, cols] = (k2[:, hh * HEAD_PAD:(hh + 1) * HEAD_PAD] + st["kr"]).astype(BF16)
            return run

        return [t_xr, t_gr, t_lat, t_v] + [t_pair(p) for p in range(D_HEADS // 2)]

    def conv_thunks(rb):
        par = rb % 2

        def t_conv(cbk):
            def run():
                cw = cw_ref[:, cbk * LANES:(cbk + 1) * LANES]
                cb = cb_ref[:, cbk * LANES:(cbk + 1) * LANES]
                for rs in range(ROW_BLOCK // ODD_RB):
                    r0 = rb * ROW_BLOCK + rs * ODD_RB
                    xc_ref[par, rs * ODD_RB:(rs + 1) * ODD_RB, cbk * LANES:(cbk + 1) * LANES] = _conv4_block(
                        xr_ref, cbk, r0, cw, cb, seq_len)
            return run

        return [t_conv(cbk) for cbk in range(n_tiles)]

    def gate_thunks(rb):
        par = rb % 2
        thunks = []
        for direction in range(2):
            for j in range(2):
                slot = (2 * direction + j) % 2

                def t_dot(direction=direction, j=j, slot=slot):
                    xcb = xc_ref[par, :, j * half_w:(j + 1) * half_w].astype(BF16)
                    pre_ref[slot, 0] = _dot(xcb, wgate_ref[direction, 0, j])
                    pre_ref[slot, 1] = _dot(xcb, wgate_ref[direction, 1, j])

                def t_post(direction=direction, j=j, slot=slot):
                    for rs in range(ROW_BLOCK // ODD_RB):
                        lrows = slice(rs * ODD_RB, (rs + 1) * ODD_RB)
                        grows = slice(rb * ROW_BLOCK + rs * ODD_RB, rb * ROW_BLOCK + (rs + 1) * ODD_RB)
                        for lt in range(half_w // LANES):
                            cbk = j * (half_w // LANES) + lt
                            lanes = slice(lt * LANES, (lt + 1) * LANES)
                            glanes = slice(cbk * LANES, (cbk + 1) * LANES)
                            xcv = xc_ref[par, lrows, glanes]
                            r = jax.nn.sigmoid(pre_ref[slot, 0, lrows, lanes] + bgate_ref[direction, 0:1, glanes])
                            i = jax.nn.sigmoid(pre_ref[slot, 1, lrows, lanes] + bgate_ref[direction, 1:2, glanes])
                            log_a = -RG_C * r * softplus[direction][:, glanes]
                            a = jnp.exp(log_a)
                            b = jnp.sqrt(jnp.tanh(-log_a) * (a * a + 1.0)) * i * xcv
                            a, b = _scan_slabs(a, b, direction == 1)
                            a_ref[direction, cbk, grows, :] = a
                            b_ref[direction, cbk, grows, :] = b

                thunks += [t_dot, t_post]
        return thunks

    def combine(direction, cbk, t0, n_rows, h0):
        reverse = direction == 1
        n_ends = n_rows // SUBLANES
        end_row = 0 if reverse else SUBLANES - 1
        posg = lax.broadcasted_iota(jnp.int32, (n_ends, LANES), 0)
        a_end = a_ref[direction, cbk, pl.ds(t0 + end_row, n_ends, stride=SUBLANES), :]
        b_end = b_ref[direction, cbk, pl.ds(t0 + end_row, n_ends, stride=SUBLANES), :]
        a_tot, b_tot = _scan_ends(a_end, b_end, posg, n_ends, reverse)
        ends = b_tot if h0 is None else a_tot * h0 + b_tot
        first = jnp.zeros((1, LANES), F32) if h0 is None else h0
        if reverse:
            carry_in = jnp.where(posg < n_ends - 1, pltpu.roll(ends, n_ends - 1, 0), first)
        else:
            carry_in = jnp.where(posg >= 1, pltpu.roll(ends, 1, 0), first)
        for rr in range(SUBLANES):
            hrep_ref[pl.ds(t0 + rr, n_ends, stride=SUBLANES), :] = carry_in
        for rs in range(n_rows // ODD_RB):
            rows = slice(t0 + rs * ODD_RB, t0 + (rs + 1) * ODD_RB)
            h = a_ref[direction, cbk, rows, :] * hrep_ref[rows, :] + b_ref[direction, cbk, rows, :]
            if direction == 0:
                xr_ref[cbk, rows, :] = h
            else:
                xr_ref[cbk, rows, :] = xr_ref[cbk, rows, :] + h
        return ends

    def attend_pair(q_rows, k_rows, pair):
        o_pair = None
        for h in (2 * pair, 2 * pair + 1):
            cols = slice(h * HEAD_PAD, (h + 1) * HEAD_PAD)
            s = lax.dot_general(q_ref[q_rows, cols], k_ref[k_rows, cols],
                                (((1,), (1,)), ((), ())), preferred_element_type=F32) * ATTN_SCALE
            m = jnp.max(s, axis=-1, keepdims=True)
            p = jnp.exp(s - m)
            l = jnp.sum(p, axis=-1, keepdims=True)
            o = _dot(p.astype(BF16), v_ref[k_rows, cols]) / l
            o_pair = o if o_pair is None else o_pair + o
        yd_ref[q_rows, pair * HEAD_PAD:(pair + 1) * HEAD_PAD] = o_pair.astype(BF16)

    def out_block(rows):
        yc = jnp.concatenate(
            [xr_ref[cbk, rows, :] * _gelu(gr_ref[rows, cbk * LANES:(cbk + 1) * LANES]) for cbk in range(n_tiles)],
            axis=1).astype(BF16)
        d = _dot(yc, wout_ref[0:W_C, :]) + _dot(yd_ref[rows, :], wout_ref[W_C:, :])
        r = ALPHA * x_ref[rows, :] + gate * d
        o_ref[rows, :] = _layernorm(r, lng_ref[layer:layer + 1, :], lnb_ref[layer:layer + 1, :])

    if not latent:
        def stage_b(s):
            t0 = s * seq_len
            rows = slice(t0, t0 + seq_len)
            thunks = conv_thunks(s) + gate_thunks(s)
            for direction in range(2):
                for cbk in range(n_tiles):
                    def t_comb(direction=direction, cbk=cbk):
                        ends = combine(direction, cbk, t0, seq_len, None)
                        last = 0 if direction == 1 else seq_len // SUBLANES - 1
                        st_out_ref[s, direction:direction + 1, cbk * LANES:(cbk + 1) * LANES] = ends[last:last + 1, :]
                    thunks.append(t_comb)
            return thunks

        assert seq_len == ROW_BLOCK
        for t in stage_a(0):
            t()
        for s in range(n_blocks):
            rows = slice(s * seq_len, (s + 1) * seq_len)
            attn = [functools.partial(attend_pair, rows, rows, pair) for pair in range(D_HEADS // 2)]
            nxt = stage_a(s + 1) if s + 1 < n_blocks else []
            mxu = []
            _interleave([functools.partial(mxu.append, t) for t in nxt] or [lambda: None],
                        [functools.partial(mxu.append, t) for t in attn])
            _interleave(stage_b(s), mxu)
            out_block(rows)
        return

    assert seq_len == T
    for t in stage_a(0):
        t()
    for rb in range(n_blocks):
        nxt = stage_a(rb + 1) if rb + 1 < n_blocks else []
        if nxt:
            nxt[0]()
        _interleave(conv_thunks(rb) + gate_thunks(rb), nxt[1:])

    def cached(cbk, carry):
        rows = pl.ds(pl.multiple_of(cbk * ROW_BLOCK, ROW_BLOCK), ROW_BLOCK)
        cckv = cckv_ref[rows, :].astype(BF16)
        ckr = ckr_ref[rows, :]
        v_ref[rows, :] = _dot(cckv, wuv_ref[...]).astype(BF16)
        for pair in range(D_HEADS // 2):
            pcols = slice(pair * pair_w, (pair + 1) * pair_w)
            k2 = _dot(cckv, wuk_ref[:, pcols])
            for hh in range(2):
                cols = slice(pair * pair_w + hh * HEAD_PAD, pair * pair_w + (hh + 1) * HEAD_PAD)
                k_ref[rows, cols] = (k2[:, hh * HEAD_PAD:(hh + 1) * HEAD_PAD] + ckr).astype(BF16)
        return carry

    lax.fori_loop(0, past_len // ROW_BLOCK, cached, 0)

    n_keys = past_len + seq_len

    def tail(qb, carry):
        q_rows = pl.ds(pl.multiple_of(qb * ROW_BLOCK, ROW_BLOCK), ROW_BLOCK)
        k_rows = slice(0, n_keys)
        rg = []
        for direction in range(2):
            def t_comb(direction=direction):
                h0 = st_ref[direction, pl.ds(qb, 1), :]
                combine(direction, qb, 0, T, h0)
            rg.append(t_comb)
        _interleave([functools.partial(attend_pair, q_rows, k_rows, pair) for pair in range(D_HEADS // 2)], rg)
        return carry

    assert n_tiles == n_blocks
    lax.fori_loop(0, n_blocks, tail, 0)

    def out(rb, carry):
        out_block(pl.ds(pl.multiple_of(rb * ROW_BLOCK, ROW_BLOCK), ROW_BLOCK))
        return carry

    lax.fori_loop(0, n_blocks, out, 0)


def _odd_layer(x, mod, w, l, seq_len, row0, row_step, name, latent_inputs=None):
    n_rows = x.shape[0]
    n_tiles = n_rows // TILE_ROWS
    latent = latent_inputs is not None
    past_len = latent_inputs["cache_ckv"].shape[1] if latent else 0
    n_seq = TILE_ROWS // seq_len
    kern = functools.partial(_odd_kernel, layer=l, seq_len=seq_len, latent=latent, past_len=past_len,
                             row0=row0, row_step=row_step)
    weights = [w["w_in"], w["conv_w"], w["conv_b"], w["w_gate"], w["b_gate"], w["lam"], w["q_g"], w["kv_g"],
               w["w_uq"], w["w_uk"], w["w_uv"], w["w_out"]]
    in_specs = [pl.BlockSpec((TILE_ROWS, D), lambda i: (i, 0), pipeline_mode=pl.Buffered(1)),
                _mod_spec(l, 0)]
    in_specs += [_const_spec(a.shape) for a in weights]
    in_specs += [_const_spec(w["ln_g"].shape), _const_spec(w["ln_b"].shape)]
    args = [x, mod] + weights + [w["ln_g"], w["ln_b"]]
    out_specs = [pl.BlockSpec((TILE_ROWS, D), lambda i: (i, 0), pipeline_mode=pl.Buffered(1))]
    out_shape = [jax.ShapeDtypeStruct((n_rows, D), F32)]
    if latent:
        li = latent_inputs
        in_specs += [
            pl.BlockSpec((None, past_len, KV_LORA), lambda i: (i, 0, 0)),
            pl.BlockSpec((None, past_len, HEAD_PAD), lambda i: (i, 0, 0)),
            pl.BlockSpec((None, 2, W_C // LANES, LANES), lambda i: (i, 0, 0, 0)),
            _const_spec(li["cos"].shape),
            _const_spec(li["sin"].shape),
        ]
        args += [li["cache_ckv"], li["cache_kr"], li["state"], li["cos"], li["sin"]]
    else:
        out_specs += [
            pl.BlockSpec((TILE_ROWS, KV_LORA), lambda i: (i, 0)),
            pl.BlockSpec((TILE_ROWS, QK_ROPE), lambda i: (i, 0)),
            pl.BlockSpec((n_seq, 2, W_C), lambda i: (i, 0, 0)),
        ]
        out_shape += [
            jax.ShapeDtypeStruct((n_rows, KV_LORA), F32),
            jax.ShapeDtypeStruct((n_rows, QK_ROPE), F32),
            jax.ShapeDtypeStruct((n_rows // seq_len, 2, W_C), F32),
        ]
    n_keys_buf = past_len + TILE_ROWS
    scratch = [
        pltpu.VMEM((W_C // LANES, TILE_ROWS, LANES), F32),
        pltpu.VMEM((TILE_ROWS, W_C), F32),
        pltpu.VMEM((2, ROW_BLOCK, W_C), F32),
        pltpu.VMEM((2, 2, ROW_BLOCK, W_C // 2), F32),
        pltpu.VMEM((2, W_C // LANES, TILE_ROWS, LANES), F32),
        pltpu.VMEM((2, W_C // LANES, TILE_ROWS, LANES), F32),
        pltpu.VMEM((TILE_ROWS, LANES), F32),
        pltpu.VMEM((TILE_ROWS, D_HEADS * HEAD_PAD), BF16),
        pltpu.VMEM((n_keys_buf, D_HEADS * HEAD_PAD), BF16),
        pltpu.VMEM((n_keys_buf, D_HEADS * HEAD_PAD), BF16),
        pltpu.VMEM((TILE_ROWS, W_C), BF16),
    ]
    return pl.pallas_call(
        kern,
        grid=(n_tiles,),
        in_specs=in_specs,
        out_specs=out_specs,
        out_shape=out_shape,
        scratch_shapes=scratch,
        compiler_params=pltpu.CompilerParams(
            dimension_semantics=("arbitrary",), vmem_limit_bytes=VMEM_LIMIT),
        name=name,
    )(*args)


def _axial_rope_tables(rows):
    half = QK_ROPE // 2
    inv = (ROPE_BASE ** (-np.arange(0, half, 2, dtype=np.float32) / half)).astype(np.float32)
    r = np.repeat(np.arange(rows, dtype=np.float32), GRID_W)
    col = np.tile(np.arange(GRID_W, dtype=np.float32), rows)
    ang = np.concatenate([r[:, None] * inv, col[:, None] * inv], axis=-1).astype(np.float32)
    cos = np.repeat(np.cos(ang), 2, axis=-1)
    sin = np.repeat(np.sin(ang), 2, axis=-1) * np.tile(np.array([-1.0, 1.0], np.float32), half)
    n = ang.shape[0]
    cos_t = np.ones((n, HEAD_PAD), np.float32)
    sin_t = np.zeros((n, HEAD_PAD), np.float32)
    cos_t[:, KR_LANE:KR_LANE + QK_ROPE] = cos
    sin_t[:, KR_LANE:KR_LANE + QK_ROPE] = sin
    return jnp.asarray(cos_t), jnp.asarray(sin_t)


def _odd_prep_kernel(win_ref, wa_ref, wx_ref, wuq_ref, wuk_ref, wuv_ref, wout_ref,
                     win_o, gate_o, wuq_o, wuk_o, wuv_o, wout_o):
    n_main = 2 * W_C + Q_LORA + KV_LORA
    win_o[:, 0:n_main] = win_ref[:, 0:n_main].astype(BF16)
    win_o[:, n_main:] = jnp.zeros((D, HEAD_PAD), BF16)
    win_o[:, n_main + KR_LANE:n_main + KR_LANE + QK_ROPE] = win_ref[:, n_main:].astype(BF16)
    wout_o[...] = wout_ref[...].astype(BF16)
    wuq_o[...] = jnp.zeros_like(wuq_o)
    wuk_o[...] = jnp.zeros_like(wuk_o)
    wuv_o[...] = jnp.zeros_like(wuv_o)
    for h in range(D_HEADS):
        wuq_o[:, h * HEAD_PAD:h * HEAD_PAD + QK_NOPE + QK_ROPE] = wuq_ref[:, h, :].astype(BF16)
        wuk_o[:, h * HEAD_PAD:h * HEAD_PAD + QK_NOPE] = wuk_ref[:, h, :].astype(BF16)
        v0 = h * HEAD_PAD + (h % 2) * V_DIM
        wuv_o[:, v0:v0 + V_DIM] = wuv_ref[:, h, :].astype(BF16)
    gate_o[...] = jnp.zeros_like(gate_o)
    per = (W_C // 2) // C_BLOCK
    for direction in range(2):
        for g, w_ref in enumerate((wa_ref, wx_ref)):
            for h in range(C_HEADS):
                j, hh = divmod(h, per)
                blk = slice(hh * C_BLOCK, (hh + 1) * C_BLOCK)
                gate_o[direction, g, j, blk, blk] = w_ref[direction, h].astype(BF16)


def _prep_odd_weights(w_in, w_a, w_x, w_uq, w_uk, w_uv, w_out):
    ins = [w_in, w_a, w_x, w_uq, w_uk, w_uv, w_out]
    outs = [
        jax.ShapeDtypeStruct((D, ODD_IN_PAD), BF16),
        jax.ShapeDtypeStruct((2, 2, 2, W_C // 2, W_C // 2), BF16),
        jax.ShapeDtypeStruct((Q_LORA, D_HEADS * HEAD_PAD), BF16),
        jax.ShapeDtypeStruct((KV_LORA, D_HEADS * HEAD_PAD), BF16),
        jax.ShapeDtypeStruct((KV_LORA, D_HEADS * HEAD_PAD), BF16),
        jax.ShapeDtypeStruct((D, D), BF16),
    ]
    return pl.pallas_call(
        _odd_prep_kernel,
        in_specs=[pl.BlockSpec(a.shape, lambda n=a.ndim: (0,) * n) for a in ins],
        out_specs=[pl.BlockSpec(o.shape, lambda n=len(o.shape): (0,) * n) for o in outs],
        out_shape=outs,
        compiler_params=pltpu.CompilerParams(vmem_limit_bytes=VMEM_LIMIT),
        name="odd_weights_bf16",
    )(*ins)


def kernel(x_prompt, x_sample, cache_mla_ckv, cache_mla_krope, state_rglru, c, c_ctx, w_mod, b_mod, ln1_g, ln1_b, ln2_g, ln2_b, even_w_in, even_w_s, even_b_s, even_w_pool, even_pool_scale, even_w_out, odd_w_in, rg_conv_w, rg_conv_b, rg_w_a, rg_b_a, rg_w_x, rg_b_x, rg_lam, mla_q_g, mla_kv_g, mla_w_uq, mla_w_uk, mla_w_uv, odd_w_out, ffn_w_up, ffn_conv_w, ffn_conv_b, ffn_w_down):
    batch, seq, _ = x_prompt.shape
    dec_batch, dec_seq, _ = x_sample.shape
    assert TILE_ROWS % seq == 0 and dec_seq == TILE_ROWS and dec_batch + 1 <= SUBLANES

    mod = _modulation(c_ctx, c, w_mod, b_mod)

    wup_bf, wdown_bf = _prep_ffn_weights(ffn_w_up, ffn_w_down)
    n_ch = 2 * D_FF // FF_CHUNK
    wf = {"ln_g": ln2_g, "ln_b": ln2_b,
          "w_up": wup_bf, "conv_w": ffn_conv_w.reshape(DEPTH, 3, n_ch, FF_CHUNK),
          "conv_b": ffn_conv_b.reshape(DEPTH, n_ch, FF_CHUNK), "w_down": wdown_bf}
    ln1 = {"ln_g": ln1_g, "ln_b": ln1_b}

    xp = x_prompt.reshape(batch * seq, D)
    xs = x_sample.reshape(dec_batch * dec_seq, D)
    new_ckv = new_kr = new_state = None
    for l in range(DEPTH):
        if l % 2 == 0:
            e = l // 2
            w = dict(ln1, w_in=even_w_in, w_s=even_w_s, b_s_t=even_b_s[e].T, w_pool=even_w_pool,
                     pool_scale=even_pool_scale[e][None, :], w_out=even_w_out)
            xp = _even_layer(xp, mod, w, l, seq, 0, 0, f"even{l}_ctx")
            xs = _even_layer(xs, mod, w, l, dec_seq, 1, 1, f"even{l}_lat")
        else:
            o = l // 2
            w_in_bf, w_gate_bf, w_uq_bf, w_uk_bf, w_uv_bf, w_out_bf = _prep_odd_weights(
                odd_w_in[o], rg_w_a[o], rg_w_x[o], mla_w_uq[o], mla_w_uk[o], mla_w_uv[o], odd_w_out[o])
            b_gate = jnp.stack([rg_b_a[o], rg_b_x[o]], axis=1)
            w = dict(ln1,
                     w_in=w_in_bf, conv_w=rg_conv_w[o], conv_b=rg_conv_b[o][None, :],
                     w_gate=w_gate_bf, b_gate=b_gate, lam=rg_lam[o],
                     q_g=mla_q_g[o][None, :], kv_g=mla_kv_g[o][None, :],
                     w_uq=w_uq_bf, w_uk=w_uk_bf, w_uv=w_uv_bf, w_out=w_out_bf)
            xp, ckv, kr, st = _odd_layer(xp, mod, w, l, seq, 0, 0, f"odd{l}_ctx")
            new_ckv = ckv.reshape(batch, 1, seq, KV_LORA)
            new_kr = kr.reshape(batch, 1, seq, QK_ROPE)
            new_state = st.reshape(batch, 1, 2, W_C)
            cos_t, sin_t = _axial_rope_tables(dec_seq // GRID_W)
            latent_inputs = {
                "cache_ckv": cache_mla_ckv[:, o],
                "cache_kr": jnp.pad(cache_mla_krope[:, o],
                                    ((0, 0), (0, 0), (KR_LANE, HEAD_PAD - KR_LANE - QK_ROPE))),
                "state": state_rglru[:, o].reshape(dec_batch, 2, W_C // LANES, LANES),
                "cos": cos_t, "sin": sin_t,
            }
            (xs,) = _odd_layer(xs, mod, w, l, dec_seq, 1, 1, f"odd{l}_lat", latent_inputs)
        xp = _ffn_layer(xp, mod, wf, l, seq, 0, 0, f"ffn{l}_ctx")
        xs = _ffn_layer(xs, mod, wf, l, dec_seq, 1, 1, f"ffn{l}_lat")
    return (xp.reshape(batch, seq, D), xs.reshape(dec_batch, dec_seq, D), new_ckv, new_kr, new_state)
```

```python
import functools
import math

import jax
import jax.numpy as jnp
import numpy as np
from jax import lax
from jax.experimental import pallas as pl
from jax.experimental.pallas import tpu as pltpu

F32 = jnp.float32
BF16 = jnp.bfloat16

D = 1024
DEPTH = 2
GRID_W = 64
ALPHA = (2 * DEPTH) ** 0.25
LN_EPS = 1e-5
RMS_EPS = 1e-6
A_HEADS = 4
A_WIDTH = 512
A_GROUP_W = A_WIDTH // A_HEADS
CHUNK = 128
POOL_WINDOWS = (2, 4, 8, 16)
B_GROUP = 128
B_WIDTH = 512
W_C = 512
C_HEADS = 8
C_BLOCK = 64
RG_C = 8.0
D_HEADS = 8
Q_LORA = 384
KV_LORA = 256
QK_NOPE = 64
QK_ROPE = 32
V_DIM = 64
ROPE_BASE = 10000.0
ATTN_SCALE = 1.0 / math.sqrt(QK_NOPE + QK_ROPE)
D_FF = 2816

LANES = 128
SUBLANES = 8
HEAD_PAD = LANES
ODD_IN_PAD = 2 * W_C + Q_LORA + KV_LORA + HEAD_PAD
KR_LANE = QK_NOPE
TILE_ROWS = 1024
ROW_BLOCK = 256
FF_CHUNK = 256
FFN_DOT_ROWS = 128
FFN_RB = 64
EVEN_RB = 64
ODD_RB = 64
VMEM_LIMIT = 58 * 1024 * 1024


def _dot(a, b):
    return jnp.dot(a, b, preferred_element_type=F32)


def _gelu(x):
    return x * (0.5 * (1.0 + jnp.tanh(0.7978845608028654 * (x + 0.044715 * (x * x * x)))))


def _layernorm(r, g, b):
    mu = jnp.mean(r, axis=-1, keepdims=True)
    c = r - mu
    var = jnp.mean(c * c, axis=-1, keepdims=True)
    return c * lax.rsqrt(var + LN_EPS) * g + b


def _shift_rows(x, d, pos, seq_len):
    n = x.shape[0]
    y = pltpu.roll(x, (-d) % n, 0)
    valid = (pos < seq_len - d) if d > 0 else (pos >= -d)
    return jnp.where(valid, y, 0.0)


def _mod_kernel(cctx_ref, c_ref, w_ref, b_ref, o_ref, s_ref):
    n_lat = c_ref.shape[0]
    s_ref[...] = jnp.zeros_like(s_ref)
    s_ref[0:1, :] = cctx_ref[...]
    s_ref[1:1 + n_lat, :] = c_ref[...]
    s = s_ref[...]
    s = s * jax.nn.sigmoid(s)
    bias = b_ref[pl.ds(pl.program_id(0), 1), :]
    o_ref[...] = _dot(s.astype(BF16), w_ref[...].astype(BF16)) + bias


def _modulation(c_ctx, c, w_mod, b_mod):
    nc = 1536
    return pl.pallas_call(
        _mod_kernel,
        grid=(DEPTH, 6 * D // nc),
        in_specs=[
            pl.BlockSpec((1, D), lambda l, j: (0, 0)),
            pl.BlockSpec(c.shape, lambda l, j: (0, 0)),
            pl.BlockSpec((None, D, nc), lambda l, j: (l, 0, j)),
            pl.BlockSpec((DEPTH, nc), lambda l, j: (0, j)),
        ],
        out_specs=pl.BlockSpec((None, SUBLANES, nc), lambda l, j: (l, 0, j)),
        out_shape=jax.ShapeDtypeStruct((DEPTH, SUBLANES, 6 * D), F32),
        scratch_shapes=[pltpu.VMEM((SUBLANES, D), F32)],
        compiler_params=pltpu.CompilerParams(
            dimension_semantics=("arbitrary", "arbitrary"), vmem_limit_bytes=VMEM_LIMIT),
        name="modulation",
    )(c_ctx.reshape(1, D), c, w_mod, b_mod)


def _mod_rows(mod_ref, row0, row_step):
    row = row0 + pl.program_id(0) * row_step
    m = mod_ref[pl.ds(row, 1), :]
    return m[:, 0:D], m[:, D:2 * D], m[:, 2 * D:3 * D]


def _const_spec(shape):
    nd = len(shape)
    return pl.BlockSpec(shape, lambda *_: (0,) * nd, pipeline_mode=pl.Buffered(1))


def _layer_spec(shape, l):
    nd = len(shape) - 1
    return pl.BlockSpec((None,) + tuple(shape[1:]), lambda *_: (l,) + (0,) * nd, pipeline_mode=pl.Buffered(1))


def _mod_spec(l, half):
    return pl.BlockSpec((None, SUBLANES, 3 * D), lambda *_: (l, 0, half))


def _interleave(primary, secondary):
    n_p, n_s = len(primary), len(secondary)
    done = 0
    for k, thunk in enumerate(primary):
        thunk()
        want = ((k + 1) * n_s) // n_p
        while done < want:
            secondary[done]()
            done += 1
    while done < n_s:
        secondary[done]()
        done += 1


def _pool_block(p_ref, gi, r0, half, seq_len, n_rows):
    cols = slice(gi * B_GROUP, (gi + 1) * B_GROUP)
    zero = jnp.zeros((SUBLANES, B_GROUP), F32)
    top = zero if r0 == 0 else p_ref[r0 - SUBLANES:r0, cols]
    bot = zero if r0 + EVEN_RB == n_rows else p_ref[r0 + EVEN_RB:r0 + EVEN_RB + SUBLANES, cols]
    pg = jnp.concatenate([top, p_ref[r0:r0 + EVEN_RB, cols], bot], axis=0)
    pos = (lax.broadcasted_iota(jnp.int32, pg.shape, 0) + (r0 - SUBLANES)) & (seq_len - 1)
    fwd = pg
    bwd = _shift_rows(pg, -1, pos, seq_len)
    k = 1
    while k < half:
        fwd = fwd + _shift_rows(fwd, k, pos, seq_len)
        bwd = bwd + _shift_rows(bwd, -k, pos, seq_len)
        k *= 2
    cnt = jnp.minimum(pos + half, seq_len) - jnp.maximum(pos - half, 0)
    pooled = (fwd + bwd) / cnt.astype(F32) - pg
    return pooled[SUBLANES:SUBLANES + EVEN_RB]


def _even_kernel(x_ref, mod_ref, win_f32_ref, ws_ref, bst_ref, wpool_ref, pscale_ref, wout_f32_ref,
                 lng_ref, lnb_ref, o_ref, u_ref, vn_ref, p_ref, ycat_ref, win_ref, wout_ref,
                 *, layer, seq_len, row0, row_step):
    T = x_ref.shape[0]
    n_blocks = T // ROW_BLOCK
    shift, scale, gate = _mod_rows(mod_ref, row0, row_step)

    @pl.when(pl.program_id(0) == 0)
    def _():
        def cast(rb, carry):
            rows = pl.ds(pl.multiple_of(rb * ROW_BLOCK, ROW_BLOCK), ROW_BLOCK)
            win_ref[rows, :] = win_f32_ref[rows, :].astype(BF16)
            wout_ref[rows, :] = wout_f32_ref[rows, :].astype(BF16)
            return carry

        lax.fori_loop(0, D // ROW_BLOCK, cast, 0)

    def stage_a(rb):
        rows = slice(rb * ROW_BLOCK, (rb + 1) * ROW_BLOCK)
        st = {}

        def t_p():
            st["hb"] = (x_ref[rows, :] * (1.0 + scale) + shift).astype(BF16)
            p_ref[rows, :] = _dot(st["hb"], win_ref[:, 2 * A_WIDTH:])

        def t_u():
            u_ref[rows, :] = _gelu(_dot(st["hb"], win_ref[:, 0:A_WIDTH]))

        def t_v():
            v = _gelu(_dot(st["hb"], win_ref[:, A_WIDTH:2 * A_WIDTH]))
            mu = jnp.mean(v, axis=-1, keepdims=True)
            c = v - mu
            var = jnp.mean(c * c, axis=-1, keepdims=True)
            vn_ref[rows, :] = (c * lax.rsqrt(var + LN_EPS)).astype(BF16)

        return [t_p, t_u, t_v]

    def stage_b(rb):
        def t_gate(n):
            def run():
                rows = slice(n * CHUNK, (n + 1) * CHUNK)
                for h in range(A_HEADS):
                    cols = slice(h * A_GROUP_W, (h + 1) * A_GROUP_W)
                    s = _dot(ws_ref[h].astype(BF16), vn_ref[rows, cols]) + bst_ref[:, h:h + 1]
                    ycat_ref[rows, cols] = (u_ref[rows, cols] * s).astype(BF16)
            return run

        def t_pool(gi):
            def run():
                cols = slice(gi * B_GROUP, (gi + 1) * B_GROUP)
                pooled = jnp.concatenate(
                    [_pool_block(p_ref, gi, rb * ROW_BLOCK + rs * EVEN_RB, POOL_WINDOWS[gi] // 2, seq_len, T)
                     for rs in range(ROW_BLOCK // EVEN_RB)], axis=0)
                yb = _dot(pooled.astype(BF16), wpool_ref[gi].astype(BF16)) * pscale_ref[:, cols]
                ycat_ref[rb * ROW_BLOCK:(rb + 1) * ROW_BLOCK, A_WIDTH + gi * B_GROUP:A_WIDTH + (gi + 1) * B_GROUP] = (
                    yb.astype(BF16))
            return run

        chunks = range(rb * ROW_BLOCK // CHUNK, (rb + 1) * ROW_BLOCK // CHUNK)
        return [t_gate(n) for n in chunks] + [t_pool(gi) for gi in range(len(POOL_WINDOWS))]

    def out_block(rb):
        rows = slice(rb * ROW_BLOCK, (rb + 1) * ROW_BLOCK)
        d = _dot(ycat_ref[rows, :], wout_ref[...])
        r = ALPHA * x_ref[rows, :] + gate * d
        o_ref[rows, :] = _layernorm(r, lng_ref[layer:layer + 1, :], lnb_ref[layer:layer + 1, :])

    for t in stage_a(0):
        t()
    for rb in range(n_blocks):
        nxt = stage_a(rb + 1) if rb + 1 < n_blocks else []
        if nxt:
            nxt[0]()
        _interleave(stage_b(rb), nxt[1:])
        out_block(rb)


def _even_layer(x, mod, w, l, seq_len, row0, row_step, name):
    n_rows = x.shape[0]
    e = l // 2
    kern = functools.partial(_even_kernel, layer=l, seq_len=seq_len, row0=row0, row_step=row_step)
    return pl.pallas_call(
        kern,
        grid=(n_rows // TILE_ROWS,),
        in_specs=[
            pl.BlockSpec((TILE_ROWS, D), lambda i: (i, 0)),
            _mod_spec(l, 0),
            _layer_spec(w["w_in"].shape, e),
            _layer_spec(w["w_s"].shape, e),
            _const_spec(w["b_s_t"].shape),
            _layer_spec(w["w_pool"].shape, e),
            _const_spec(w["pool_scale"].shape),
            _layer_spec(w["w_out"].shape, e),
            _const_spec(w["ln_g"].shape),
            _const_spec(w["ln_b"].shape),
        ],
        out_specs=pl.BlockSpec((TILE_ROWS, D), lambda i: (i, 0)),
        out_shape=jax.ShapeDtypeStruct((n_rows, D), F32),
        scratch_shapes=[
            pltpu.VMEM((TILE_ROWS, A_WIDTH), F32),
            pltpu.VMEM((TILE_ROWS, A_WIDTH), BF16),
            pltpu.VMEM((TILE_ROWS, B_WIDTH), F32),
            pltpu.VMEM((TILE_ROWS, D), BF16),
            pltpu.VMEM((D, 2 * A_WIDTH + B_WIDTH), BF16),
            pltpu.VMEM((D, D), BF16),
        ],
        compiler_params=pltpu.CompilerParams(
            dimension_semantics=("arbitrary",), vmem_limit_bytes=VMEM_LIMIT),
        name=name,
    )(x, mod, w["w_in"], w["w_s"], w["b_s_t"], w["w_pool"], w["pool_scale"], w["w_out"],
      w["ln_g"], w["ln_b"])


def _conv3_block(z_ref, r0, lanes, cw, cb, seq_len):
    nv = FFN_RB // SUBLANES
    z3 = z_ref[r0:r0 + FFN_RB, lanes].reshape(nv, SUBLANES, LANES)
    sub = lax.broadcasted_iota(jnp.int32, (nv, SUBLANES, LANES), 1)
    down = pltpu.roll(z3, 1, 1)
    up = pltpu.roll(z3, SUBLANES - 1, 1)
    if r0 % seq_len == 0:
        prev = jnp.zeros((1, SUBLANES, LANES), F32)
    else:
        prev = pltpu.roll(z_ref[r0 - SUBLANES:r0, lanes], 1, 0)[None]
    if (r0 + FFN_RB) % seq_len == 0:
        nxt = jnp.zeros((1, SUBLANES, LANES), F32)
    else:
        nxt = pltpu.roll(z_ref[r0 + FFN_RB:r0 + FFN_RB + SUBLANES, lanes], SUBLANES - 1, 0)[None]
    zm = jnp.where(sub == 0, jnp.concatenate([prev, down[:-1]], axis=0), down)
    zp = jnp.where(sub == SUBLANES - 1, jnp.concatenate([up[1:], nxt], axis=0), up)
    y = cw[0][:, lanes] * zm + cw[1][:, lanes] * z3 + cw[2][:, lanes] * zp + cb[:, lanes]
    return y.reshape(FFN_RB, LANES)


def _ffn_kernel(x_ref, mod_ref, wup_ref, cw_ref, cb_ref, wd_ref, lng_ref, lnb_ref, o_ref,
                hb_ref, zg0_ref, zv0_ref, zg1_ref, zv1_ref, a_ref, *, layer, seq_len, row0, row_step):
    T = x_ref.shape[0]
    n_chunks = wd_ref.shape[0]
    shift, scale, gate = _mod_rows(mod_ref, row0, row_step)
    z_bufs = ((zg0_ref, zv0_ref), (zg1_ref, zv1_ref))

    def row_block(rb):
        return pl.ds(pl.multiple_of(rb * ROW_BLOCK, ROW_BLOCK), ROW_BLOCK)

    def modulate(rb, carry):
        rows = row_block(rb)
        hb_ref[rows, :] = (x_ref[rows, :] * (1.0 + scale) + shift).astype(BF16)
        return carry

    lax.fori_loop(0, T // ROW_BLOCK, modulate, 0)

    n_parts = T // FFN_DOT_ROWS
    blocks_per_part = FFN_DOT_ROWS // FFN_RB

    def up_proj(c, bufs, q):
        rows = slice(q * FFN_DOT_ROWS, (q + 1) * FFN_DOT_ROWS)
        bufs[0][rows, :] = _dot(hb_ref[rows, :], wup_ref[c])
        bufs[1][rows, :] = _dot(hb_ref[rows, :], wup_ref[c + n_chunks])

    def gate_blocks(c, bufs, q):
        cwg = [cw_ref[k, pl.ds(c, 1), :] for k in range(3)]
        cwv = [cw_ref[k, pl.ds(c + n_chunks, 1), :] for k in range(3)]
        cbg, cbv = cb_ref[pl.ds(c, 1), :], cb_ref[pl.ds(c + n_chunks, 1), :]
        for rb in range(q * blocks_per_part, (q + 1) * blocks_per_part):
            r0 = rb * FFN_RB
            for lt in range(FF_CHUNK // LANES):
                lanes = slice(lt * LANES, (lt + 1) * LANES)
                g = _conv3_block(bufs[0], r0, lanes, cwg, cbg, seq_len)
                v = _conv3_block(bufs[1], r0, lanes, cwv, cbv, seq_len)
                a_ref[c, r0:r0 + FFN_RB, lanes] = (g * jax.nn.sigmoid(g) * v).astype(BF16)

    for q in range(n_parts):
        up_proj(0, z_bufs[0], q)

    def chunk_pair(k, carry):
        c = 2 * k
        for q in range(n_parts):
            up_proj(c + 1, z_bufs[1], q)
            gate_blocks(c, z_bufs[0], q)
        for q in range(n_parts):
            up_proj(c + 2, z_bufs[0], q)
            gate_blocks(c + 1, z_bufs[1], q)
        return carry

    assert n_chunks % 2 == 1
    lax.fori_loop(0, n_chunks // 2, chunk_pair, 0)
    for q in range(n_parts):
        gate_blocks(n_chunks - 1, z_bufs[0], q)

    def out(rb, carry):
        rows = row_block(rb)
        acc = _dot(a_ref[0, rows, :], wd_ref[0])
        for c in range(1, n_chunks):
            acc = acc + _dot(a_ref[c, rows, :], wd_ref[c])
        r = ALPHA * x_ref[rows, :] + gate * acc
        o_ref[rows, :] = _layernorm(r, lng_ref[layer:layer + 1, :], lnb_ref[layer:layer + 1, :])
        return carry

    lax.fori_loop(0, T // ROW_BLOCK, out, 0)


def _ffn_layer(x, mod, w, l, seq_len, row0, row_step, name):
    n_rows = x.shape[0]
    n_chunks = D_FF // FF_CHUNK
    kern = functools.partial(_ffn_kernel, layer=l, seq_len=seq_len, row0=row0, row_step=row_step)
    params = [w["w_up"], w["conv_w"], w["conv_b"], w["w_down"]]
    return pl.pallas_call(
        kern,
        grid=(n_rows // TILE_ROWS,),
        in_specs=[pl.BlockSpec((TILE_ROWS, D), lambda i: (i, 0)), _mod_spec(l, 1)]
        + [_layer_spec(a.shape, l) for a in params]
        + [_const_spec(w["ln_g"].shape), _const_spec(w["ln_b"].shape)],
        out_specs=pl.BlockSpec((TILE_ROWS, D), lambda i: (i, 0)),
        out_shape=jax.ShapeDtypeStruct((n_rows, D), F32),
        scratch_shapes=[
            pltpu.VMEM((TILE_ROWS, D), BF16),
            pltpu.VMEM((TILE_ROWS, FF_CHUNK), F32),
            pltpu.VMEM((TILE_ROWS, FF_CHUNK), F32),
            pltpu.VMEM((TILE_ROWS, FF_CHUNK), F32),
            pltpu.VMEM((TILE_ROWS, FF_CHUNK), F32),
            pltpu.VMEM((n_chunks, TILE_ROWS, FF_CHUNK), BF16),
        ],
        compiler_params=pltpu.CompilerParams(
            dimension_semantics=("arbitrary",), vmem_limit_bytes=VMEM_LIMIT),
        name=name,
    )(x, mod, *params, w["ln_g"], w["ln_b"])


def _prep_up_kernel(w_ref, o_ref):
    for k in range(o_ref.shape[0]):
        o_ref[k] = w_ref[:, k * FF_CHUNK:(k + 1) * FF_CHUNK].astype(BF16)


def _prep_down_kernel(w_ref, o_ref):
    o_ref[...] = w_ref[...].astype(BF16)


def _prep_ffn_weights(w_up, w_down):
    n_ch = 2 * D_FF // FF_CHUNK
    up = pl.pallas_call(
        _prep_up_kernel,
        grid=(DEPTH, 2),
        in_specs=[pl.BlockSpec((None, D, D_FF), lambda l, j: (l, 0, j))],
        out_specs=pl.BlockSpec((None, n_ch // 2, D, FF_CHUNK), lambda l, j: (l, j, 0, 0)),
        out_shape=jax.ShapeDtypeStruct((DEPTH, n_ch, D, FF_CHUNK), BF16),
        compiler_params=pltpu.CompilerParams(
            dimension_semantics=("arbitrary", "arbitrary"), vmem_limit_bytes=VMEM_LIMIT),
        name="ffn_up_bf16",
    )(w_up)
    half = D_FF // 2
    down = pl.pallas_call(
        _prep_down_kernel,
        grid=(DEPTH, 2),
        in_specs=[pl.BlockSpec((None, half, D), lambda l, j: (l, j, 0))],
        out_specs=pl.BlockSpec((None, half, D), lambda l, j: (l, j, 0)),
        out_shape=jax.ShapeDtypeStruct((DEPTH, D_FF, D), BF16),
        compiler_params=pltpu.CompilerParams(
            dimension_semantics=("arbitrary", "arbitrary"), vmem_limit_bytes=VMEM_LIMIT),
        name="ffn_down_bf16",
    )(w_down)
    return up, down.reshape(DEPTH, n_ch // 2, FF_CHUNK, D)


def _conv4_block(xr_ref, cbk, r0, cw, cb, seq_len):
    nv = ODD_RB // SUBLANES
    z3 = xr_ref[cbk, r0:r0 + ODD_RB, :].reshape(nv, SUBLANES, LANES)
    sub = lax.broadcasted_iota(jnp.int32, (nv, SUBLANES, LANES), 1)
    down = pltpu.roll(z3, 1, 1)
    up1 = pltpu.roll(z3, SUBLANES - 1, 1)
    up2 = pltpu.roll(z3, SUBLANES - 2, 1)
    zero = jnp.zeros((1, SUBLANES, LANES), F32)
    if r0 % seq_len == 0:
        prev = zero
    else:
        prev = pltpu.roll(xr_ref[cbk, r0 - SUBLANES:r0, :], 1, 0)[None]
    if (r0 + ODD_RB) % seq_len == 0:
        nxt1 = nxt2 = zero
    else:
        nx = xr_ref[cbk, r0 + ODD_RB:r0 + ODD_RB + SUBLANES, :]
        nxt1 = pltpu.roll(nx, SUBLANES - 1, 0)[None]
        nxt2 = pltpu.roll(nx, SUBLANES - 2, 0)[None]
    zm = jnp.where(sub == 0, jnp.concatenate([prev, down[:-1]], axis=0), down)
    zp1 = jnp.where(sub == SUBLANES - 1, jnp.concatenate([up1[1:], nxt1], axis=0), up1)
    zp2 = jnp.where(sub >= SUBLANES - 2, jnp.concatenate([up2[1:], nxt2], axis=0), up2)
    y = cw[0:1, :] * zm + cw[1:2, :] * z3 + cw[2:3, :] * zp1 + cw[3:4, :] * zp2 + cb
    return y.reshape(ODD_RB, LANES)


def _scan_slabs(a, b, reverse):
    nv = a.shape[0] // SUBLANES
    a = a.reshape(nv, SUBLANES, LANES)
    b = b.reshape(nv, SUBLANES, LANES)
    sub = lax.broadcasted_iota(jnp.int32, (nv, SUBLANES, LANES), 1)
    for d in (1, 2, 4):
        if reverse:
            valid, sh = sub < SUBLANES - d, SUBLANES - d
        else:
            valid, sh = sub >= d, d
        a_prev = jnp.where(valid, pltpu.roll(a, sh, 1), 1.0)
        b_prev = jnp.where(valid, pltpu.roll(b, sh, 1), 0.0)
        b = b + a * b_prev
        a = a * a_prev
    return a.reshape(nv * SUBLANES, LANES), b.reshape(nv * SUBLANES, LANES)


def _scan_ends(a, b, posg, n_groups, reverse):
    n = a.shape[0]
    d = 1
    while d < n_groups:
        if reverse:
            valid, sh = posg < n_groups - d, n - d
        else:
            valid, sh = posg >= d, d
        a_prev = jnp.where(valid, pltpu.roll(a, sh, 0), 1.0)
        b_prev = jnp.where(valid, pltpu.roll(b, sh, 0), 0.0)
        b = b + a * b_prev
        a = a * a_prev
        d *= 2
    return a, b


def _rope_pairs(x, cos, sin_signed, even_lane):
    n = x.shape[1]
    swapped = jnp.where(even_lane, pltpu.roll(x, n - 1, 1), pltpu.roll(x, 1, 1))
    return x * cos + swapped * sin_signed


def _rmsnorm(x, g):
    return x * lax.rsqrt(jnp.mean(x * x, axis=-1, keepdims=True) + RMS_EPS) * g


def _odd_kernel(*refs, layer, seq_len, latent, past_len, row0, row_step):
    n_in = 21 if latent else 16
    n_out = 1 if latent else 4
    (x_ref, mod_ref, win_ref, cw_ref, cb_ref, wgate_ref, bgate_ref, lam_ref, qg_ref, kvg_ref,
     wuq_ref, wuk_ref, wuv_ref, wout_ref, lng_ref, lnb_ref) = refs[:16]
    if latent:
        cckv_ref, ckr_ref, st_ref, cos_ref, sin_ref = refs[16:21]
        (o_ref,) = refs[n_in:n_in + n_out]
    else:
        o_ref, ckv_out_ref, kr_out_ref, st_out_ref = refs[n_in:n_in + n_out]
    (xr_ref, gr_ref, xc_ref, pre_ref, a_ref, b_ref, hrep_ref, q_ref, k_ref, v_ref, yd_ref) = refs[n_in + n_out:]

    T = x_ref.shape[0]
    n_blocks = T // ROW_BLOCK
    n_tiles = W_C // LANES
    half_w = W_C // 2
    shift, scale, gate = _mod_rows(mod_ref, row0, row_step)
    o_q = 2 * W_C
    o_kv = o_q + Q_LORA
    o_kr = o_kv + KV_LORA
    pair_w = 2 * HEAD_PAD
    even_lane = (lax.broadcasted_iota(jnp.int32, (ROW_BLOCK, HEAD_PAD), 1) & 1) == 0
    softplus = []
    for direction in range(2):
        neg = -lam_ref[direction:direction + 1, :]
        softplus.append(jnp.maximum(neg, 0.0) + jnp.log1p(jnp.exp(-jnp.abs(neg))))

    def stage_a(rb):
        rows = slice(rb * ROW_BLOCK, (rb + 1) * ROW_BLOCK)
        k_rows = slice(past_len + rb * ROW_BLOCK, past_len + (rb + 1) * ROW_BLOCK)
        st = {}

        def t_xr():
            st["hb"] = (x_ref[rows, :] * (1.0 + scale) + shift).astype(BF16)
            z = _dot(st["hb"], win_ref[:, 0:W_C])
            for cbk in range(n_tiles):
                xr_ref[cbk, rows, :] = z[:, cbk * LANES:(cbk + 1) * LANES]

        def t_gr():
            gr_ref[rows, :] = _dot(st["hb"], win_ref[:, W_C:o_q])

        def t_lat():
            hb = st["hb"]
            st["qn"] = _rmsnorm(_dot(hb, win_ref[:, o_q:o_kv]), qg_ref[...]).astype(BF16)
            ckv = _rmsnorm(_dot(hb, win_ref[:, o_kv:o_kr]), kvg_ref[...])
            kr = _dot(hb, win_ref[:, o_kr:o_kr + HEAD_PAD])
            if latent:
                st["cos"] = cos_ref[rows, :]
                st["sin"] = sin_ref[rows, :]
                kr = _rope_pairs(kr, st["cos"], st["sin"], even_lane)
            else:
                ckv_out_ref[rows, :] = ckv
                kr_out_ref[rows, :] = kr[:, KR_LANE:KR_LANE + QK_ROPE]
            st["kr"] = kr
            st["ckvb"] = ckv.astype(BF16)

        def t_v():
            v_ref[k_rows, :] = _dot(st["ckvb"], wuv_ref[...]).astype(BF16)

        def t_pair(pair):
            def run():
                pcols = slice(pair * pair_w, (pair + 1) * pair_w)
                q2 = _dot(st["qn"], wuq_ref[:, pcols])
                k2 = _dot(st["ckvb"], wuk_ref[:, pcols])
                for hh in range(2):
                    cols = slice(pair * pair_w + hh * HEAD_PAD, pair * pair_w + (hh + 1) * HEAD_PAD)
                    qh = q2[:, hh * HEAD_PAD:(hh + 1) * HEAD_PAD]
                    if latent:
                        qh = _rope_pairs(qh, st["cos"], st["sin"], even_lane)
                    q_ref[rows, cols] = qh.astype(BF16)
                    k_ref[k_rows, cols] = (k2[:, hh * HEAD_PAD:(hh + 1) * HEAD_PAD] + st["kr"]).astype(BF16)
            return run

        return [t_xr, t_gr, t_lat, t_v] + [t_pair(p) for p in range(D_HEADS // 2)]

    def conv_thunks(rb):
        par = rb % 2

        def t_conv(cbk):
            def run():
                cw = cw_ref[:, cbk * LANES:(cbk + 1) * LANES]
                cb = cb_ref[:, cbk * LANES:(cbk + 1) * LANES]
                for rs in range(ROW_BLOCK // ODD_RB):
                    r0 = rb * ROW_BLOCK + rs * ODD_RB
                    xc_ref[par, rs * ODD_RB:(rs + 1) * ODD_RB, cbk * LANES:(cbk + 1) * LANES] = _conv4_block(
                        xr_ref, cbk, r0, cw, cb, seq_len)
            return run

        return [t_conv(cbk) for cbk in range(n_tiles)]

    def gate_thunks(rb):
        par = rb % 2
        thunks = []
        for direction in range(2):
            for j in range(2):
                slot = (2 * direction + j) % 2

                def t_dot(direction=direction, j=j, slot=slot):
                    xcb = xc_ref[par, :, j * half_w:(j + 1) * half_w].astype(BF16)
                    pre_ref[slot, 0] = _dot(xcb, wgate_ref[direction, 0, j])
                    pre_ref[slot, 1] = _dot(xcb, wgate_ref[direction, 1, j])

                def t_post(direction=direction, j=j, slot=slot):
                    for rs in range(ROW_BLOCK // ODD_RB):
                        lrows = slice(rs * ODD_RB, (rs + 1) * ODD_RB)
                        grows = slice(rb * ROW_BLOCK + rs * ODD_RB, rb * ROW_BLOCK + (rs + 1) * ODD_RB)
                        for lt in range(half_w // LANES):
                            cbk = j * (half_w // LANES) + lt
                            lanes = slice(lt * LANES, (lt + 1) * LANES)
                            glanes = slice(cbk * LANES, (cbk + 1) * LANES)
                            xcv = xc_ref[par, lrows, glanes]
                            r = jax.nn.sigmoid(pre_ref[slot, 0, lrows, lanes] + bgate_ref[direction, 0:1, glanes])
                            i = jax.nn.sigmoid(pre_ref[slot, 1, lrows, lanes] + bgate_ref[direction, 1:2, glanes])
                            log_a = -RG_C * r * softplus[direction][:, glanes]
                            a = jnp.exp(log_a)
                            b = jnp.sqrt(jnp.tanh(-log_a) * (a * a + 1.0)) * i * xcv
                            a, b = _scan_slabs(a, b, direction == 1)
                            a_ref[direction, cbk, grows, :] = a
                            b_ref[direction, cbk, grows, :] = b

                thunks += [t_dot, t_post]
        return thunks

    def combine(direction, cbk, t0, n_rows, h0):
        reverse = direction == 1
        n_ends = n_rows // SUBLANES
        end_row = 0 if reverse else SUBLANES - 1
        posg = lax.broadcasted_iota(jnp.int32, (n_ends, LANES), 0)
        a_end = a_ref[direction, cbk, pl.ds(t0 + end_row, n_ends, stride=SUBLANES), :]
        b_end = b_ref[direction, cbk, pl.ds(t0 + end_row, n_ends, stride=SUBLANES), :]
        a_tot, b_tot = _scan_ends(a_end, b_end, posg, n_ends, reverse)
        ends = b_tot if h0 is None else a_tot * h0 + b_tot
        first = jnp.zeros((1, LANES), F32) if h0 is None else h0
        if reverse:
            carry_in = jnp.where(posg < n_ends - 1, pltpu.roll(ends, n_ends - 1, 0), first)
        else:
            carry_in = jnp.where(posg >= 1, pltpu.roll(ends, 1, 0), first)
        for rr in range(SUBLANES):
            hrep_ref[pl.ds(t0 + rr, n_ends, stride=SUBLANES), :] = carry_in
        for rs in range(n_rows // ODD_RB):
            rows = slice(t0 + rs * ODD_RB, t0 + (rs + 1) * ODD_RB)
            h = a_ref[direction, cbk, rows, :] * hrep_ref[rows, :] + b_ref[direction, cbk, rows, :]
            if direction == 0:
                xr_ref[cbk, rows, :] = h
            else:
                xr_ref[cbk, rows, :] = xr_ref[cbk, rows, :] + h
        return ends

    def attend_pair(q_rows, k_rows, pair):
        o_pair = None
        for h in (2 * pair, 2 * pair + 1):
            cols = slice(h * HEAD_PAD, (h + 1) * HEAD_PAD)
            s = lax.dot_general(q_ref[q_rows, cols], k_ref[k_rows, cols],
                                (((1,), (1,)), ((), ())), preferred_element_type=F32) * ATTN_SCALE
            m = jnp.max(s, axis=-1, keepdims=True)
            p = jnp.exp(s - m)
            l = jnp.sum(p, axis=-1, keepdims=True)
            o = _dot(p.astype(BF16), v_ref[k_rows, cols]) / l
            o_pair = o if o_pair is None else o_pair + o
        yd_ref[q_rows, pair * HEAD_PAD:(pair + 1) * HEAD_PAD] = o_pair.astype(BF16)

    def out_block(rows):
        yc = jnp.concatenate(
            [xr_ref[cbk, rows, :] * _gelu(gr_ref[rows, cbk * LANES:(cbk + 1) * LANES]) for cbk in range(n_tiles)],
            axis=1).astype(BF16)
        d = _dot(yc, wout_ref[0:W_C, :]) + _dot(yd_ref[rows, :], wout_ref[W_C:, :])
        r = ALPHA * x_ref[rows, :] + gate * d
        o_ref[rows, :] = _layernorm(r, lng_ref[layer:layer + 1, :], lnb_ref[layer:layer + 1, :])

    if not latent:
        def stage_b(s):
            t0 = s * seq_len
            rows = slice(t0, t0 + seq_len)
            thunks = conv_thunks(s) + gate_thunks(s)
            for direction in range(2):
                for cbk in range(n_tiles):
                    def t_comb(direction=direction, cbk=cbk):
                        ends = combine(direction, cbk, t0, seq_len, None)
                        last = 0 if direction == 1 else seq_len // SUBLANES - 1
                        st_out_ref[s, direction:direction + 1, cbk * LANES:(cbk + 1) * LANES] = ends[last:last + 1, :]
                    thunks.append(t_comb)
            return thunks

        assert seq_len == ROW_BLOCK
        for t in stage_a(0):
            t()
        for s in range(n_blocks):
            rows = slice(s * seq_len, (s + 1) * seq_len)
            attn = [functools.partial(attend_pair, rows, rows, pair) for pair in range(D_HEADS // 2)]
            nxt = stage_a(s + 1) if s + 1 < n_blocks else []
            mxu = []
            _interleave([functools.partial(mxu.append, t) for t in nxt] or [lambda: None],
                        [functools.partial(mxu.append, t) for t in attn])
            _interleave(stage_b(s), mxu)
            out_block(rows)
        return

    assert seq_len == T
    for t in stage_a(0):
        t()
    for rb in range(n_blocks):
        nxt = stage_a(rb + 1) if rb + 1 < n_blocks else []
        if nxt:
            nxt[0]()
        _interleave(conv_thunks(rb) + gate_thunks(rb), nxt[1:])

    def cached(cbk, carry):
        rows = pl.ds(pl.multiple_of(cbk * ROW_BLOCK, ROW_BLOCK), ROW_BLOCK)
        cckv = cckv_ref[rows, :].astype(BF16)
        ckr = ckr_ref[rows, :]
        v_ref[rows, :] = _dot(cckv, wuv_ref[...]).astype(BF16)
        for pair in range(D_HEADS // 2):
            pcols = slice(pair * pair_w, (pair + 1) * pair_w)
            k2 = _dot(cckv, wuk_ref[:, pcols])
            for hh in range(2):
                cols = slice(pair * pair_w + hh * HEAD_PAD, pair * pair_w + (hh + 1) * HEAD_PAD)
                k_ref[rows, cols] = (k2[:, hh * HEAD_PAD:(hh + 1) * HEAD_PAD] + ckr).astype(BF16)
        return carry

    lax.fori_loop(0, past_len // ROW_BLOCK, cached, 0)

    n_keys = past_len + seq_len

    def tail(qb, carry):
        q_rows = pl.ds(pl.multiple_of(qb * ROW_BLOCK, ROW_BLOCK), ROW_BLOCK)
        k_rows = slice(0, n_keys)
        rg = []
        for direction in range(2):
            def t_comb(direction=direction):
                h0 = st_ref[direction, pl.ds(qb, 1), :]
                combine(direction, qb, 0, T, h0)
            rg.append(t_comb)
        _interleave([functools.partial(attend_pair, q_rows, k_rows, pair) for pair in range(D_HEADS // 2)], rg)
        return carry

    assert n_tiles == n_blocks
    lax.fori_loop(0, n_blocks, tail, 0)

    def out(rb, carry):
        out_block(pl.ds(pl.multiple_of(rb * ROW_BLOCK, ROW_BLOCK), ROW_BLOCK))
        return carry

    lax.fori_loop(0, n_blocks, out, 0)


def _odd_layer(x, mod, w, l, seq_len, row0, row_step, name, latent_inputs=None):
    n_rows = x.shape[0]
    n_tiles = n_rows // TILE_ROWS
    latent = latent_inputs is not None
    past_len = latent_inputs["cache_ckv"].shape[1] if latent else 0
    n_seq = TILE_ROWS // seq_len
    kern = functools.partial(_odd_kernel, layer=l, seq_len=seq_len, latent=latent, past_len=past_len,
                             row0=row0, row_step=row_step)
    weights = [w["w_in"], w["conv_w"], w["conv_b"], w["w_gate"], w["b_gate"], w["lam"], w["q_g"], w["kv_g"],
               w["w_uq"], w["w_uk"], w["w_uv"], w["w_out"]]
    tile_mode = pl.Buffered(1) if latent else pl.Buffered(2)
    in_specs = [pl.BlockSpec((TILE_ROWS, D), lambda i: (i, 0), pipeline_mode=tile_mode),
                _mod_spec(l, 0)]
    in_specs += [_const_spec(a.shape) for a in weights]
    in_specs += [_const_spec(w["ln_g"].shape), _const_spec(w["ln_b"].shape)]
    args = [x, mod] + weights + [w["ln_g"], w["ln_b"]]
    out_specs = [pl.BlockSpec((TILE_ROWS, D), lambda i: (i, 0), pipeline_mode=tile_mode)]
    out_shape = [jax.ShapeDtypeStruct((n_rows, D), F32)]
    if latent:
        li = latent_inputs
        in_specs += [
            pl.BlockSpec((None, past_len, KV_LORA), lambda i: (i, 0, 0)),
            pl.BlockSpec((None, past_len, HEAD_PAD), lambda i: (i, 0, 0)),
            pl.BlockSpec((None, 2, W_C // LANES, LANES), lambda i: (i, 0, 0, 0)),
            _const_spec(li["cos"].shape),
            _const_spec(li["sin"].shape),
        ]
        args += [li["cache_ckv"], li["cache_kr"], li["state"], li["cos"], li["sin"]]
    else:
        out_specs += [
            pl.BlockSpec((TILE_ROWS, KV_LORA), lambda i: (i, 0)),
            pl.BlockSpec((TILE_ROWS, QK_ROPE), lambda i: (i, 0)),
            pl.BlockSpec((n_seq, 2, W_C), lambda i: (i, 0, 0)),
        ]
        out_shape += [
            jax.ShapeDtypeStruct((n_rows, KV_LORA), F32),
            jax.ShapeDtypeStruct((n_rows, QK_ROPE), F32),
            jax.ShapeDtypeStruct((n_rows // seq_len, 2, W_C), F32),
        ]
    n_keys_buf = past_len + TILE_ROWS
    scratch = [
        pltpu.VMEM((W_C // LANES, TILE_ROWS, LANES), F32),
        pltpu.VMEM((TILE_ROWS, W_C), F32),
        pltpu.VMEM((2, ROW_BLOCK, W_C), F32),
        pltpu.VMEM((2, 2, ROW_BLOCK, W_C // 2), F32),
        pltpu.VMEM((2, W_C // LANES, TILE_ROWS, LANES), F32),
        pltpu.VMEM((2, W_C // LANES, TILE_ROWS, LANES), F32),
        pltpu.VMEM((TILE_ROWS, LANES), F32),
        pltpu.VMEM((TILE_ROWS, D_HEADS * HEAD_PAD), BF16),
        pltpu.VMEM((n_keys_buf, D_HEADS * HEAD_PAD), BF16),
        pltpu.VMEM((n_keys_buf, D_HEADS * HEAD_PAD), BF16),
        pltpu.VMEM((TILE_ROWS, W_C), BF16),
    ]
    return pl.pallas_call(
        kern,
        grid=(n_tiles,),
        in_specs=in_specs,
        out_specs=out_specs,
        out_shape=out_shape,
        scratch_shapes=scratch,
        compiler_params=pltpu.CompilerParams(
            dimension_semantics=("arbitrary",), vmem_limit_bytes=VMEM_LIMIT),
        name=name,
    )(*args)


def _axial_rope_tables(rows):
    half = QK_ROPE // 2
    inv = (ROPE_BASE ** (-np.arange(0, half, 2, dtype=np.float32) / half)).astype(np.float32)
    r = np.repeat(np.arange(rows, dtype=np.float32), GRID_W)
    col = np.tile(np.arange(GRID_W, dtype=np.float32), rows)
    ang = np.concatenate([r[:, None] * inv, col[:, None] * inv], axis=-1).astype(np.float32)
    cos = np.repeat(np.cos(ang), 2, axis=-1)
    sin = np.repeat(np.sin(ang), 2, axis=-1) * np.tile(np.array([-1.0, 1.0], np.float32), half)
    n = ang.shape[0]
    cos_t = np.ones((n, HEAD_PAD), np.float32)
    sin_t = np.zeros((n, HEAD_PAD), np.float32)
    cos_t[:, KR_LANE:KR_LANE + QK_ROPE] = cos
    sin_t[:, KR_LANE:KR_LANE + QK_ROPE] = sin
    return jnp.asarray(cos_t), jnp.asarray(sin_t)


def _odd_prep_kernel(win_ref, wa_ref, wx_ref, wuq_ref, wuk_ref, wuv_ref, wout_ref,
                     win_o, gate_o, wuq_o, wuk_o, wuv_o, wout_o):
    n_main = 2 * W_C + Q_LORA + KV_LORA
    win_o[:, 0:n_main] = win_ref[:, 0:n_main].astype(BF16)
    win_o[:, n_main:] = jnp.zeros((D, HEAD_PAD), BF16)
    win_o[:, n_main + KR_LANE:n_main + KR_LANE + QK_ROPE] = win_ref[:, n_main:].astype(BF16)
    wout_o[...] = wout_ref[...].astype(BF16)
    wuq_o[...] = jnp.zeros_like(wuq_o)
    wuk_o[...] = jnp.zeros_like(wuk_o)
    wuv_o[...] = jnp.zeros_like(wuv_o)
    for h in range(D_HEADS):
        wuq_o[:, h * HEAD_PAD:h * HEAD_PAD + QK_NOPE + QK_ROPE] = wuq_ref[:, h, :].astype(BF16)
        wuk_o[:, h * HEAD_PAD:h * HEAD_PAD + QK_NOPE] = wuk_ref[:, h, :].astype(BF16)
        v0 = h * HEAD_PAD + (h % 2) * V_DIM
        wuv_o[:, v0:v0 + V_DIM] = wuv_ref[:, h, :].astype(BF16)
    gate_o[...] = jnp.zeros_like(gate_o)
    per = (W_C // 2) // C_BLOCK
    for direction in range(2):
        for g, w_ref in enumerate((wa_ref, wx_ref)):
            for h in range(C_HEADS):
                j, hh = divmod(h, per)
                blk = slice(hh * C_BLOCK, (hh + 1) * C_BLOCK)
                gate_o[direction, g, j, blk, blk] = w_ref[direction, h].astype(BF16)


def _prep_odd_weights(w_in, w_a, w_x, w_uq, w_uk, w_uv, w_out):
    ins = [w_in, w_a, w_x, w_uq, w_uk, w_uv, w_out]
    outs = [
        jax.ShapeDtypeStruct((D, ODD_IN_PAD), BF16),
        jax.ShapeDtypeStruct((2, 2, 2, W_C // 2, W_C // 2), BF16),
        jax.ShapeDtypeStruct((Q_LORA, D_HEADS * HEAD_PAD), BF16),
        jax.ShapeDtypeStruct((KV_LORA, D_HEADS * HEAD_PAD), BF16),
        jax.ShapeDtypeStruct((KV_LORA, D_HEADS * HEAD_PAD), BF16),
        jax.ShapeDtypeStruct((D, D), BF16),
    ]
    return pl.pallas_call(
        _odd_prep_kernel,
        in_specs=[pl.BlockSpec(a.shape, lambda n=a.ndim: (0,) * n) for a in ins],
        out_specs=[pl.BlockSpec(o.shape, lambda n=len(o.shape): (0,) * n) for o in outs],
        out_shape=outs,
        compiler_params=pltpu.CompilerParams(vmem_limit_bytes=VMEM_LIMIT),
        name="odd_weights_bf16",
    )(*ins)


def kernel(x_prompt, x_sample, cache_mla_ckv, cache_mla_krope, state_rglru, c, c_ctx, w_mod, b_mod, ln1_g, ln1_b, ln2_g, ln2_b, even_w_in, even_w_s, even_b_s, even_w_pool, even_pool_scale, even_w_out, odd_w_in, rg_conv_w, rg_conv_b, rg_w_a, rg_b_a, rg_w_x, rg_b_x, rg_lam, mla_q_g, mla_kv_g, mla_w_uq, mla_w_uk, mla_w_uv, odd_w_out, ffn_w_up, ffn_conv_w, ffn_conv_b, ffn_w_down):
    batch, seq, _ = x_prompt.shape
    dec_batch, dec_seq, _ = x_sample.shape
    assert TILE_ROWS % seq == 0 and dec_seq == TILE_ROWS and dec_batch + 1 <= SUBLANES

    mod = _modulation(c_ctx, c, w_mod, b_mod)

    wup_bf, wdown_bf = _prep_ffn_weights(ffn_w_up, ffn_w_down)
    n_ch = 2 * D_FF // FF_CHUNK
    wf = {"ln_g": ln2_g, "ln_b": ln2_b,
          "w_up": wup_bf, "conv_w": ffn_conv_w.reshape(DEPTH, 3, n_ch, FF_CHUNK),
          "conv_b": ffn_conv_b.reshape(DEPTH, n_ch, FF_CHUNK), "w_down": wdown_bf}
    ln1 = {"ln_g": ln1_g, "ln_b": ln1_b}

    xp = x_prompt.reshape(batch * seq, D)
    xs = x_sample.reshape(dec_batch * dec_seq, D)
    new_ckv = new_kr = new_state = None
    for l in range(DEPTH):
        if l % 2 == 0:
            e = l // 2
            w = dict(ln1, w_in=even_w_in, w_s=even_w_s, b_s_t=even_b_s[e].T, w_pool=even_w_pool,
                     pool_scale=even_pool_scale[e][None, :], w_out=even_w_out)
            xp = _even_layer(xp, mod, w, l, seq, 0, 0, f"even{l}_ctx")
            xs = _even_layer(xs, mod, w, l, dec_seq, 1, 1, f"even{l}_lat")
        else:
            o = l // 2
            w_in_bf, w_gate_bf, w_uq_bf, w_uk_bf, w_uv_bf, w_out_bf = _prep_odd_weights(
                odd_w_in[o], rg_w_a[o], rg_w_x[o], mla_w_uq[o], mla_w_uk[o], mla_w_uv[o], odd_w_out[o])
            b_gate = jnp.stack([rg_b_a[o], rg_b_x[o]], axis=1)
            w = dict(ln1,
                     w_in=w_in_bf, conv_w=rg_conv_w[o], conv_b=rg_conv_b[o][None, :],
                     w_gate=w_gate_bf, b_gate=b_gate, lam=rg_lam[o],
                     q_g=mla_q_g[o][None, :], kv_g=mla_kv_g[o][None, :],
                     w_uq=w_uq_bf, w_uk=w_uk_bf, w_uv=w_uv_bf, w_out=w_out_bf)
            xp, ckv, kr, st = _odd_layer(xp, mod, w, l, seq, 0, 0, f"odd{l}_ctx")
            new_ckv = ckv.reshape(batch, 1, seq, KV_LORA)
            new_kr = kr.reshape(batch, 1, seq, QK_ROPE)
            new_state = st.reshape(batch, 1, 2, W_C)
            cos_t, sin_t = _axial_rope_tables(dec_seq // GRID_W)
            latent_inputs = {
                "cache_ckv": cache_mla_ckv[:, o],
                "cache_kr": jnp.pad(cache_mla_krope[:, o],
                                    ((0, 0), (0, 0), (KR_LANE, HEAD_PAD - KR_LANE - QK_ROPE))),
                "state": state_rglru[:, o].reshape(dec_batch, 2, W_C // LANES, LANES),
                "cos": cos_t, "sin": sin_t,
            }
            (xs,) = _odd_layer(xs, mod, w, l, dec_seq, 1, 1, f"odd{l}_lat", latent_inputs)
        xp = _ffn_layer(xp, mod, wf, l, seq, 0, 0, f"ffn{l}_ctx")
        xs = _ffn_layer(xs, mod, wf, l, dec_seq, 1, 1, f"ffn{l}_lat")
    return (xp.reshape(batch, seq, D), xs.reshape(dec_batch, dec_seq, D), new_ckv, new_kr, new_state)
```

```python
import functools
import math

import jax
import jax.numpy as jnp
import numpy as np
from jax import lax
from jax.experimental import pallas as pl
from jax.experimental.pallas import tpu as pltpu

F32 = jnp.float32
BF16 = jnp.bfloat16

D = 1024
DEPTH = 2
GRID_W = 64
ALPHA = (2 * DEPTH) ** 0.25
LN_EPS = 1e-5
RMS_EPS = 1e-6
A_HEADS = 4
A_WIDTH = 512
A_GROUP_W = A_WIDTH // A_HEADS
CHUNK = 128
POOL_WINDOWS = (2, 4, 8, 16)
B_GROUP = 128
B_WIDTH = 512
W_C = 512
C_HEADS = 8
C_BLOCK = 64
RG_C = 8.0
D_HEADS = 8
Q_LORA = 384
KV_LORA = 256
QK_NOPE = 64
QK_ROPE = 32
V_DIM = 64
ROPE_BASE = 10000.0
ATTN_SCALE = 1.0 / math.sqrt(QK_NOPE + QK_ROPE)
D_FF = 2816

LANES = 128
SUBLANES = 8
HEAD_PAD = LANES
ODD_IN_PAD = 2 * W_C + Q_LORA + KV_LORA + HEAD_PAD
KR_LANE = QK_NOPE
TILE_ROWS = 1024
ROW_BLOCK = 256
FF_CHUNK = 256
FFN_DOT_ROWS = 128
FFN_RB = 64
EVEN_RB = 64
ODD_RB = 64
VMEM_LIMIT = 61 * 1024 * 1024


def _dot(a, b):
    return jnp.dot(a, b, preferred_element_type=F32)


def _gelu(x):
    return x * (0.5 * (1.0 + jnp.tanh(0.7978845608028654 * (x + 0.044715 * (x * x * x)))))


def _layernorm(r, g, b):
    mu = jnp.mean(r, axis=-1, keepdims=True)
    c = r - mu
    var = jnp.mean(c * c, axis=-1, keepdims=True)
    return c * lax.rsqrt(var + LN_EPS) * g + b


def _shift_rows(x, d, pos, seq_len):
    n = x.shape[0]
    y = pltpu.roll(x, (-d) % n, 0)
    valid = (pos < seq_len - d) if d > 0 else (pos >= -d)
    return jnp.where(valid, y, 0.0)


def _mod_kernel(cctx_ref, c_ref, w_ref, b_ref, o_ref, s_ref):
    n_lat = c_ref.shape[0]
    s_ref[...] = jnp.zeros_like(s_ref)
    s_ref[0:1, :] = cctx_ref[...]
    s_ref[1:1 + n_lat, :] = c_ref[...]
    s = s_ref[...]
    s = s * jax.nn.sigmoid(s)
    bias = b_ref[pl.ds(pl.program_id(0), 1), :]
    o_ref[...] = _dot(s.astype(BF16), w_ref[...].astype(BF16)) + bias


def _modulation(c_ctx, c, w_mod, b_mod):
    nc = 1536
    return pl.pallas_call(
        _mod_kernel,
        grid=(DEPTH, 6 * D // nc),
        in_specs=[
            pl.BlockSpec((1, D), lambda l, j: (0, 0)),
            pl.BlockSpec(c.shape, lambda l, j: (0, 0)),
            pl.BlockSpec((None, D, nc), lambda l, j: (l, 0, j)),
            pl.BlockSpec((DEPTH, nc), lambda l, j: (0, j)),
        ],
        out_specs=pl.BlockSpec((None, SUBLANES, nc), lambda l, j: (l, 0, j)),
        out_shape=jax.ShapeDtypeStruct((DEPTH, SUBLANES, 6 * D), F32),
        scratch_shapes=[pltpu.VMEM((SUBLANES, D), F32)],
        compiler_params=pltpu.CompilerParams(
            dimension_semantics=("arbitrary", "arbitrary"), vmem_limit_bytes=VMEM_LIMIT),
        name="modulation",
    )(c_ctx.reshape(1, D), c, w_mod, b_mod)


def _mod_rows(mod_ref, row0, row_step):
    row = row0 + pl.program_id(0) * row_step
    m = mod_ref[pl.ds(row, 1), :]
    return m[:, 0:D], m[:, D:2 * D], m[:, 2 * D:3 * D]


def _const_spec(shape):
    nd = len(shape)
    return pl.BlockSpec(shape, lambda *_: (0,) * nd, pipeline_mode=pl.Buffered(1))


def _layer_spec(shape, l):
    nd = len(shape) - 1
    return pl.BlockSpec((None,) + tuple(shape[1:]), lambda *_: (l,) + (0,) * nd, pipeline_mode=pl.Buffered(1))


def _mod_spec(l, half):
    return pl.BlockSpec((None, SUBLANES, 3 * D), lambda *_: (l, 0, half))


def _interleave(primary, secondary):
    n_p, n_s = len(primary), len(secondary)
    done = 0
    for k, thunk in enumerate(primary):
        thunk()
        want = ((k + 1) * n_s) // n_p
        while done < want:
            secondary[done]()
            done += 1
    while done < n_s:
        secondary[done]()
        done += 1


def _pool_block(p_ref, gi, r0, half, seq_len, n_rows):
    cols = slice(gi * B_GROUP, (gi + 1) * B_GROUP)
    zero = jnp.zeros((SUBLANES, B_GROUP), F32)
    top = zero if r0 == 0 else p_ref[r0 - SUBLANES:r0, cols]
    bot = zero if r0 + EVEN_RB == n_rows else p_ref[r0 + EVEN_RB:r0 + EVEN_RB + SUBLANES, cols]
    pg = jnp.concatenate([top, p_ref[r0:r0 + EVEN_RB, cols], bot], axis=0)
    pos = (lax.broadcasted_iota(jnp.int32, pg.shape, 0) + (r0 - SUBLANES)) & (seq_len - 1)
    fwd = pg
    bwd = _shift_rows(pg, -1, pos, seq_len)
    k = 1
    while k < half:
        fwd = fwd + _shift_rows(fwd, k, pos, seq_len)
        bwd = bwd + _shift_rows(bwd, -k, pos, seq_len)
        k *= 2
    cnt = jnp.minimum(pos + half, seq_len) - jnp.maximum(pos - half, 0)
    pooled = (fwd + bwd) / cnt.astype(F32) - pg
    return pooled[SUBLANES:SUBLANES + EVEN_RB]


def _even_kernel(x_ref, mod_ref, win_f32_ref, ws_ref, bst_ref, wpool_ref, pscale_ref, wout_f32_ref,
                 lng_ref, lnb_ref, o_ref, u_ref, vn_ref, p_ref, ycat_ref, win_ref, wout_ref,
                 *, layer, seq_len, row0, row_step):
    T = x_ref.shape[0]
    n_blocks = T // ROW_BLOCK
    shift, scale, gate = _mod_rows(mod_ref, row0, row_step)

    @pl.when(pl.program_id(0) == 0)
    def _():
        def cast(rb, carry):
            rows = pl.ds(pl.multiple_of(rb * ROW_BLOCK, ROW_BLOCK), ROW_BLOCK)
            win_ref[rows, :] = win_f32_ref[rows, :].astype(BF16)
            wout_ref[rows, :] = wout_f32_ref[rows, :].astype(BF16)
            return carry

        lax.fori_loop(0, D // ROW_BLOCK, cast, 0)

    def stage_a(rb):
        rows = slice(rb * ROW_BLOCK, (rb + 1) * ROW_BLOCK)
        st = {}

        def t_p():
            st["hb"] = (x_ref[rows, :] * (1.0 + scale) + shift).astype(BF16)
            p_ref[rows, :] = _dot(st["hb"], win_ref[:, 2 * A_WIDTH:])

        def t_u():
            u_ref[rows, :] = _gelu(_dot(st["hb"], win_ref[:, 0:A_WIDTH]))

        def t_v():
            v = _gelu(_dot(st["hb"], win_ref[:, A_WIDTH:2 * A_WIDTH]))
            mu = jnp.mean(v, axis=-1, keepdims=True)
            c = v - mu
            var = jnp.mean(c * c, axis=-1, keepdims=True)
            vn_ref[rows, :] = (c * lax.rsqrt(var + LN_EPS)).astype(BF16)

        return [t_p, t_u, t_v]

    def stage_b(rb):
        def t_gate(n):
            def run():
                rows = slice(n * CHUNK, (n + 1) * CHUNK)
                for h in range(A_HEADS):
                    cols = slice(h * A_GROUP_W, (h + 1) * A_GROUP_W)
                    s = _dot(ws_ref[h].astype(BF16), vn_ref[rows, cols]) + bst_ref[:, h:h + 1]
                    ycat_ref[rows, cols] = (u_ref[rows, cols] * s).astype(BF16)
            return run

        def t_pool(gi):
            def run():
                cols = slice(gi * B_GROUP, (gi + 1) * B_GROUP)
                pooled = jnp.concatenate(
                    [_pool_block(p_ref, gi, rb * ROW_BLOCK + rs * EVEN_RB, POOL_WINDOWS[gi] // 2, seq_len, T)
                     for rs in range(ROW_BLOCK // EVEN_RB)], axis=0)
                yb = _dot(pooled.astype(BF16), wpool_ref[gi].astype(BF16)) * pscale_ref[:, cols]
                ycat_ref[rb * ROW_BLOCK:(rb + 1) * ROW_BLOCK, A_WIDTH + gi * B_GROUP:A_WIDTH + (gi + 1) * B_GROUP] = (
                    yb.astype(BF16))
            return run

        chunks = range(rb * ROW_BLOCK // CHUNK, (rb + 1) * ROW_BLOCK // CHUNK)
        return [t_gate(n) for n in chunks] + [t_pool(gi) for gi in range(len(POOL_WINDOWS))]

    def out_block(rb):
        rows = slice(rb * ROW_BLOCK, (rb + 1) * ROW_BLOCK)
        d = _dot(ycat_ref[rows, :], wout_ref[...])
        r = ALPHA * x_ref[rows, :] + gate * d
        o_ref[rows, :] = _layernorm(r, lng_ref[layer:layer + 1, :], lnb_ref[layer:layer + 1, :])

    for t in stage_a(0):
        t()
    for rb in range(n_blocks):
        nxt = stage_a(rb + 1) if rb + 1 < n_blocks else []
        if nxt:
            nxt[0]()
        _interleave(stage_b(rb), nxt[1:])
        out_block(rb)


def _even_layer(x, mod, w, l, seq_len, row0, row_step, name):
    n_rows = x.shape[0]
    e = l // 2
    kern = functools.partial(_even_kernel, layer=l, seq_len=seq_len, row0=row0, row_step=row_step)
    return pl.pallas_call(
        kern,
        grid=(n_rows // TILE_ROWS,),
        in_specs=[
            pl.BlockSpec((TILE_ROWS, D), lambda i: (i, 0)),
            _mod_spec(l, 0),
            _layer_spec(w["w_in"].shape, e),
            _layer_spec(w["w_s"].shape, e),
            _const_spec(w["b_s_t"].shape),
            _layer_spec(w["w_pool"].shape, e),
            _const_spec(w["pool_scale"].shape),
            _layer_spec(w["w_out"].shape, e),
            _const_spec(w["ln_g"].shape),
            _const_spec(w["ln_b"].shape),
        ],
        out_specs=pl.BlockSpec((TILE_ROWS, D), lambda i: (i, 0)),
        out_shape=jax.ShapeDtypeStruct((n_rows, D), F32),
        scratch_shapes=[
            pltpu.VMEM((TILE_ROWS, A_WIDTH), F32),
            pltpu.VMEM((TILE_ROWS, A_WIDTH), BF16),
            pltpu.VMEM((TILE_ROWS, B_WIDTH), F32),
            pltpu.VMEM((TILE_ROWS, D), BF16),
            pltpu.VMEM((D, 2 * A_WIDTH + B_WIDTH), BF16),
            pltpu.VMEM((D, D), BF16),
        ],
        compiler_params=pltpu.CompilerParams(
            dimension_semantics=("arbitrary",), vmem_limit_bytes=VMEM_LIMIT),
        name=name,
    )(x, mod, w["w_in"], w["w_s"], w["b_s_t"], w["w_pool"], w["pool_scale"], w["w_out"],
      w["ln_g"], w["ln_b"])


def _conv3_block(z_ref, r0, lanes, cw, cb, seq_len):
    nv = FFN_RB // SUBLANES
    z3 = z_ref[r0:r0 + FFN_RB, lanes].reshape(nv, SUBLANES, LANES)
    sub = lax.broadcasted_iota(jnp.int32, (nv, SUBLANES, LANES), 1)
    down = pltpu.roll(z3, 1, 1)
    up = pltpu.roll(z3, SUBLANES - 1, 1)
    if r0 % seq_len == 0:
        prev = jnp.zeros((1, SUBLANES, LANES), F32)
    else:
        prev = pltpu.roll(z_ref[r0 - SUBLANES:r0, lanes], 1, 0)[None]
    if (r0 + FFN_RB) % seq_len == 0:
        nxt = jnp.zeros((1, SUBLANES, LANES), F32)
    else:
        nxt = pltpu.roll(z_ref[r0 + FFN_RB:r0 + FFN_RB + SUBLANES, lanes], SUBLANES - 1, 0)[None]
    zm = jnp.where(sub == 0, jnp.concatenate([prev, down[:-1]], axis=0), down)
    zp = jnp.where(sub == SUBLANES - 1, jnp.concatenate([up[1:], nxt], axis=0), up)
    y = cw[0][:, lanes] * zm + cw[1][:, lanes] * z3 + cw[2][:, lanes] * zp + cb[:, lanes]
    return y.reshape(FFN_RB, LANES)


def _ffn_kernel(x_ref, mod_ref, wup_ref, cw_ref, cb_ref, wd_ref, lng_ref, lnb_ref, o_ref,
                hb_ref, zg0_ref, zv0_ref, zg1_ref, zv1_ref, a_ref, *, layer, seq_len, row0, row_step):
    T = x_ref.shape[0]
    n_chunks = wd_ref.shape[0]
    shift, scale, gate = _mod_rows(mod_ref, row0, row_step)
    z_bufs = ((zg0_ref, zv0_ref), (zg1_ref, zv1_ref))

    def row_block(rb):
        return pl.ds(pl.multiple_of(rb * ROW_BLOCK, ROW_BLOCK), ROW_BLOCK)

    def modulate(rb, carry):
        rows = row_block(rb)
        hb_ref[rows, :] = (x_ref[rows, :] * (1.0 + scale) + shift).astype(BF16)
        return carry

    lax.fori_loop(0, T // ROW_BLOCK, modulate, 0)

    n_parts = T // FFN_DOT_ROWS
    blocks_per_part = FFN_DOT_ROWS // FFN_RB

    def up_proj(c, bufs, q):
        rows = slice(q * FFN_DOT_ROWS, (q + 1) * FFN_DOT_ROWS)
        bufs[0][rows, :] = _dot(hb_ref[rows, :], wup_ref[c])
        bufs[1][rows, :] = _dot(hb_ref[rows, :], wup_ref[c + n_chunks])

    def gate_blocks(c, bufs, q):
        cwg = [cw_ref[k, pl.ds(c, 1), :] for k in range(3)]
        cwv = [cw_ref[k, pl.ds(c + n_chunks, 1), :] for k in range(3)]
        cbg, cbv = cb_ref[pl.ds(c, 1), :], cb_ref[pl.ds(c + n_chunks, 1), :]
        for rb in range(q * blocks_per_part, (q + 1) * blocks_per_part):
            r0 = rb * FFN_RB
            for lt in range(FF_CHUNK // LANES):
                lanes = slice(lt * LANES, (lt + 1) * LANES)
                g = _conv3_block(bufs[0], r0, lanes, cwg, cbg, seq_len)
                v = _conv3_block(bufs[1], r0, lanes, cwv, cbv, seq_len)
                a_ref[c, r0:r0 + FFN_RB, lanes] = (g * jax.nn.sigmoid(g) * v).astype(BF16)

    for q in range(n_parts):
        up_proj(0, z_bufs[0], q)

    def chunk_pair(k, carry):
        c = 2 * k
        for q in range(n_parts):
            up_proj(c + 1, z_bufs[1], q)
            gate_blocks(c, z_bufs[0], q)
        for q in range(n_parts):
            up_proj(c + 2, z_bufs[0], q)
            gate_blocks(c + 1, z_bufs[1], q)
        return carry

    assert n_chunks % 2 == 1
    lax.fori_loop(0, n_chunks // 2, chunk_pair, 0)
    for q in range(n_parts):
        gate_blocks(n_chunks - 1, z_bufs[0], q)

    def out(rb, carry):
        rows = row_block(rb)
        acc = _dot(a_ref[0, rows, :], wd_ref[0])
        for c in range(1, n_chunks):
            acc = acc + _dot(a_ref[c, rows, :], wd_ref[c])
        r = ALPHA * x_ref[rows, :] + gate * acc
        o_ref[rows, :] = _layernorm(r, lng_ref[layer:layer + 1, :], lnb_ref[layer:layer + 1, :])
        return carry

    lax.fori_loop(0, T // ROW_BLOCK, out, 0)


def _ffn_layer(x, mod, w, l, seq_len, row0, row_step, name):
    n_rows = x.shape[0]
    n_chunks = D_FF // FF_CHUNK
    kern = functools.partial(_ffn_kernel, layer=l, seq_len=seq_len, row0=row0, row_step=row_step)
    params = [w["w_up"], w["conv_w"], w["conv_b"], w["w_down"]]
    return pl.pallas_call(
        kern,
        grid=(n_rows // TILE_ROWS,),
        in_specs=[pl.BlockSpec((TILE_ROWS, D), lambda i: (i, 0)), _mod_spec(l, 1)]
        + [_layer_spec(a.shape, l) for a in params]
        + [_const_spec(w["ln_g"].shape), _const_spec(w["ln_b"].shape)],
        out_specs=pl.BlockSpec((TILE_ROWS, D), lambda i: (i, 0)),
        out_shape=jax.ShapeDtypeStruct((n_rows, D), F32),
        scratch_shapes=[
            pltpu.VMEM((TILE_ROWS, D), BF16),
            pltpu.VMEM((TILE_ROWS, FF_CHUNK), F32),
            pltpu.VMEM((TILE_ROWS, FF_CHUNK), F32),
            pltpu.VMEM((TILE_ROWS, FF_CHUNK), F32),
            pltpu.VMEM((TILE_ROWS, FF_CHUNK), F32),
            pltpu.VMEM((n_chunks, TILE_ROWS, FF_CHUNK), BF16),
        ],
        compiler_params=pltpu.CompilerParams(
            dimension_semantics=("arbitrary",), vmem_limit_bytes=VMEM_LIMIT),
        name=name,
    )(x, mod, *params, w["ln_g"], w["ln_b"])


def _prep_up_kernel(w_ref, o_ref):
    for k in range(o_ref.shape[0]):
        o_ref[k] = w_ref[:, k * FF_CHUNK:(k + 1) * FF_CHUNK].astype(BF16)


def _prep_down_kernel(w_ref, o_ref):
    o_ref[...] = w_ref[...].astype(BF16)


def _prep_ffn_weights(w_up, w_down):
    n_ch = 2 * D_FF // FF_CHUNK
    up = pl.pallas_call(
        _prep_up_kernel,
        grid=(DEPTH, 2),
        in_specs=[pl.BlockSpec((None, D, D_FF), lambda l, j: (l, 0, j))],
        out_specs=pl.BlockSpec((None, n_ch // 2, D, FF_CHUNK), lambda l, j: (l, j, 0, 0)),
        out_shape=jax.ShapeDtypeStruct((DEPTH, n_ch, D, FF_CHUNK), BF16),
        compiler_params=pltpu.CompilerParams(
            dimension_semantics=("arbitrary", "arbitrary"), vmem_limit_bytes=VMEM_LIMIT),
        name="ffn_up_bf16",
    )(w_up)
    half = D_FF // 2
    down = pl.pallas_call(
        _prep_down_kernel,
        grid=(DEPTH, 2),
        in_specs=[pl.BlockSpec((None, half, D), lambda l, j: (l, j, 0))],
        out_specs=pl.BlockSpec((None, half, D), lambda l, j: (l, j, 0)),
        out_shape=jax.ShapeDtypeStruct((DEPTH, D_FF, D), BF16),
        compiler_params=pltpu.CompilerParams(
            dimension_semantics=("arbitrary", "arbitrary"), vmem_limit_bytes=VMEM_LIMIT),
        name="ffn_down_bf16",
    )(w_down)
    return up, down.reshape(DEPTH, n_ch // 2, FF_CHUNK, D)


def _conv4_block(xr_ref, cbk, r0, cw, cb, seq_len):
    nv = ODD_RB // SUBLANES
    z3 = xr_ref[cbk, r0:r0 + ODD_RB, :].reshape(nv, SUBLANES, LANES)
    sub = lax.broadcasted_iota(jnp.int32, (nv, SUBLANES, LANES), 1)
    down = pltpu.roll(z3, 1, 1)
    up1 = pltpu.roll(z3, SUBLANES - 1, 1)
    up2 = pltpu.roll(z3, SUBLANES - 2, 1)
    zero = jnp.zeros((1, SUBLANES, LANES), F32)
    if r0 % seq_len == 0:
        prev = zero
    else:
        prev = pltpu.roll(xr_ref[cbk, r0 - SUBLANES:r0, :], 1, 0)[None]
    if (r0 + ODD_RB) % seq_len == 0:
        nxt1 = nxt2 = zero
    else:
        nx = xr_ref[cbk, r0 + ODD_RB:r0 + ODD_RB + SUBLANES, :]
        nxt1 = pltpu.roll(nx, SUBLANES - 1, 0)[None]
        nxt2 = pltpu.roll(nx, SUBLANES - 2, 0)[None]
    zm = jnp.where(sub == 0, jnp.concatenate([prev, down[:-1]], axis=0), down)
    zp1 = jnp.where(sub == SUBLANES - 1, jnp.concatenate([up1[1:], nxt1], axis=0), up1)
    zp2 = jnp.where(sub >= SUBLANES - 2, jnp.concatenate([up2[1:], nxt2], axis=0), up2)
    y = cw[0:1, :] * zm + cw[1:2, :] * z3 + cw[2:3, :] * zp1 + cw[3:4, :] * zp2 + cb
    return y.reshape(ODD_RB, LANES)


def _scan_slabs(a, b, reverse):
    nv = a.shape[0] // SUBLANES
    a = a.reshape(nv, SUBLANES, LANES)
    b = b.reshape(nv, SUBLANES, LANES)
    sub = lax.broadcasted_iota(jnp.int32, (nv, SUBLANES, LANES), 1)
    for d in (1, 2, 4):
        if reverse:
            valid, sh = sub < SUBLANES - d, SUBLANES - d
        else:
            valid, sh = sub >= d, d
        a_prev = jnp.where(valid, pltpu.roll(a, sh, 1), 1.0)
        b_prev = jnp.where(valid, pltpu.roll(b, sh, 1), 0.0)
        b = b + a * b_prev
        a = a * a_prev
    return a.reshape(nv * SUBLANES, LANES), b.reshape(nv * SUBLANES, LANES)


def _scan_ends(a, b, posg, n_groups, reverse):
    n = a.shape[0]
    d = 1
    while d < n_groups:
        if reverse:
            valid, sh = posg < n_groups - d, n - d
        else:
            valid, sh = posg >= d, d
        a_prev = jnp.where(valid, pltpu.roll(a, sh, 0), 1.0)
        b_prev = jnp.where(valid, pltpu.roll(b, sh, 0), 0.0)
        b = b + a * b_prev
        a = a * a_prev
        d *= 2
    return a, b


def _rope_pairs(x, cos, sin_signed, even_lane):
    n = x.shape[1]
    swapped = jnp.where(even_lane, pltpu.roll(x, n - 1, 1), pltpu.roll(x, 1, 1))
    return x * cos + swapped * sin_signed


def _rmsnorm(x, g):
    return x * lax.rsqrt(jnp.mean(x * x, axis=-1, keepdims=True) + RMS_EPS) * g


def _odd_kernel(*refs, layer, seq_len, latent, past_len, row0, row_step):
    n_in = 21 if latent else 16
    n_out = 1 if latent else 4
    (x_ref, mod_ref, win_ref, cw_ref, cb_ref, wgate_ref, bgate_ref, lam_ref, qg_ref, kvg_ref,
     wuq_ref, wuk_ref, wuv_ref, wout_ref, lng_ref, lnb_ref) = refs[:16]
    if latent:
        cckv_ref, ckr_ref, st_ref, cos_ref, sin_ref = refs[16:21]
        (o_ref,) = refs[n_in:n_in + n_out]
    else:
        o_ref, ckv_out_ref, kr_out_ref, st_out_ref = refs[n_in:n_in + n_out]
    (xr_ref, gr_ref, xc_ref, pre_ref, a_ref, b_ref, hrep_ref, q_ref, k_ref, v_ref, yd_ref) = refs[n_in + n_out:]

    T = x_ref.shape[0]
    n_blocks = T // ROW_BLOCK
    n_tiles = W_C // LANES
    half_w = W_C // 2
    shift, scale, gate = _mod_rows(mod_ref, row0, row_step)
    o_q = 2 * W_C
    o_kv = o_q + Q_LORA
    o_kr = o_kv + KV_LORA
    pair_w = 2 * HEAD_PAD
    even_lane = (lax.broadcasted_iota(jnp.int32, (ROW_BLOCK, HEAD_PAD), 1) & 1) == 0
    softplus = []
    for direction in range(2):
        neg = -lam_ref[direction:direction + 1, :]
        softplus.append(jnp.maximum(neg, 0.0) + jnp.log1p(jnp.exp(-jnp.abs(neg))))

    def stage_a(rb):
        rows = slice(rb * ROW_BLOCK, (rb + 1) * ROW_BLOCK)
        k_rows = slice(past_len + rb * ROW_BLOCK, past_len + (rb + 1) * ROW_BLOCK)
        st = {}

        def t_xr():
            st["hb"] = (x_ref[rows, :] * (1.0 + scale) + shift).astype(BF16)
            z = _dot(st["hb"], win_ref[:, 0:W_C])
            for cbk in range(n_tiles):
                xr_ref[cbk, rows, :] = z[:, cbk * LANES:(cbk + 1) * LANES]

        def t_gr():
            gr_ref[rows, :] = _dot(st["hb"], win_ref[:, W_C:o_q])

        def t_lat():
            hb = st["hb"]
            st["qn"] = _rmsnorm(_dot(hb, win_ref[:, o_q:o_kv]), qg_ref[...]).astype(BF16)
            ckv = _rmsnorm(_dot(hb, win_ref[:, o_kv:o_kr]), kvg_ref[...])
            kr = _dot(hb, win_ref[:, o_kr:o_kr + HEAD_PAD])
            if latent:
                st["cos"] = cos_ref[rows, :]
                st["sin"] = sin_ref[rows, :]
                kr = _rope_pairs(kr, st["cos"], st["sin"], even_lane)
            else:
                ckv_out_ref[rows, :] = ckv
                kr_out_ref[rows, :] = kr[:, KR_LANE:KR_LANE + QK_ROPE]
            st["kr"] = kr
            st["ckvb"] = ckv.astype(BF16)

        def t_v():
            v_ref[k_rows, :] = _dot(st["ckvb"], wuv_ref[...]).astype(BF16)

        def t_pair(pair):
            def run():
                pcols = slice(pair * pair_w, (pair + 1) * pair_w)
                q2 = _dot(st["qn"], wuq_ref[:, pcols])
                k2 = _dot(st["ckvb"], wuk_ref[:, pcols])
                for hh in range(2):
                    cols = slice(pair * pair_w + hh * HEAD_PAD, pair * pair_w + (hh + 1) * HEAD_PAD)
                    qh = q2[:, hh * HEAD_PAD:(hh + 1) * HEAD_PAD]
                    if latent:
                        qh = _rope_pairs(qh, st["cos"], st["sin"], even_lane)
                    q_ref[rows, cols] = qh.astype(BF16)
                    k_ref[k_rows, cols] = (k2[:, hh * HEAD_PAD:(hh + 1) * HEAD_PAD] + st["kr"]).astype(BF16)
            return run

        return [t_xr, t_gr, t_lat, t_v] + [t_pair(p) for p in range(D_HEADS // 2)]

    def conv_thunks(rb):
        par = rb % xc_ref.shape[0]

        def t_conv(cbk):
            def run():
                cw = cw_ref[:, cbk * LANES:(cbk + 1) * LANES]
                cb = cb_ref[:, cbk * LANES:(cbk + 1) * LANES]
                for rs in range(ROW_BLOCK // ODD_RB):
                    r0 = rb * ROW_BLOCK + rs * ODD_RB
                    xc_ref[par, rs * ODD_RB:(rs + 1) * ODD_RB, cbk * LANES:(cbk + 1) * LANES] = _conv4_block(
                        xr_ref, cbk, r0, cw, cb, seq_len)
            return run

        return [t_conv(cbk) for cbk in range(n_tiles)]

    def gate_thunks(rb):
        par = rb % xc_ref.shape[0]
        thunks = []
        for direction in range(2):
            for j in range(2):
                slot = (2 * direction + j) % pre_ref.shape[0]

                def t_dot(direction=direction, j=j, slot=slot):
                    xcb = xc_ref[par, :, j * half_w:(j + 1) * half_w].astype(BF16)
                    pre_ref[slot, 0] = _dot(xcb, wgate_ref[direction, 0, j])
                    pre_ref[slot, 1] = _dot(xcb, wgate_ref[direction, 1, j])

                def t_post(direction=direction, j=j, slot=slot):
                    for rs in range(ROW_BLOCK // ODD_RB):
                        lrows = slice(rs * ODD_RB, (rs + 1) * ODD_RB)
                        grows = slice(rb * ROW_BLOCK + rs * ODD_RB, rb * ROW_BLOCK + (rs + 1) * ODD_RB)
                        for lt in range(half_w // LANES):
                            cbk = j * (half_w // LANES) + lt
                            lanes = slice(lt * LANES, (lt + 1) * LANES)
                            glanes = slice(cbk * LANES, (cbk + 1) * LANES)
                            xcv = xc_ref[par, lrows, glanes]
                            r = jax.nn.sigmoid(pre_ref[slot, 0, lrows, lanes] + bgate_ref[direction, 0:1, glanes])
                            i = jax.nn.sigmoid(pre_ref[slot, 1, lrows, lanes] + bgate_ref[direction, 1:2, glanes])
                            log_a = -RG_C * r * softplus[direction][:, glanes]
                            a = jnp.exp(log_a)
                            b = jnp.sqrt(jnp.tanh(-log_a) * (a * a + 1.0)) * i * xcv
                            a, b = _scan_slabs(a, b, direction == 1)
                            a_ref[direction, cbk, grows, :] = a
                            b_ref[direction, cbk, grows, :] = b

                thunks += [t_dot, t_post]
        return thunks

    def combine(direction, cbk, t0, n_rows, h0):
        reverse = direction == 1
        n_ends = n_rows // SUBLANES
        end_row = 0 if reverse else SUBLANES - 1
        posg = lax.broadcasted_iota(jnp.int32, (n_ends, LANES), 0)
        a_end = a_ref[direction, cbk, pl.ds(t0 + end_row, n_ends, stride=SUBLANES), :]
        b_end = b_ref[direction, cbk, pl.ds(t0 + end_row, n_ends, stride=SUBLANES), :]
        a_tot, b_tot = _scan_ends(a_end, b_end, posg, n_ends, reverse)
        ends = b_tot if h0 is None else a_tot * h0 + b_tot
        first = jnp.zeros((1, LANES), F32) if h0 is None else h0
        if reverse:
            carry_in = jnp.where(posg < n_ends - 1, pltpu.roll(ends, n_ends - 1, 0), first)
        else:
            carry_in = jnp.where(posg >= 1, pltpu.roll(ends, 1, 0), first)
        for rr in range(SUBLANES):
            hrep_ref[pl.ds(t0 + rr, n_ends, stride=SUBLANES), :] = carry_in
        for rs in range(n_rows // ODD_RB):
            rows = slice(t0 + rs * ODD_RB, t0 + (rs + 1) * ODD_RB)
            h = a_ref[direction, cbk, rows, :] * hrep_ref[rows, :] + b_ref[direction, cbk, rows, :]
            if direction == 0:
                xr_ref[cbk, rows, :] = h
            else:
                xr_ref[cbk, rows, :] = xr_ref[cbk, rows, :] + h
        return ends

    def attend_pair(q_rows, k_rows, pair):
        o_pair = None
        for h in (2 * pair, 2 * pair + 1):
            cols = slice(h * HEAD_PAD, (h + 1) * HEAD_PAD)
            s = lax.dot_general(q_ref[q_rows, cols], k_ref[k_rows, cols],
                                (((1,), (1,)), ((), ())), preferred_element_type=F32) * ATTN_SCALE
            m = jnp.max(s, axis=-1, keepdims=True)
            p = jnp.exp(s - m)
            l = jnp.sum(p, axis=-1, keepdims=True)
            o = _dot(p.astype(BF16), v_ref[k_rows, cols]) / l
            o_pair = o if o_pair is None else o_pair + o
        yd_ref[q_rows, pair * HEAD_PAD:(pair + 1) * HEAD_PAD] = o_pair.astype(BF16)

    def out_block(rows):
        yc = jnp.concatenate(
            [xr_ref[cbk, rows, :] * _gelu(gr_ref[rows, cbk * LANES:(cbk + 1) * LANES]) for cbk in range(n_tiles)],
            axis=1).astype(BF16)
        d = _dot(yc, wout_ref[0:W_C, :]) + _dot(yd_ref[rows, :], wout_ref[W_C:, :])
        r = ALPHA * x_ref[rows, :] + gate * d
        o_ref[rows, :] = _layernorm(r, lng_ref[layer:layer + 1, :], lnb_ref[layer:layer + 1, :])

    if not latent:
        def stage_b(s):
            t0 = s * seq_len
            rows = slice(t0, t0 + seq_len)
            thunks = conv_thunks(s) + gate_thunks(s)
            for direction in range(2):
                for cbk in range(n_tiles):
                    def t_comb(direction=direction, cbk=cbk):
                        ends = combine(direction, cbk, t0, seq_len, None)
                        last = 0 if direction == 1 else seq_len // SUBLANES - 1
                        st_out_ref[s, direction:direction + 1, cbk * LANES:(cbk + 1) * LANES] = ends[last:last + 1, :]
                    thunks.append(t_comb)
            return thunks

        assert seq_len == ROW_BLOCK
        for t in stage_a(0):
            t()
        for s in range(n_blocks):
            rows = slice(s * seq_len, (s + 1) * seq_len)
            attn = [functools.partial(attend_pair, rows, rows, pair) for pair in range(D_HEADS // 2)]
            nxt = stage_a(s + 1) if s + 1 < n_blocks else []
            mxu = []
            _interleave([functools.partial(mxu.append, t) for t in nxt] or [lambda: None],
                        [functools.partial(mxu.append, t) for t in attn])
            _interleave(stage_b(s), mxu)
            out_block(rows)
        return

    assert seq_len == T
    for t in stage_a(0):
        t()
    for rb in range(n_blocks):
        nxt = stage_a(rb + 1) if rb + 1 < n_blocks else []
        if nxt:
            nxt[0]()
        _interleave(conv_thunks(rb) + gate_thunks(rb), nxt[1:])

    def cached(cbk, carry):
        rows = pl.ds(pl.multiple_of(cbk * ROW_BLOCK, ROW_BLOCK), ROW_BLOCK)
        cckv = cckv_ref[rows, :].astype(BF16)
        ckr = ckr_ref[rows, :]
        v_ref[rows, :] = _dot(cckv, wuv_ref[...]).astype(BF16)
        for pair in range(D_HEADS // 2):
            pcols = slice(pair * pair_w, (pair + 1) * pair_w)
            k2 = _dot(cckv, wuk_ref[:, pcols])
            for hh in range(2):
                cols = slice(pair * pair_w + hh * HEAD_PAD, pair * pair_w + (hh + 1) * HEAD_PAD)
                k_ref[rows, cols] = (k2[:, hh * HEAD_PAD:(hh + 1) * HEAD_PAD] + ckr).astype(BF16)
        return carry

    lax.fori_loop(0, past_len // ROW_BLOCK, cached, 0)

    n_keys = past_len + seq_len

    def tail(qb, carry):
        q_rows = pl.ds(pl.multiple_of(qb * ROW_BLOCK, ROW_BLOCK), ROW_BLOCK)
        k_rows = slice(0, n_keys)
        rg = []
        for direction in range(2):
            def t_comb(direction=direction):
                h0 = st_ref[direction, pl.ds(qb, 1), :]
                combine(direction, qb, 0, T, h0)
            rg.append(t_comb)
        _interleave([functools.partial(attend_pair, q_rows, k_rows, pair) for pair in range(D_HEADS // 2)], rg)
        return carry

    assert n_tiles == n_blocks
    lax.fori_loop(0, n_blocks, tail, 0)

    def out(rb, carry):
        out_block(pl.ds(pl.multiple_of(rb * ROW_BLOCK, ROW_BLOCK), ROW_BLOCK))
        return carry

    lax.fori_loop(0, n_blocks, out, 0)


def _odd_layer(x, mod, w, l, seq_len, row0, row_step, name, latent_inputs=None):
    n_rows = x.shape[0]
    n_tiles = n_rows // TILE_ROWS
    latent = latent_inputs is not None
    past_len = latent_inputs["cache_ckv"].shape[1] if latent else 0
    n_seq = TILE_ROWS // seq_len
    kern = functools.partial(_odd_kernel, layer=l, seq_len=seq_len, latent=latent, past_len=past_len,
                             row0=row0, row_step=row_step)
    weights = [w["w_in"], w["conv_w"], w["conv_b"], w["w_gate"], w["b_gate"], w["lam"], w["q_g"], w["kv_g"],
               w["w_uq"], w["w_uk"], w["w_uv"], w["w_out"]]
    out_mode = pl.Buffered(1) if latent else pl.Buffered(2)
    in_specs = [pl.BlockSpec((TILE_ROWS, D), lambda i: (i, 0)),
                _mod_spec(l, 0)]
    in_specs += [_const_spec(a.shape) for a in weights]
    in_specs += [_const_spec(w["ln_g"].shape), _const_spec(w["ln_b"].shape)]
    args = [x, mod] + weights + [w["ln_g"], w["ln_b"]]
    out_specs = [pl.BlockSpec((TILE_ROWS, D), lambda i: (i, 0), pipeline_mode=out_mode)]
    out_shape = [jax.ShapeDtypeStruct((n_rows, D), F32)]
    if latent:
        li = latent_inputs
        in_specs += [
            pl.BlockSpec((None, past_len, KV_LORA), lambda i: (i, 0, 0)),
            pl.BlockSpec((None, past_len, HEAD_PAD), lambda i: (i, 0, 0)),
            pl.BlockSpec((None, 2, W_C // LANES, LANES), lambda i: (i, 0, 0, 0)),
            _const_spec(li["cos"].shape),
            _const_spec(li["sin"].shape),
        ]
        args += [li["cache_ckv"], li["cache_kr"], li["state"], li["cos"], li["sin"]]
    else:
        out_specs += [
            pl.BlockSpec((TILE_ROWS, KV_LORA), lambda i: (i, 0)),
            pl.BlockSpec((TILE_ROWS, QK_ROPE), lambda i: (i, 0)),
            pl.BlockSpec((n_seq, 2, W_C), lambda i: (i, 0, 0)),
        ]
        out_shape += [
            jax.ShapeDtypeStruct((n_rows, KV_LORA), F32),
            jax.ShapeDtypeStruct((n_rows, QK_ROPE), F32),
            jax.ShapeDtypeStruct((n_rows // seq_len, 2, W_C), F32),
        ]
    n_keys_buf = past_len + TILE_ROWS
    n_slots = 1 if latent else 2
    scratch = [
        pltpu.VMEM((W_C // LANES, TILE_ROWS, LANES), F32),
        pltpu.VMEM((TILE_ROWS, W_C), F32),
        pltpu.VMEM((n_slots, ROW_BLOCK, W_C), F32),
        pltpu.VMEM((n_slots, 2, ROW_BLOCK, W_C // 2), F32),
        pltpu.VMEM((2, W_C // LANES, TILE_ROWS, LANES), F32),
        pltpu.VMEM((2, W_C // LANES, TILE_ROWS, LANES), F32),
        pltpu.VMEM((TILE_ROWS, LANES), F32),
        pltpu.VMEM((TILE_ROWS, D_HEADS * HEAD_PAD), BF16),
        pltpu.VMEM((n_keys_buf, D_HEADS * HEAD_PAD), BF16),
        pltpu.VMEM((n_keys_buf, D_HEADS * HEAD_PAD), BF16),
        pltpu.VMEM((TILE_ROWS, W_C), BF16),
    ]
    return pl.pallas_call(
        kern,
        grid=(n_tiles,),
        in_specs=in_specs,
        out_specs=out_specs,
        out_shape=out_shape,
        scratch_shapes=scratch,
        compiler_params=pltpu.CompilerParams(
            dimension_semantics=("arbitrary",), vmem_limit_bytes=VMEM_LIMIT),
        name=name,
    )(*args)


def _axial_rope_tables(rows):
    half = QK_ROPE // 2
    inv = (ROPE_BASE ** (-np.arange(0, half, 2, dtype=np.float32) / half)).astype(np.float32)
    r = np.repeat(np.arange(rows, dtype=np.float32), GRID_W)
    col = np.tile(np.arange(GRID_W, dtype=np.float32), rows)
    ang = np.concatenate([r[:, None] * inv, col[:, None] * inv], axis=-1).astype(np.float32)
    cos = np.repeat(np.cos(ang), 2, axis=-1)
    sin = np.repeat(np.sin(ang), 2, axis=-1) * np.tile(np.array([-1.0, 1.0], np.float32), half)
    n = ang.shape[0]
    cos_t = np.ones((n, HEAD_PAD), np.float32)
    sin_t = np.zeros((n, HEAD_PAD), np.float32)
    cos_t[:, KR_LANE:KR_LANE + QK_ROPE] = cos
    sin_t[:, KR_LANE:KR_LANE + QK_ROPE] = sin
    return jnp.asarray(cos_t), jnp.asarray(sin_t)


def _odd_prep_kernel(win_ref, wa_ref, wx_ref, wuq_ref, wuk_ref, wuv_ref, wout_ref,
                     win_o, gate_o, wuq_o, wuk_o, wuv_o, wout_o):
    n_main = 2 * W_C + Q_LORA + KV_LORA
    win_o[:, 0:n_main] = win_ref[:, 0:n_main].astype(BF16)
    win_o[:, n_main:] = jnp.zeros((D, HEAD_PAD), BF16)
    win_o[:, n_main + KR_LANE:n_main + KR_LANE + QK_ROPE] = win_ref[:, n_main:].astype(BF16)
    wout_o[...] = wout_ref[...].astype(BF16)
    wuq_o[...] = jnp.zeros_like(wuq_o)
    wuk_o[...] = jnp.zeros_like(wuk_o)
    wuv_o[...] = jnp.zeros_like(wuv_o)
    for h in range(D_HEADS):
        wuq_o[:, h * HEAD_PAD:h * HEAD_PAD + QK_NOPE + QK_ROPE] = wuq_ref[:, h, :].astype(BF16)
        wuk_o[:, h * HEAD_PAD:h * HEAD_PAD + QK_NOPE] = wuk_ref[:, h, :].astype(BF16)
        v0 = h * HEAD_PAD + (h % 2) * V_DIM
        wuv_o[:, v0:v0 + V_DIM] = wuv_ref[:, h, :].astype(BF16)
    gate_o[...] = jnp.zeros_like(gate_o)
    per = (W_C // 2) // C_BLOCK
    for direction in range(2):
        for g, w_ref in enumerate((wa_ref, wx_ref)):
            for h in range(C_HEADS):
                j, hh = divmod(h, per)
                blk = slice(hh * C_BLOCK, (hh + 1) * C_BLOCK)
                gate_o[direction, g, j, blk, blk] = w_ref[direction, h].astype(BF16)


def _prep_odd_weights(w_in, w_a, w_x, w_uq, w_uk, w_uv, w_out):
    ins = [w_in, w_a, w_x, w_uq, w_uk, w_uv, w_out]
    outs = [
        jax.ShapeDtypeStruct((D, ODD_IN_PAD), BF16),
        jax.ShapeDtypeStruct((2, 2, 2, W_C // 2, W_C // 2), BF16),
        jax.ShapeDtypeStruct((Q_LORA, D_HEADS * HEAD_PAD), BF16),
        jax.ShapeDtypeStruct((KV_LORA, D_HEADS * HEAD_PAD), BF16),
        jax.ShapeDtypeStruct((KV_LORA, D_HEADS * HEAD_PAD), BF16),
        jax.ShapeDtypeStruct((D, D), BF16),
    ]
    return pl.pallas_call(
        _odd_prep_kernel,
        in_specs=[pl.BlockSpec(a.shape, lambda n=a.ndim: (0,) * n) for a in ins],
        out_specs=[pl.BlockSpec(o.shape, lambda n=len(o.shape): (0,) * n) for o in outs],
        out_shape=outs,
        compiler_params=pltpu.CompilerParams(vmem_limit_bytes=VMEM_LIMIT),
        name="odd_weights_bf16",
    )(*ins)


def kernel(x_prompt, x_sample, cache_mla_ckv, cache_mla_krope, state_rglru, c, c_ctx, w_mod, b_mod, ln1_g, ln1_b, ln2_g, ln2_b, even_w_in, even_w_s, even_b_s, even_w_pool, even_pool_scale, even_w_out, odd_w_in, rg_conv_w, rg_conv_b, rg_w_a, rg_b_a, rg_w_x, rg_b_x, rg_lam, mla_q_g, mla_kv_g, mla_w_uq, mla_w_uk, mla_w_uv, odd_w_out, ffn_w_up, ffn_conv_w, ffn_conv_b, ffn_w_down):
    batch, seq, _ = x_prompt.shape
    dec_batch, dec_seq, _ = x_sample.shape
    assert TILE_ROWS % seq == 0 and dec_seq == TILE_ROWS and dec_batch + 1 <= SUBLANES

    mod = _modulation(c_ctx, c, w_mod, b_mod)

    wup_bf, wdown_bf = _prep_ffn_weights(ffn_w_up, ffn_w_down)
    n_ch = 2 * D_FF // FF_CHUNK
    wf = {"ln_g": ln2_g, "ln_b": ln2_b,
          "w_up": wup_bf, "conv_w": ffn_conv_w.reshape(DEPTH, 3, n_ch, FF_CHUNK),
          "conv_b": ffn_conv_b.reshape(DEPTH, n_ch, FF_CHUNK), "w_down": wdown_bf}
    ln1 = {"ln_g": ln1_g, "ln_b": ln1_b}

    xp = x_prompt.reshape(batch * seq, D)
    xs = x_sample.reshape(dec_batch * dec_seq, D)
    new_ckv = new_kr = new_state = None
    for l in range(DEPTH):
        if l % 2 == 0:
            e = l // 2
            w = dict(ln1, w_in=even_w_in, w_s=even_w_s, b_s_t=even_b_s[e].T, w_pool=even_w_pool,
                     pool_scale=even_pool_scale[e][None, :], w_out=even_w_out)
            xp = _even_layer(xp, mod, w, l, seq, 0, 0, f"even{l}_ctx")
            xs = _even_layer(xs, mod, w, l, dec_seq, 1, 1, f"even{l}_lat")
        else:
            o = l // 2
            w_in_bf, w_gate_bf, w_uq_bf, w_uk_bf, w_uv_bf, w_out_bf = _prep_odd_weights(
                odd_w_in[o], rg_w_a[o], rg_w_x[o], mla_w_uq[o], mla_w_uk[o], mla_w_uv[o], odd_w_out[o])
            b_gate = jnp.stack([rg_b_a[o], rg_b_x[o]], axis=1)
            w = dict(ln1,
                     w_in=w_in_bf, conv_w=rg_conv_w[o], conv_b=rg_conv_b[o][None, :],
                     w_gate=w_gate_bf, b_gate=b_gate, lam=rg_lam[o],
                     q_g=mla_q_g[o][None, :], kv_g=mla_kv_g[o][None, :],
                     w_uq=w_uq_bf, w_uk=w_uk_bf, w_uv=w_uv_bf, w_out=w_out_bf)
            xp, ckv, kr, st = _odd_layer(xp, mod, w, l, seq, 0, 0, f"odd{l}_ctx")
            new_ckv = ckv.reshape(batch, 1, seq, KV_LORA)
            new_kr = kr.reshape(batch, 1, seq, QK_ROPE)
            new_state = st.reshape(batch, 1, 2, W_C)
            cos_t, sin_t = _axial_rope_tables(dec_seq // GRID_W)
            latent_inputs = {
                "cache_ckv": cache_mla_ckv[:, o],
                "cache_kr": jnp.pad(cache_mla_krope[:, o],
                                    ((0, 0), (0, 0), (KR_LANE, HEAD_PAD - KR_LANE - QK_ROPE))),
                "state": state_rglru[:, o].reshape(dec_batch, 2, W_C // LANES, LANES),
                "cos": cos_t, "sin": sin_t,
            }
            (xs,) = _odd_layer(xs, mod, w, l, dec_seq, 1, 1, f"odd{l}_lat", latent_inputs)
        xp = _ffn_layer(xp, mod, wf, l, seq, 0, 0, f"ffn{l}_ctx")
        xs = _ffn_layer(xs, mod, wf, l, dec_seq, 1, 1, f"ffn{l}_lat")
    return (xp.reshape(batch, seq, D), xs.reshape(dec_batch, dec_seq, D), new_ckv, new_kr, new_state)
```

```python
import functools
import math

import jax
import jax.numpy as jnp
import numpy as np
from jax import lax
from jax.experimental import pallas as pl
from jax.experimental.pallas import tpu as pltpu

F32 = jnp.float32
BF16 = jnp.bfloat16

D = 1024
DEPTH = 2
GRID_W = 64
ALPHA = (2 * DEPTH) ** 0.25
LN_EPS = 1e-5
RMS_EPS = 1e-6
A_HEADS = 4
A_WIDTH = 512
A_GROUP_W = A_WIDTH // A_HEADS
CHUNK = 128
POOL_WINDOWS = (2, 4, 8, 16)
B_GROUP = 128
B_WIDTH = 512
W_C = 512
C_HEADS = 8
C_BLOCK = 64
RG_C = 8.0
D_HEADS = 8
Q_LORA = 384
KV_LORA = 256
QK_NOPE = 64
QK_ROPE = 32
V_DIM = 64
ROPE_BASE = 10000.0
ATTN_SCALE = 1.0 / math.sqrt(QK_NOPE + QK_ROPE)
D_FF = 2816

LANES = 128
SUBLANES = 8
HEAD_PAD = LANES
ODD_IN_PAD = 2 * W_C + Q_LORA + KV_LORA + HEAD_PAD
KR_LANE = QK_NOPE
TILE_ROWS = 1024
ROW_BLOCK = 256
MOD_COLS = 3072
FF_CHUNK = 256
FFN_DOT_ROWS = 128
FFN_RB = 64
EVEN_RB = 64
ODD_RB = 64
VMEM_LIMIT = 61 * 1024 * 1024


def _dot(a, b):
    return jnp.dot(a, b, preferred_element_type=F32)


def _gelu(x):
    return x * (0.5 * (1.0 + jnp.tanh(0.7978845608028654 * (x + 0.044715 * (x * x * x)))))


def _layernorm(r, g, b):
    mu = jnp.mean(r, axis=-1, keepdims=True)
    c = r - mu
    var = jnp.mean(c * c, axis=-1, keepdims=True)
    return c * lax.rsqrt(var + LN_EPS) * g + b


def _shift_rows(x, d, pos, seq_len):
    n = x.shape[0]
    y = pltpu.roll(x, (-d) % n, 0)
    valid = (pos < seq_len - d) if d > 0 else (pos >= -d)
    return jnp.where(valid, y, 0.0)


def _mod_kernel(cctx_ref, c_ref, w_ref, b_ref, o_ref, s_ref):
    n_lat = c_ref.shape[0]
    s_ref[...] = jnp.zeros_like(s_ref)
    s_ref[0:1, :] = cctx_ref[...]
    s_ref[1:1 + n_lat, :] = c_ref[...]
    s = s_ref[...]
    s = s * jax.nn.sigmoid(s)
    bias = b_ref[pl.ds(pl.program_id(0), 1), :]
    o_ref[...] = _dot(s.astype(BF16), w_ref[...].astype(BF16)) + bias


def _modulation(c_ctx, c, w_mod, b_mod):
    nc = MOD_COLS
    return pl.pallas_call(
        _mod_kernel,
        grid=(DEPTH, 6 * D // nc),
        in_specs=[
            pl.BlockSpec((1, D), lambda l, j: (0, 0)),
            pl.BlockSpec(c.shape, lambda l, j: (0, 0)),
            pl.BlockSpec((None, D, nc), lambda l, j: (l, 0, j)),
            pl.BlockSpec((DEPTH, nc), lambda l, j: (0, j)),
        ],
        out_specs=pl.BlockSpec((None, SUBLANES, nc), lambda l, j: (l, 0, j)),
        out_shape=jax.ShapeDtypeStruct((DEPTH, SUBLANES, 6 * D), F32),
        scratch_shapes=[pltpu.VMEM((SUBLANES, D), F32)],
        compiler_params=pltpu.CompilerParams(
            dimension_semantics=("arbitrary", "arbitrary"), vmem_limit_bytes=VMEM_LIMIT),
        name="modulation",
    )(c_ctx.reshape(1, D), c, w_mod, b_mod)


def _mod_rows(mod_ref, row0, row_step):
    row = row0 + pl.program_id(0) * row_step
    m = mod_ref[pl.ds(row, 1), :]
    return m[:, 0:D], m[:, D:2 * D], m[:, 2 * D:3 * D]


def _const_spec(shape):
    nd = len(shape)
    return pl.BlockSpec(shape, lambda *_: (0,) * nd, pipeline_mode=pl.Buffered(1))


def _layer_spec(shape, l):
    nd = len(shape) - 1
    return pl.BlockSpec((None,) + tuple(shape[1:]), lambda *_: (l,) + (0,) * nd, pipeline_mode=pl.Buffered(1))


def _mod_spec(l, half):
    return pl.BlockSpec((None, SUBLANES, 3 * D), lambda *_: (l, 0, half))


def _interleave(primary, secondary):
    n_p, n_s = len(primary), len(secondary)
    done = 0
    for k, thunk in enumerate(primary):
        thunk()
        want = ((k + 1) * n_s) // n_p
        while done < want:
            secondary[done]()
            done += 1
    while done < n_s:
        secondary[done]()
        done += 1


def _pool_block(p_ref, gi, r0, half, seq_len, n_rows):
    cols = slice(gi * B_GROUP, (gi + 1) * B_GROUP)
    zero = jnp.zeros((SUBLANES, B_GROUP), F32)
    top = zero if r0 == 0 else p_ref[r0 - SUBLANES:r0, cols]
    bot = zero if r0 + EVEN_RB == n_rows else p_ref[r0 + EVEN_RB:r0 + EVEN_RB + SUBLANES, cols]
    pg = jnp.concatenate([top, p_ref[r0:r0 + EVEN_RB, cols], bot], axis=0)
    pos = (lax.broadcasted_iota(jnp.int32, pg.shape, 0) + (r0 - SUBLANES)) & (seq_len - 1)
    fwd = pg
    bwd = _shift_rows(pg, -1, pos, seq_len)
    k = 1
    while k < half:
        fwd = fwd + _shift_rows(fwd, k, pos, seq_len)
        bwd = bwd + _shift_rows(bwd, -k, pos, seq_len)
        k *= 2
    cnt = jnp.minimum(pos + half, seq_len) - jnp.maximum(pos - half, 0)
    pooled = (fwd + bwd) / cnt.astype(F32) - pg
    return pooled[SUBLANES:SUBLANES + EVEN_RB]


def _even_kernel(x_ref, mod_ref, win_f32_ref, ws_ref, bst_ref, wpool_ref, pscale_ref, wout_f32_ref,
                 lng_ref, lnb_ref, o_ref, u_ref, vn_ref, p_ref, ycat_ref, win_ref, wout_ref,
                 *, layer, seq_len, row0, row_step):
    T = x_ref.shape[0]
    n_blocks = T // ROW_BLOCK
    shift, scale, gate = _mod_rows(mod_ref, row0, row_step)

    @pl.when(pl.program_id(0) == 0)
    def _():
        def cast(rb, carry):
            rows = pl.ds(pl.multiple_of(rb * ROW_BLOCK, ROW_BLOCK), ROW_BLOCK)
            win_ref[rows, :] = win_f32_ref[rows, :].astype(BF16)
            wout_ref[rows, :] = wout_f32_ref[rows, :].astype(BF16)
            return carry

        lax.fori_loop(0, D // ROW_BLOCK, cast, 0)

    def stage_a(rb):
        rows = slice(rb * ROW_BLOCK, (rb + 1) * ROW_BLOCK)
        st = {}

        def t_p():
            st["hb"] = (x_ref[rows, :] * (1.0 + scale) + shift).astype(BF16)
            p_ref[rows, :] = _dot(st["hb"], win_ref[:, 2 * A_WIDTH:])

        def t_u():
            u_ref[rows, :] = _gelu(_dot(st["hb"], win_ref[:, 0:A_WIDTH]))

        def t_v():
            v = _gelu(_dot(st["hb"], win_ref[:, A_WIDTH:2 * A_WIDTH]))
            mu = jnp.mean(v, axis=-1, keepdims=True)
            c = v - mu
            var = jnp.mean(c * c, axis=-1, keepdims=True)
            vn_ref[rows, :] = (c * lax.rsqrt(var + LN_EPS)).astype(BF16)

        return [t_p, t_u, t_v]

    def stage_b(rb):
        def t_gate(n):
            def run():
                rows = slice(n * CHUNK, (n + 1) * CHUNK)
                for h in range(A_HEADS):
                    cols = slice(h * A_GROUP_W, (h + 1) * A_GROUP_W)
                    s = _dot(ws_ref[h].astype(BF16), vn_ref[rows, cols]) + bst_ref[:, h:h + 1]
                    ycat_ref[rows, cols] = (u_ref[rows, cols] * s).astype(BF16)
            return run

        def t_pool(gi):
            def run():
                cols = slice(gi * B_GROUP, (gi + 1) * B_GROUP)
                pooled = jnp.concatenate(
                    [_pool_block(p_ref, gi, rb * ROW_BLOCK + rs * EVEN_RB, POOL_WINDOWS[gi] // 2, seq_len, T)
                     for rs in range(ROW_BLOCK // EVEN_RB)], axis=0)
                yb = _dot(pooled.astype(BF16), wpool_ref[gi].astype(BF16)) * pscale_ref[:, cols]
                ycat_ref[rb * ROW_BLOCK:(rb + 1) * ROW_BLOCK, A_WIDTH + gi * B_GROUP:A_WIDTH + (gi + 1) * B_GROUP] = (
                    yb.astype(BF16))
            return run

        chunks = range(rb * ROW_BLOCK // CHUNK, (rb + 1) * ROW_BLOCK // CHUNK)
        return [t_gate(n) for n in chunks] + [t_pool(gi) for gi in range(len(POOL_WINDOWS))]

    def out_block(rb):
        rows = slice(rb * ROW_BLOCK, (rb + 1) * ROW_BLOCK)
        d = _dot(ycat_ref[rows, :], wout_ref[...])
        r = ALPHA * x_ref[rows, :] + gate * d
        o_ref[rows, :] = _layernorm(r, lng_ref[layer:layer + 1, :], lnb_ref[layer:layer + 1, :])

    for t in stage_a(0):
        t()
    for rb in range(n_blocks):
        nxt = stage_a(rb + 1) if rb + 1 < n_blocks else []
        if nxt:
            nxt[0]()
        _interleave(stage_b(rb), nxt[1:])
        out_block(rb)


def _even_layer(x, mod, w, l, seq_len, row0, row_step, name):
    n_rows = x.shape[0]
    e = l // 2
    kern = functools.partial(_even_kernel, layer=l, seq_len=seq_len, row0=row0, row_step=row_step)
    return pl.pallas_call(
        kern,
        grid=(n_rows // TILE_ROWS,),
        in_specs=[
            pl.BlockSpec((TILE_ROWS, D), lambda i: (i, 0)),
            _mod_spec(l, 0),
            _layer_spec(w["w_in"].shape, e),
            _layer_spec(w["w_s"].shape, e),
            _const_spec(w["b_s_t"].shape),
            _layer_spec(w["w_pool"].shape, e),
            _const_spec(w["pool_scale"].shape),
            _layer_spec(w["w_out"].shape, e),
            _const_spec(w["ln_g"].shape),
            _const_spec(w["ln_b"].shape),
        ],
        out_specs=pl.BlockSpec((TILE_ROWS, D), lambda i: (i, 0)),
        out_shape=jax.ShapeDtypeStruct((n_rows, D), F32),
        scratch_shapes=[
            pltpu.VMEM((TILE_ROWS, A_WIDTH), F32),
            pltpu.VMEM((TILE_ROWS, A_WIDTH), BF16),
            pltpu.VMEM((TILE_ROWS, B_WIDTH), F32),
            pltpu.VMEM((TILE_ROWS, D), BF16),
            pltpu.VMEM((D, 2 * A_WIDTH + B_WIDTH), BF16),
            pltpu.VMEM((D, D), BF16),
        ],
        compiler_params=pltpu.CompilerParams(
            dimension_semantics=("arbitrary",), vmem_limit_bytes=VMEM_LIMIT),
        name=name,
    )(x, mod, w["w_in"], w["w_s"], w["b_s_t"], w["w_pool"], w["pool_scale"], w["w_out"],
      w["ln_g"], w["ln_b"])


def _conv3_block(z_ref, r0, lanes, cw, cb, seq_len):
    nv = FFN_RB // SUBLANES
    z3 = z_ref[r0:r0 + FFN_RB, lanes].reshape(nv, SUBLANES, LANES)
    sub = lax.broadcasted_iota(jnp.int32, (nv, SUBLANES, LANES), 1)
    down = pltpu.roll(z3, 1, 1)
    up = pltpu.roll(z3, SUBLANES - 1, 1)
    if r0 % seq_len == 0:
        prev = jnp.zeros((1, SUBLANES, LANES), F32)
    else:
        prev = pltpu.roll(z_ref[r0 - SUBLANES:r0, lanes], 1, 0)[None]
    if (r0 + FFN_RB) % seq_len == 0:
        nxt = jnp.zeros((1, SUBLANES, LANES), F32)
    else:
        nxt = pltpu.roll(z_ref[r0 + FFN_RB:r0 + FFN_RB + SUBLANES, lanes], SUBLANES - 1, 0)[None]
    zm = jnp.where(sub == 0, jnp.concatenate([prev, down[:-1]], axis=0), down)
    zp = jnp.where(sub == SUBLANES - 1, jnp.concatenate([up[1:], nxt], axis=0), up)
    y = cw[0][:, lanes] * zm + cw[1][:, lanes] * z3 + cw[2][:, lanes] * zp + cb[:, lanes]
    return y.reshape(FFN_RB, LANES)


def _ffn_kernel(x_ref, mod_ref, wup_ref, cw_ref, cb_ref, wd_ref, lng_ref, lnb_ref, o_ref,
                hb_ref, zg0_ref, zv0_ref, zg1_ref, zv1_ref, a_ref, *, layer, seq_len, row0, row_step):
    T = x_ref.shape[0]
    n_chunks = wd_ref.shape[0]
    shift, scale, gate = _mod_rows(mod_ref, row0, row_step)
    z_bufs = ((zg0_ref, zv0_ref), (zg1_ref, zv1_ref))

    def row_block(rb):
        return pl.ds(pl.multiple_of(rb * ROW_BLOCK, ROW_BLOCK), ROW_BLOCK)

    def modulate(rb, carry):
        rows = row_block(rb)
        hb_ref[rows, :] = (x_ref[rows, :] * (1.0 + scale) + shift).astype(BF16)
        return carry

    lax.fori_loop(0, T // ROW_BLOCK, modulate, 0)

    n_parts = T // FFN_DOT_ROWS
    blocks_per_part = FFN_DOT_ROWS // FFN_RB

    def up_proj(c, bufs, q):
        rows = slice(q * FFN_DOT_ROWS, (q + 1) * FFN_DOT_ROWS)
        bufs[0][rows, :] = _dot(hb_ref[rows, :], wup_ref[c])
        bufs[1][rows, :] = _dot(hb_ref[rows, :], wup_ref[c + n_chunks])

    def gate_blocks(c, bufs, q):
        cwg = [cw_ref[k, pl.ds(c, 1), :] for k in range(3)]
        cwv = [cw_ref[k, pl.ds(c + n_chunks, 1), :] for k in range(3)]
        cbg, cbv = cb_ref[pl.ds(c, 1), :], cb_ref[pl.ds(c + n_chunks, 1), :]
        for rb in range(q * blocks_per_part, (q + 1) * blocks_per_part):
            r0 = rb * FFN_RB
            for lt in range(FF_CHUNK // LANES):
                lanes = slice(lt * LANES, (lt + 1) * LANES)
                g = _conv3_block(bufs[0], r0, lanes, cwg, cbg, seq_len)
                v = _conv3_block(bufs[1], r0, lanes, cwv, cbv, seq_len)
                a_ref[c, r0:r0 + FFN_RB, lanes] = (g * jax.nn.sigmoid(g) * v).astype(BF16)

    for q in range(n_parts):
        up_proj(0, z_bufs[0], q)

    def chunk_pair(k, carry):
        c = 2 * k
        for q in range(n_parts):
            up_proj(c + 1, z_bufs[1], q)
            gate_blocks(c, z_bufs[0], q)
        for q in range(n_parts):
            up_proj(c + 2, z_bufs[0], q)
            gate_blocks(c + 1, z_bufs[1], q)
        return carry

    assert n_chunks % 2 == 1
    lax.fori_loop(0, n_chunks // 2, chunk_pair, 0)
    for q in range(n_parts):
        gate_blocks(n_chunks - 1, z_bufs[0], q)

    def out(rb, carry):
        rows = row_block(rb)
        acc = _dot(a_ref[0, rows, :], wd_ref[0])
        for c in range(1, n_chunks):
            acc = acc + _dot(a_ref[c, rows, :], wd_ref[c])
        r = ALPHA * x_ref[rows, :] + gate * acc
        o_ref[rows, :] = _layernorm(r, lng_ref[layer:layer + 1, :], lnb_ref[layer:layer + 1, :])
        return carry

    lax.fori_loop(0, T // ROW_BLOCK, out, 0)


def _ffn_layer(x, mod, w, l, seq_len, row0, row_step, name):
    n_rows = x.shape[0]
    n_chunks = D_FF // FF_CHUNK
    kern = functools.partial(_ffn_kernel, layer=l, seq_len=seq_len, row0=row0, row_step=row_step)
    params = [w["w_up"], w["conv_w"], w["conv_b"], w["w_down"]]
    return pl.pallas_call(
        kern,
        grid=(n_rows // TILE_ROWS,),
        in_specs=[pl.BlockSpec((TILE_ROWS, D), lambda i: (i, 0)), _mod_spec(l, 1)]
        + [_layer_spec(a.shape, l) for a in params]
        + [_const_spec(w["ln_g"].shape), _const_spec(w["ln_b"].shape)],
        out_specs=pl.BlockSpec((TILE_ROWS, D), lambda i: (i, 0)),
        out_shape=jax.ShapeDtypeStruct((n_rows, D), F32),
        scratch_shapes=[
            pltpu.VMEM((TILE_ROWS, D), BF16),
            pltpu.VMEM((TILE_ROWS, FF_CHUNK), F32),
            pltpu.VMEM((TILE_ROWS, FF_CHUNK), F32),
            pltpu.VMEM((TILE_ROWS, FF_CHUNK), F32),
            pltpu.VMEM((TILE_ROWS, FF_CHUNK), F32),
            pltpu.VMEM((n_chunks, TILE_ROWS, FF_CHUNK), BF16),
        ],
        compiler_params=pltpu.CompilerParams(
            dimension_semantics=("arbitrary",), vmem_limit_bytes=VMEM_LIMIT),
        name=name,
    )(x, mod, *params, w["ln_g"], w["ln_b"])


def _prep_up_kernel(w_ref, o_ref):
    for k in range(o_ref.shape[0]):
        o_ref[k] = w_ref[:, k * FF_CHUNK:(k + 1) * FF_CHUNK].astype(BF16)


def _prep_down_kernel(w_ref, o_ref):
    o_ref[...] = w_ref[...].astype(BF16)


def _prep_ffn_weights(w_up, w_down):
    n_ch = 2 * D_FF // FF_CHUNK
    up = pl.pallas_call(
        _prep_up_kernel,
        grid=(DEPTH, 2),
        in_specs=[pl.BlockSpec((None, D, D_FF), lambda l, j: (l, 0, j))],
        out_specs=pl.BlockSpec((None, n_ch // 2, D, FF_CHUNK), lambda l, j: (l, j, 0, 0)),
        out_shape=jax.ShapeDtypeStruct((DEPTH, n_ch, D, FF_CHUNK), BF16),
        compiler_params=pltpu.CompilerParams(
            dimension_semantics=("arbitrary", "arbitrary"), vmem_limit_bytes=VMEM_LIMIT),
        name="ffn_up_bf16",
    )(w_up)
    half = D_FF // 2
    down = pl.pallas_call(
        _prep_down_kernel,
        grid=(DEPTH, 2),
        in_specs=[pl.BlockSpec((None, half, D), lambda l, j: (l, j, 0))],
        out_specs=pl.BlockSpec((None, half, D), lambda l, j: (l, j, 0)),
        out_shape=jax.ShapeDtypeStruct((DEPTH, D_FF, D), BF16),
        compiler_params=pltpu.CompilerParams(
            dimension_semantics=("arbitrary", "arbitrary"), vmem_limit_bytes=VMEM_LIMIT),
        name="ffn_down_bf16",
    )(w_down)
    return up, down.reshape(DEPTH, n_ch // 2, FF_CHUNK, D)


def _conv4_block(xr_ref, cbk, r0, cw, cb, seq_len):
    nv = ODD_RB // SUBLANES
    z3 = xr_ref[cbk, r0:r0 + ODD_RB, :].reshape(nv, SUBLANES, LANES)
    sub = lax.broadcasted_iota(jnp.int32, (nv, SUBLANES, LANES), 1)
    down = pltpu.roll(z3, 1, 1)
    up1 = pltpu.roll(z3, SUBLANES - 1, 1)
    up2 = pltpu.roll(z3, SUBLANES - 2, 1)
    zero = jnp.zeros((1, SUBLANES, LANES), F32)
    if r0 % seq_len == 0:
        prev = zero
    else:
        prev = pltpu.roll(xr_ref[cbk, r0 - SUBLANES:r0, :], 1, 0)[None]
    if (r0 + ODD_RB) % seq_len == 0:
        nxt1 = nxt2 = zero
    else:
        nx = xr_ref[cbk, r0 + ODD_RB:r0 + ODD_RB + SUBLANES, :]
        nxt1 = pltpu.roll(nx, SUBLANES - 1, 0)[None]
        nxt2 = pltpu.roll(nx, SUBLANES - 2, 0)[None]
    zm = jnp.where(sub == 0, jnp.concatenate([prev, down[:-1]], axis=0), down)
    zp1 = jnp.where(sub == SUBLANES - 1, jnp.concatenate([up1[1:], nxt1], axis=0), up1)
    zp2 = jnp.where(sub >= SUBLANES - 2, jnp.concatenate([up2[1:], nxt2], axis=0), up2)
    y = cw[0:1, :] * zm + cw[1:2, :] * z3 + cw[2:3, :] * zp1 + cw[3:4, :] * zp2 + cb
    return y.reshape(ODD_RB, LANES)


def _scan_slabs(a, b, reverse):
    nv = a.shape[0] // SUBLANES
    a = a.reshape(nv, SUBLANES, LANES)
    b = b.reshape(nv, SUBLANES, LANES)
    sub = lax.broadcasted_iota(jnp.int32, (nv, SUBLANES, LANES), 1)
    for d in (1, 2, 4):
        if reverse:
            valid, sh = sub < SUBLANES - d, SUBLANES - d
        else:
            valid, sh = sub >= d, d
        a_prev = jnp.where(valid, pltpu.roll(a, sh, 1), 1.0)
        b_prev = jnp.where(valid, pltpu.roll(b, sh, 1), 0.0)
        b = b + a * b_prev
        a = a * a_prev
    return a.reshape(nv * SUBLANES, LANES), b.reshape(nv * SUBLANES, LANES)


def _scan_ends(a, b, posg, n_groups, reverse):
    n = a.shape[0]
    d = 1
    while d < n_groups:
        if reverse:
            valid, sh = posg < n_groups - d, n - d
        else:
            valid, sh = posg >= d, d
        a_prev = jnp.where(valid, pltpu.roll(a, sh, 0), 1.0)
        b_prev = jnp.where(valid, pltpu.roll(b, sh, 0), 0.0)
        b = b + a * b_prev
        a = a * a_prev
        d *= 2
    return a, b


def _rope_pairs(x, cos, sin_signed, even_lane):
    n = x.shape[1]
    swapped = jnp.where(even_lane, pltpu.roll(x, n - 1, 1), pltpu.roll(x, 1, 1))
    return x * cos + swapped * sin_signed


def _rmsnorm(x, g):
    return x * lax.rsqrt(jnp.mean(x * x, axis=-1, keepdims=True) + RMS_EPS) * g


def _odd_kernel(*refs, layer, seq_len, latent, past_len, row0, row_step):
    n_in = 21 if latent else 16
    n_out = 1 if latent else 4
    (x_ref, mod_ref, win_ref, cw_ref, cb_ref, wgate_ref, bgate_ref, lam_ref, qg_ref, kvg_ref,
     wuq_ref, wuk_ref, wuv_ref, wout_ref, lng_ref, lnb_ref) = refs[:16]
    if latent:
        cckv_ref, ckr_ref, st_ref, cos_ref, sin_ref = refs[16:21]
        (o_ref,) = refs[n_in:n_in + n_out]
    else:
        o_ref, ckv_out_ref, kr_out_ref, st_out_ref = refs[n_in:n_in + n_out]
    (xr_ref, gr_ref, xc_ref, pre_ref, a_ref, b_ref, hrep_ref, q_ref, k_ref, v_ref, yd_ref) = refs[n_in + n_out:]

    T = x_ref.shape[0]
    n_blocks = T // ROW_BLOCK
    n_tiles = W_C // LANES
    half_w = W_C // 2
    shift, scale, gate = _mod_rows(mod_ref, row0, row_step)
    o_q = 2 * W_C
    o_kv = o_q + Q_LORA
    o_kr = o_kv + KV_LORA
    pair_w = 2 * HEAD_PAD
    even_lane = (lax.broadcasted_iota(jnp.int32, (ROW_BLOCK, HEAD_PAD), 1) & 1) == 0
    softplus = []
    for direction in range(2):
        neg = -lam_ref[direction:direction + 1, :]
        softplus.append(jnp.maximum(neg, 0.0) + jnp.log1p(jnp.exp(-jnp.abs(neg))))

    def stage_a(rb):
        rows = slice(rb * ROW_BLOCK, (rb + 1) * ROW_BLOCK)
        k_rows = slice(past_len + rb * ROW_BLOCK, past_len + (rb + 1) * ROW_BLOCK)
        st = {}

        def t_xr():
            st["hb"] = (x_ref[rows, :] * (1.0 + scale) + shift).astype(BF16)
            z = _dot(st["hb"], win_ref[:, 0:W_C])
            for cbk in range(n_tiles):
                xr_ref[cbk, rows, :] = z[:, cbk * LANES:(cbk + 1) * LANES]

        def t_gr():
            gr_ref[rows, :] = _dot(st["hb"], win_ref[:, W_C:o_q])

        def t_lat():
            hb = st["hb"]
            st["qn"] = _rmsnorm(_dot(hb, win_ref[:, o_q:o_kv]), qg_ref[...]).astype(BF16)
            ckv = _rmsnorm(_dot(hb, win_ref[:, o_kv:o_kr]), kvg_ref[...])
            kr = _dot(hb, win_ref[:, o_kr:o_kr + HEAD_PAD])
            if latent:
                st["cos"] = cos_ref[rows, :]
                st["sin"] = sin_ref[rows, :]
                kr = _rope_pairs(kr, st["cos"], st["sin"], even_lane)
            else:
                ckv_out_ref[rows, :] = ckv
                kr_out_ref[rows, :] = kr[:, KR_LANE:KR_LANE + QK_ROPE]
            st["kr"] = kr
            st["ckvb"] = ckv.astype(BF16)

        def t_v():
            v_ref[k_rows, :] = _dot(st["ckvb"], wuv_ref[...]).astype(BF16)

        def t_pair(pair):
            def run():
                pcols = slice(pair * pair_w, (pair + 1) * pair_w)
                q2 = _dot(st["qn"], wuq_ref[:, pcols])
                k2 = _dot(st["ckvb"], wuk_ref[:, pcols])
                for hh in range(2):
                    cols = slice(pair * pair_w + hh * HEAD_PAD, pair * pair_w + (hh + 1) * HEAD_PAD)
                    qh = q2[:, hh * HEAD_PAD:(hh + 1) * HEAD_PAD]
                    if latent:
                        qh = _rope_pairs(qh, st["cos"], st["sin"], even_lane)
                    q_ref[rows, cols] = qh.astype(BF16)
                    k_ref[k_rows, cols] = (k2[:, hh * HEAD_PAD:(hh + 1) * HEAD_PAD] + st["kr"]).astype(BF16)
            return run

        return [t_xr, t_gr, t_lat, t_v] + [t_pair(p) for p in range(D_HEADS // 2)]

    def conv_thunks(rb):
        par = rb % xc_ref.shape[0]

        def t_conv(cbk):
            def run():
                cw = cw_ref[:, cbk * LANES:(cbk + 1) * LANES]
                cb = cb_ref[:, cbk * LANES:(cbk + 1) * LANES]
                for rs in range(ROW_BLOCK // ODD_RB):
                    r0 = rb * ROW_BLOCK + rs * ODD_RB
                    xc_ref[par, rs * ODD_RB:(rs + 1) * ODD_RB, cbk * LANES:(cbk + 1) * LANES] = _conv4_block(
                        xr_ref, cbk, r0, cw, cb, seq_len)
            return run

        return [t_conv(cbk) for cbk in range(n_tiles)]

    def gate_thunks(rb):
        par = rb % xc_ref.shape[0]
        thunks = []
        for direction in range(2):
            for j in range(2):
                slot = (2 * direction + j) % pre_ref.shape[0]

                def t_dot(direction=direction, j=j, slot=slot):
                    xcb = xc_ref[par, :, j * half_w:(j + 1) * half_w].astype(BF16)
                    pre_ref[slot, 0] = _dot(xcb, wgate_ref[direction, 0, j])
                    pre_ref[slot, 1] = _dot(xcb, wgate_ref[direction, 1, j])

                def t_post(direction=direction, j=j, slot=slot):
                    for rs in range(ROW_BLOCK // ODD_RB):
                        lrows = slice(rs * ODD_RB, (rs + 1) * ODD_RB)
                        grows = slice(rb * ROW_BLOCK + rs * ODD_RB, rb * ROW_BLOCK + (rs + 1) * ODD_RB)
                        for lt in range(half_w // LANES):
                            cbk = j * (half_w // LANES) + lt
                            lanes = slice(lt * LANES, (lt + 1) * LANES)
                            glanes = slice(cbk * LANES, (cbk + 1) * LANES)
                            xcv = xc_ref[par, lrows, glanes]
                            r = jax.nn.sigmoid(pre_ref[slot, 0, lrows, lanes] + bgate_ref[direction, 0:1, glanes])
                            i = jax.nn.sigmoid(pre_ref[slot, 1, lrows, lanes] + bgate_ref[direction, 1:2, glanes])
                            log_a = -RG_C * r * softplus[direction][:, glanes]
                            a = jnp.exp(log_a)
                            b = jnp.sqrt(jnp.tanh(-log_a) * (a * a + 1.0)) * i * xcv
                            a, b = _scan_slabs(a, b, direction == 1)
                            a_ref[direction, cbk, grows, :] = a
                            b_ref[direction, cbk, grows, :] = b

                thunks += [t_dot, t_post]
        return thunks

    def combine(direction, cbk, t0, n_rows, h0):
        reverse = direction == 1
        n_ends = n_rows // SUBLANES
        end_row = 0 if reverse else SUBLANES - 1
        posg = lax.broadcasted_iota(jnp.int32, (n_ends, LANES), 0)
        a_end = a_ref[direction, cbk, pl.ds(t0 + end_row, n_ends, stride=SUBLANES), :]
        b_end = b_ref[direction, cbk, pl.ds(t0 + end_row, n_ends, stride=SUBLANES), :]
        a_tot, b_tot = _scan_ends(a_end, b_end, posg, n_ends, reverse)
        ends = b_tot if h0 is None else a_tot * h0 + b_tot
        first = jnp.zeros((1, LANES), F32) if h0 is None else h0
        if reverse:
            carry_in = jnp.where(posg < n_ends - 1, pltpu.roll(ends, n_ends - 1, 0), first)
        else:
            carry_in = jnp.where(posg >= 1, pltpu.roll(ends, 1, 0), first)
        for rr in range(SUBLANES):
            hrep_ref[pl.ds(t0 + rr, n_ends, stride=SUBLANES), :] = carry_in
        for rs in range(n_rows // ODD_RB):
            rows = slice(t0 + rs * ODD_RB, t0 + (rs + 1) * ODD_RB)
            h = a_ref[direction, cbk, rows, :] * hrep_ref[rows, :] + b_ref[direction, cbk, rows, :]
            if direction == 0:
                xr_ref[cbk, rows, :] = h
            else:
                xr_ref[cbk, rows, :] = xr_ref[cbk, rows, :] + h
        return ends

    def attend_pair(q_rows, k_rows, pair):
        o_pair = None
        for h in (2 * pair, 2 * pair + 1):
            cols = slice(h * HEAD_PAD, (h + 1) * HEAD_PAD)
            s = lax.dot_general(q_ref[q_rows, cols], k_ref[k_rows, cols],
                                (((1,), (1,)), ((), ())), preferred_element_type=F32) * ATTN_SCALE
            m = jnp.max(s, axis=-1, keepdims=True)
            p = jnp.exp(s - m)
            l = jnp.sum(p, axis=-1, keepdims=True)
            o = _dot(p.astype(BF16), v_ref[k_rows, cols]) / l
            o_pair = o if o_pair is None else o_pair + o
        yd_ref[q_rows, pair * HEAD_PAD:(pair + 1) * HEAD_PAD] = o_pair.astype(BF16)

    def out_block(rows):
        yc = jnp.concatenate(
            [xr_ref[cbk, rows, :] * _gelu(gr_ref[rows, cbk * LANES:(cbk + 1) * LANES]) for cbk in range(n_tiles)],
            axis=1).astype(BF16)
        d = _dot(yc, wout_ref[0:W_C, :]) + _dot(yd_ref[rows, :], wout_ref[W_C:, :])
        r = ALPHA * x_ref[rows, :] + gate * d
        o_ref[rows, :] = _layernorm(r, lng_ref[layer:layer + 1, :], lnb_ref[layer:layer + 1, :])

    if not latent:
        def stage_b(s):
            t0 = s * seq_len
            rows = slice(t0, t0 + seq_len)
            thunks = conv_thunks(s) + gate_thunks(s)
            for direction in range(2):
                for cbk in range(n_tiles):
                    def t_comb(direction=direction, cbk=cbk):
                        ends = combine(direction, cbk, t0, seq_len, None)
                        last = 0 if direction == 1 else seq_len // SUBLANES - 1
                        st_out_ref[s, direction:direction + 1, cbk * LANES:(cbk + 1) * LANES] = ends[last:last + 1, :]
                    thunks.append(t_comb)
            return thunks

        assert seq_len == ROW_BLOCK
        for t in stage_a(0):
            t()
        for s in range(n_blocks):
            rows = slice(s * seq_len, (s + 1) * seq_len)
            attn = [functools.partial(attend_pair, rows, rows, pair) for pair in range(D_HEADS // 2)]
            nxt = stage_a(s + 1) if s + 1 < n_blocks else []
            mxu = []
            _interleave([functools.partial(mxu.append, t) for t in nxt] or [lambda: None],
                        [functools.partial(mxu.append, t) for t in attn])
            _interleave(stage_b(s), mxu)
            out_block(rows)
        return

    assert seq_len == T
    for t in stage_a(0):
        t()
    for rb in range(n_blocks):
        nxt = stage_a(rb + 1) if rb + 1 < n_blocks else []
        if nxt:
            nxt[0]()
        _interleave(conv_thunks(rb) + gate_thunks(rb), nxt[1:])

    def cached(cbk, carry):
        rows = pl.ds(pl.multiple_of(cbk * ROW_BLOCK, ROW_BLOCK), ROW_BLOCK)
        cckv = cckv_ref[rows, :].astype(BF16)
        ckr = ckr_ref[rows, :]
        v_ref[rows, :] = _dot(cckv, wuv_ref[...]).astype(BF16)
        for pair in range(D_HEADS // 2):
            pcols = slice(pair * pair_w, (pair + 1) * pair_w)
            k2 = _dot(cckv, wuk_ref[:, pcols])
            for hh in range(2):
                cols = slice(pair * pair_w + hh * HEAD_PAD, pair * pair_w + (hh + 1) * HEAD_PAD)
                k_ref[rows, cols] = (k2[:, hh * HEAD_PAD:(hh + 1) * HEAD_PAD] + ckr).astype(BF16)
        return carry

    lax.fori_loop(0, past_len // ROW_BLOCK, cached, 0)

    n_keys = past_len + seq_len

    def tail(qb, carry):
        q_rows = pl.ds(pl.multiple_of(qb * ROW_BLOCK, ROW_BLOCK), ROW_BLOCK)
        k_rows = slice(0, n_keys)
        rg = []
        for direction in range(2):
            def t_comb(direction=direction):
                h0 = st_ref[direction, pl.ds(qb, 1), :]
                combine(direction, qb, 0, T, h0)
            rg.append(t_comb)
        _interleave([functools.partial(attend_pair, q_rows, k_rows, pair) for pair in range(D_HEADS // 2)], rg)
        return carry

    assert n_tiles == n_blocks
    lax.fori_loop(0, n_blocks, tail, 0)

    def out(rb, carry):
        out_block(pl.ds(pl.multiple_of(rb * ROW_BLOCK, ROW_BLOCK), ROW_BLOCK))
        return carry

    lax.fori_loop(0, n_blocks, out, 0)


def _odd_layer(x, mod, w, l, seq_len, row0, row_step, name, latent_inputs=None):
    n_rows = x.shape[0]
    n_tiles = n_rows // TILE_ROWS
    latent = latent_inputs is not None
    past_len = latent_inputs["cache_ckv"].shape[1] if latent else 0
    n_seq = TILE_ROWS // seq_len
    kern = functools.partial(_odd_kernel, layer=l, seq_len=seq_len, latent=latent, past_len=past_len,
                             row0=row0, row_step=row_step)
    weights = [w["w_in"], w["conv_w"], w["conv_b"], w["w_gate"], w["b_gate"], w["lam"], w["q_g"], w["kv_g"],
               w["w_uq"], w["w_uk"], w["w_uv"], w["w_out"]]
    out_mode = pl.Buffered(1) if latent else pl.Buffered(2)
    in_specs = [pl.BlockSpec((TILE_ROWS, D), lambda i: (i, 0)),
                _mod_spec(l, 0)]
    in_specs += [_const_spec(a.shape) for a in weights]
    in_specs += [_const_spec(w["ln_g"].shape), _const_spec(w["ln_b"].shape)]
    args = [x, mod] + weights + [w["ln_g"], w["ln_b"]]
    out_specs = [pl.BlockSpec((TILE_ROWS, D), lambda i: (i, 0), pipeline_mode=out_mode)]
    out_shape = [jax.ShapeDtypeStruct((n_rows, D), F32)]
    if latent:
        li = latent_inputs
        in_specs += [
            pl.BlockSpec((None, past_len, KV_LORA), lambda i: (i, 0, 0)),
            pl.BlockSpec((None, past_len, HEAD_PAD), lambda i: (i, 0, 0)),
            pl.BlockSpec((None, 2, W_C // LANES, LANES), lambda i: (i, 0, 0, 0)),
            _const_spec(li["cos"].shape),
            _const_spec(li["sin"].shape),
        ]
        args += [li["cache_ckv"], li["cache_kr"], li["state"], li["cos"], li["sin"]]
    else:
        out_specs += [
            pl.BlockSpec((TILE_ROWS, KV_LORA), lambda i: (i, 0)),
            pl.BlockSpec((TILE_ROWS, QK_ROPE), lambda i: (i, 0)),
            pl.BlockSpec((n_seq, 2, W_C), lambda i: (i, 0, 0)),
        ]
        out_shape += [
            jax.ShapeDtypeStruct((n_rows, KV_LORA), F32),
            jax.ShapeDtypeStruct((n_rows, QK_ROPE), F32),
            jax.ShapeDtypeStruct((n_rows // seq_len, 2, W_C), F32),
        ]
    n_keys_buf = past_len + TILE_ROWS
    n_slots = 1 if latent else 2
    scratch = [
        pltpu.VMEM((W_C // LANES, TILE_ROWS, LANES), F32),
        pltpu.VMEM((TILE_ROWS, W_C), F32),
        pltpu.VMEM((n_slots, ROW_BLOCK, W_C), F32),
        pltpu.VMEM((n_slots, 2, ROW_BLOCK, W_C // 2), F32),
        pltpu.VMEM((2, W_C // LANES, TILE_ROWS, LANES), F32),
        pltpu.VMEM((2, W_C // LANES, TILE_ROWS, LANES), F32),
        pltpu.VMEM((TILE_ROWS, LANES), F32),
        pltpu.VMEM((TILE_ROWS, D_HEADS * HEAD_PAD), BF16),
        pltpu.VMEM((n_keys_buf, D_HEADS * HEAD_PAD), BF16),
        pltpu.VMEM((n_keys_buf, D_HEADS * HEAD_PAD), BF16),
        pltpu.VMEM((TILE_ROWS, W_C), BF16),
    ]
    return pl.pallas_call(
        kern,
        grid=(n_tiles,),
        in_specs=in_specs,
        out_specs=out_specs,
        out_shape=out_shape,
        scratch_shapes=scratch,
        compiler_params=pltpu.CompilerParams(
            dimension_semantics=("arbitrary",), vmem_limit_bytes=VMEM_LIMIT),
        name=name,
    )(*args)


def _axial_rope_tables(rows):
    half = QK_ROPE // 2
    inv = (ROPE_BASE ** (-np.arange(0, half, 2, dtype=np.float32) / half)).astype(np.float32)
    r = np.repeat(np.arange(rows, dtype=np.float32), GRID_W)
    col = np.tile(np.arange(GRID_W, dtype=np.float32), rows)
    ang = np.concatenate([r[:, None] * inv, col[:, None] * inv], axis=-1).astype(np.float32)
    cos = np.repeat(np.cos(ang), 2, axis=-1)
    sin = np.repeat(np.sin(ang), 2, axis=-1) * np.tile(np.array([-1.0, 1.0], np.float32), half)
    n = ang.shape[0]
    cos_t = np.ones((n, HEAD_PAD), np.float32)
    sin_t = np.zeros((n, HEAD_PAD), np.float32)
    cos_t[:, KR_LANE:KR_LANE + QK_ROPE] = cos
    sin_t[:, KR_LANE:KR_LANE + QK_ROPE] = sin
    return jnp.asarray(cos_t), jnp.asarray(sin_t)


def _odd_prep_kernel(win_ref, wa_ref, wx_ref, wuq_ref, wuk_ref, wuv_ref, wout_ref,
                     win_o, gate_o, wuq_o, wuk_o, wuv_o, wout_o):
    n_main = 2 * W_C + Q_LORA + KV_LORA
    win_o[:, 0:n_main] = win_ref[:, 0:n_main].astype(BF16)
    win_o[:, n_main:] = jnp.zeros((win_o.shape[0], HEAD_PAD), BF16)
    win_o[:, n_main + KR_LANE:n_main + KR_LANE + QK_ROPE] = win_ref[:, n_main:].astype(BF16)
    wout_o[...] = wout_ref[...].astype(BF16)

    @pl.when(pl.program_id(0) == 0)
    def _():
        wuq_o[...] = jnp.zeros_like(wuq_o)
        wuk_o[...] = jnp.zeros_like(wuk_o)
        wuv_o[...] = jnp.zeros_like(wuv_o)
        for h in range(D_HEADS):
            wuq_o[:, h * HEAD_PAD:h * HEAD_PAD + QK_NOPE + QK_ROPE] = wuq_ref[:, h, :].astype(BF16)
            wuk_o[:, h * HEAD_PAD:h * HEAD_PAD + QK_NOPE] = wuk_ref[:, h, :].astype(BF16)
            v0 = h * HEAD_PAD + (h % 2) * V_DIM
            wuv_o[:, v0:v0 + V_DIM] = wuv_ref[:, h, :].astype(BF16)
        gate_o[...] = jnp.zeros_like(gate_o)
        per = (W_C // 2) // C_BLOCK
        for direction in range(2):
            for g, w_ref in enumerate((wa_ref, wx_ref)):
                for h in range(C_HEADS):
                    j, hh = divmod(h, per)
                    blk = slice(hh * C_BLOCK, (hh + 1) * C_BLOCK)
                    gate_o[direction, g, j, blk, blk] = w_ref[direction, h].astype(BF16)


def _prep_odd_weights(w_in, w_a, w_x, w_uq, w_uk, w_uv, w_out):
    small_in = [w_a, w_x, w_uq, w_uk, w_uv]
    small_out = [
        jax.ShapeDtypeStruct((2, 2, 2, W_C // 2, W_C // 2), BF16),
        jax.ShapeDtypeStruct((Q_LORA, D_HEADS * HEAD_PAD), BF16),
        jax.ShapeDtypeStruct((KV_LORA, D_HEADS * HEAD_PAD), BF16),
        jax.ShapeDtypeStruct((KV_LORA, D_HEADS * HEAD_PAD), BF16),
    ]

    def whole(shape):
        nd = len(shape)
        return pl.BlockSpec(tuple(shape), lambda i: (0,) * nd)

    def slab(cols):
        return pl.BlockSpec((ROW_BLOCK, cols), lambda i: (i, 0))

    return pl.pallas_call(
        _odd_prep_kernel,
        grid=(D // ROW_BLOCK,),
        in_specs=[slab(w_in.shape[1])] + [whole(a.shape) for a in small_in] + [slab(D)],
        out_specs=[slab(ODD_IN_PAD)] + [whole(o.shape) for o in small_out] + [slab(D)],
        out_shape=[jax.ShapeDtypeStruct((D, ODD_IN_PAD), BF16)] + small_out + [jax.ShapeDtypeStruct((D, D), BF16)],
        compiler_params=pltpu.CompilerParams(
            dimension_semantics=("arbitrary",), vmem_limit_bytes=VMEM_LIMIT),
        name="odd_weights_bf16",
    )(w_in, *small_in, w_out)


def kernel(x_prompt, x_sample, cache_mla_ckv, cache_mla_krope, state_rglru, c, c_ctx, w_mod, b_mod, ln1_g, ln1_b, ln2_g, ln2_b, even_w_in, even_w_s, even_b_s, even_w_pool, even_pool_scale, even_w_out, odd_w_in, rg_conv_w, rg_conv_b, rg_w_a, rg_b_a, rg_w_x, rg_b_x, rg_lam, mla_q_g, mla_kv_g, mla_w_uq, mla_w_uk, mla_w_uv, odd_w_out, ffn_w_up, ffn_conv_w, ffn_conv_b, ffn_w_down):
    batch, seq, _ = x_prompt.shape
    dec_batch, dec_seq, _ = x_sample.shape
    assert TILE_ROWS % seq == 0 and dec_seq == TILE_ROWS and dec_batch + 1 <= SUBLANES

    mod = _modulation(c_ctx, c, w_mod, b_mod)

    wup_bf, wdown_bf = _prep_ffn_weights(ffn_w_up, ffn_w_down)
    n_ch = 2 * D_FF // FF_CHUNK
    wf = {"ln_g": ln2_g, "ln_b": ln2_b,
          "w_up": wup_bf, "conv_w": ffn_conv_w.reshape(DEPTH, 3, n_ch, FF_CHUNK),
          "conv_b": ffn_conv_b.reshape(DEPTH, n_ch, FF_CHUNK), "w_down": wdown_bf}
    ln1 = {"ln_g": ln1_g, "ln_b": ln1_b}

    xp = x_prompt.reshape(batch * seq, D)
    xs = x_sample.reshape(dec_batch * dec_seq, D)
    new_ckv = new_kr = new_state = None
    for l in range(DEPTH):
        if l % 2 == 0:
            e = l // 2
            w = dict(ln1, w_in=even_w_in, w_s=even_w_s, b_s_t=even_b_s[e].T, w_pool=even_w_pool,
                     pool_scale=even_pool_scale[e][None, :], w_out=even_w_out)
            xp = _even_layer(xp, mod, w, l, seq, 0, 0, f"even{l}_ctx")
            xs = _even_layer(xs, mod, w, l, dec_seq, 1, 1, f"even{l}_lat")
        else:
            o = l // 2
            w_in_bf, w_gate_bf, w_uq_bf, w_uk_bf, w_uv_bf, w_out_bf = _prep_odd_weights(
                odd_w_in[o], rg_w_a[o], rg_w_x[o], mla_w_uq[o], mla_w_uk[o], mla_w_uv[o], odd_w_out[o])
            b_gate = jnp.stack([rg_b_a[o], rg_b_x[o]], axis=1)
            w = dict(ln1,
                     w_in=w_in_bf, conv_w=rg_conv_w[o], conv_b=rg_conv_b[o][None, :],
                     w_gate=w_gate_bf, b_gate=b_gate, lam=rg_lam[o],
                     q_g=mla_q_g[o][None, :], kv_g=mla_kv_g[o][None, :],
                     w_uq=w_uq_bf, w_uk=w_uk_bf, w_uv=w_uv_bf, w_out=w_out_bf)
            xp, ckv, kr, st = _odd_layer(xp, mod, w, l, seq, 0, 0, f"odd{l}_ctx")
            new_ckv = ckv.reshape(batch, 1, seq, KV_LORA)
            new_kr = kr.reshape(batch, 1, seq, QK_ROPE)
            new_state = st.reshape(batch, 1, 2, W_C)
            cos_t, sin_t = _axial_rope_tables(dec_seq // GRID_W)
            latent_inputs = {
                "cache_ckv": cache_mla_ckv[:, o],
                "cache_kr": jnp.pad(cache_mla_krope[:, o],
                                    ((0, 0), (0, 0), (KR_LANE, HEAD_PAD - KR_LANE - QK_ROPE))),
                "state": state_rglru[:, o].reshape(dec_batch, 2, W_C // LANES, LANES),
                "cos": cos_t, "sin": sin_t,
            }
            (xs,) = _odd_layer(xs, mod, w, l, dec_seq, 1, 1, f"odd{l}_lat", latent_inputs)
        xp = _ffn_layer(xp, mod, wf, l, seq, 0, 0, f"ffn{l}_ctx")
        xs = _ffn_layer(xs, mod, wf, l, dec_seq, 1, 1, f"ffn{l}_lat")
    return (xp.reshape(batch, seq, D), xs.reshape(dec_batch, dec_seq, D), new_ckv, new_kr, new_state)
```

```python
import functools
import math

import jax
import jax.numpy as jnp
import numpy as np
from jax import lax
from jax.experimental import pallas as pl
from jax.experimental.pallas import tpu as pltpu

F32 = jnp.float32
BF16 = jnp.bfloat16

D = 1024
DEPTH = 2
GRID_W = 64
ALPHA = (2 * DEPTH) ** 0.25
LN_EPS = 1e-5
RMS_EPS = 1e-6
A_HEADS = 4
A_WIDTH = 512
A_GROUP_W = A_WIDTH // A_HEADS
CHUNK = 128
POOL_WINDOWS = (2, 4, 8, 16)
B_GROUP = 128
B_WIDTH = 512
W_C = 512
C_HEADS = 8
C_BLOCK = 64
RG_C = 8.0
D_HEADS = 8
Q_LORA = 384
KV_LORA = 256
QK_NOPE = 64
QK_ROPE = 32
V_DIM = 64
ROPE_BASE = 10000.0
ATTN_SCALE = 1.0 / math.sqrt(QK_NOPE + QK_ROPE)
D_FF = 2816

LANES = 128
SUBLANES = 8
HEAD_PAD = LANES
ODD_IN_PAD = 2 * W_C + Q_LORA + KV_LORA + HEAD_PAD
KR_LANE = QK_NOPE
TILE_ROWS = 1024
ROW_BLOCK = 256
FF_CHUNK = 256
FFN_DOT_ROWS = 256
FFN_RB = 64
EVEN_RB = 64
ODD_RB = 64
VMEM_LIMIT = 61 * 1024 * 1024


def _dot(a, b):
    return jnp.dot(a, b, preferred_element_type=F32)


def _gelu(x):
    return x * (0.5 * (1.0 + jnp.tanh(0.7978845608028654 * (x + 0.044715 * (x * x * x)))))


def _layernorm(r, g, b):
    mu = jnp.mean(r, axis=-1, keepdims=True)
    c = r - mu
    var = jnp.mean(c * c, axis=-1, keepdims=True)
    return c * lax.rsqrt(var + LN_EPS) * g + b


def _shift_rows(x, d, pos, seq_len):
    n = x.shape[0]
    y = pltpu.roll(x, (-d) % n, 0)
    valid = (pos < seq_len - d) if d > 0 else (pos >= -d)
    return jnp.where(valid, y, 0.0)


def _mod_kernel(cctx_ref, c_ref, w_ref, b_ref, o_ref, s_ref):
    n_lat = c_ref.shape[0]
    s_ref[...] = jnp.zeros_like(s_ref)
    s_ref[0:1, :] = cctx_ref[...]
    s_ref[1:1 + n_lat, :] = c_ref[...]
    s = s_ref[...]
    s = s * jax.nn.sigmoid(s)
    bias = b_ref[pl.ds(pl.program_id(0), 1), :]
    o_ref[...] = _dot(s.astype(BF16), w_ref[...].astype(BF16)) + bias


def _modulation(c_ctx, c, w_mod, b_mod):
    nc = 1536
    return pl.pallas_call(
        _mod_kernel,
        grid=(DEPTH, 6 * D // nc),
        in_specs=[
            pl.BlockSpec((1, D), lambda l, j: (0, 0)),
            pl.BlockSpec(c.shape, lambda l, j: (0, 0)),
            pl.BlockSpec((None, D, nc), lambda l, j: (l, 0, j)),
            pl.BlockSpec((DEPTH, nc), lambda l, j: (0, j)),
        ],
        out_specs=pl.BlockSpec((None, SUBLANES, nc), lambda l, j: (l, 0, j)),
        out_shape=jax.ShapeDtypeStruct((DEPTH, SUBLANES, 6 * D), F32),
        scratch_shapes=[pltpu.VMEM((SUBLANES, D), F32)],
        compiler_params=pltpu.CompilerParams(
            dimension_semantics=("arbitrary", "arbitrary"), vmem_limit_bytes=VMEM_LIMIT),
        name="modulation",
    )(c_ctx.reshape(1, D), c, w_mod, b_mod)


def _mod_rows(mod_ref, row0, row_step):
    row = row0 + pl.program_id(0) * row_step
    m = mod_ref[pl.ds(row, 1), :]
    return m[:, 0:D], m[:, D:2 * D], m[:, 2 * D:3 * D]


def _const_spec(shape):
    nd = len(shape)
    return pl.BlockSpec(shape, lambda *_: (0,) * nd, pipeline_mode=pl.Buffered(1))


def _layer_spec(shape, l):
    nd = len(shape) - 1
    return pl.BlockSpec((None,) + tuple(shape[1:]), lambda *_: (l,) + (0,) * nd, pipeline_mode=pl.Buffered(1))


def _mod_spec(l, half):
    return pl.BlockSpec((None, SUBLANES, 3 * D), lambda *_: (l, 0, half))


def _interleave(primary, secondary):
    n_p, n_s = len(primary), len(secondary)
    done = 0
    for k, thunk in enumerate(primary):
        thunk()
        want = ((k + 1) * n_s) // n_p
        while done < want:
            secondary[done]()
            done += 1
    while done < n_s:
        secondary[done]()
        done += 1


def _pool_block(p_ref, gi, r0, half, seq_len, n_rows):
    cols = slice(gi * B_GROUP, (gi + 1) * B_GROUP)
    zero = jnp.zeros((SUBLANES, B_GROUP), F32)
    top = zero if r0 == 0 else p_ref[r0 - SUBLANES:r0, cols]
    bot = zero if r0 + EVEN_RB == n_rows else p_ref[r0 + EVEN_RB:r0 + EVEN_RB + SUBLANES, cols]
    pg = jnp.concatenate([top, p_ref[r0:r0 + EVEN_RB, cols], bot], axis=0)
    pos = (lax.broadcasted_iota(jnp.int32, pg.shape, 0) + (r0 - SUBLANES)) & (seq_len - 1)
    fwd = pg
    bwd = _shift_rows(pg, -1, pos, seq_len)
    k = 1
    while k < half:
        fwd = fwd + _shift_rows(fwd, k, pos, seq_len)
        bwd = bwd + _shift_rows(bwd, -k, pos, seq_len)
        k *= 2
    cnt = jnp.minimum(pos + half, seq_len) - jnp.maximum(pos - half, 0)
    pooled = (fwd + bwd) / cnt.astype(F32) - pg
    return pooled[SUBLANES:SUBLANES + EVEN_RB]


def _even_kernel(x_ref, mod_ref, win_f32_ref, ws_ref, bst_ref, wpool_ref, pscale_ref, wout_f32_ref,
                 lng_ref, lnb_ref, o_ref, u_ref, vn_ref, p_ref, ycat_ref, win_ref, wout_ref,
                 *, layer, seq_len, row0, row_step):
    T = x_ref.shape[0]
    n_blocks = T // ROW_BLOCK
    shift, scale, gate = _mod_rows(mod_ref, row0, row_step)

    @pl.when(pl.program_id(0) == 0)
    def _():
        def cast(rb, carry):
            rows = pl.ds(pl.multiple_of(rb * ROW_BLOCK, ROW_BLOCK), ROW_BLOCK)
            win_ref[rows, :] = win_f32_ref[rows, :].astype(BF16)
            wout_ref[rows, :] = wout_f32_ref[rows, :].astype(BF16)
            return carry

        lax.fori_loop(0, D // ROW_BLOCK, cast, 0)

    def stage_a(rb):
        rows = slice(rb * ROW_BLOCK, (rb + 1) * ROW_BLOCK)
        st = {}

        def t_p():
            st["hb"] = (x_ref[rows, :] * (1.0 + scale) + shift).astype(BF16)
            p_ref[rows, :] = _dot(st["hb"], win_ref[:, 2 * A_WIDTH:])

        def t_u():
            u_ref[rows, :] = _gelu(_dot(st["hb"], win_ref[:, 0:A_WIDTH]))

        def t_v():
            v = _gelu(_dot(st["hb"], win_ref[:, A_WIDTH:2 * A_WIDTH]))
            mu = jnp.mean(v, axis=-1, keepdims=True)
            c = v - mu
            var = jnp.mean(c * c, axis=-1, keepdims=True)
            vn_ref[rows, :] = (c * lax.rsqrt(var + LN_EPS)).astype(BF16)

        return [t_p, t_u, t_v]

    def stage_b(rb):
        def t_gate(n):
            def run():
                rows = slice(n * CHUNK, (n + 1) * CHUNK)
                for h in range(A_HEADS):
                    cols = slice(h * A_GROUP_W, (h + 1) * A_GROUP_W)
                    s = _dot(ws_ref[h].astype(BF16), vn_ref[rows, cols]) + bst_ref[:, h:h + 1]
                    ycat_ref[rows, cols] = (u_ref[rows, cols] * s).astype(BF16)
            return run

        def t_pool(gi):
            def run():
                cols = slice(gi * B_GROUP, (gi + 1) * B_GROUP)
                pooled = jnp.concatenate(
                    [_pool_block(p_ref, gi, rb * ROW_BLOCK + rs * EVEN_RB, POOL_WINDOWS[gi] // 2, seq_len, T)
                     for rs in range(ROW_BLOCK // EVEN_RB)], axis=0)
                yb = _dot(pooled.astype(BF16), wpool_ref[gi].astype(BF16)) * pscale_ref[:, cols]
                ycat_ref[rb * ROW_BLOCK:(rb + 1) * ROW_BLOCK, A_WIDTH + gi * B_GROUP:A_WIDTH + (gi + 1) * B_GROUP] = (
                    yb.astype(BF16))
            return run

        chunks = range(rb * ROW_BLOCK // CHUNK, (rb + 1) * ROW_BLOCK // CHUNK)
        return [t_gate(n) for n in chunks] + [t_pool(gi) for gi in range(len(POOL_WINDOWS))]

    def out_block(rb):
        rows = slice(rb * ROW_BLOCK, (rb + 1) * ROW_BLOCK)
        d = _dot(ycat_ref[rows, :], wout_ref[...])
        r = ALPHA * x_ref[rows, :] + gate * d
        o_ref[rows, :] = _layernorm(r, lng_ref[layer:layer + 1, :], lnb_ref[layer:layer + 1, :])

    for t in stage_a(0):
        t()
    for rb in range(n_blocks):
        nxt = stage_a(rb + 1) if rb + 1 < n_blocks else []
        if nxt:
            nxt[0]()
        _interleave(stage_b(rb), nxt[1:])
        out_block(rb)


def _even_layer(x, mod, w, l, seq_len, row0, row_step, name):
    n_rows = x.shape[0]
    e = l // 2
    kern = functools.partial(_even_kernel, layer=l, seq_len=seq_len, row0=row0, row_step=row_step)
    return pl.pallas_call(
        kern,
        grid=(n_rows // TILE_ROWS,),
        in_specs=[
            pl.BlockSpec((TILE_ROWS, D), lambda i: (i, 0)),
            _mod_spec(l, 0),
            _layer_spec(w["w_in"].shape, e),
            _layer_spec(w["w_s"].shape, e),
            _const_spec(w["b_s_t"].shape),
            _layer_spec(w["w_pool"].shape, e),
            _const_spec(w["pool_scale"].shape),
            _layer_spec(w["w_out"].shape, e),
            _const_spec(w["ln_g"].shape),
            _const_spec(w["ln_b"].shape),
        ],
        out_specs=pl.BlockSpec((TILE_ROWS, D), lambda i: (i, 0)),
        out_shape=jax.ShapeDtypeStruct((n_rows, D), F32),
        scratch_shapes=[
            pltpu.VMEM((TILE_ROWS, A_WIDTH), F32),
            pltpu.VMEM((TILE_ROWS, A_WIDTH), BF16),
            pltpu.VMEM((TILE_ROWS, B_WIDTH), F32),
            pltpu.VMEM((TILE_ROWS, D), BF16),
            pltpu.VMEM((D, 2 * A_WIDTH + B_WIDTH), BF16),
            pltpu.VMEM((D, D), BF16),
        ],
        compiler_params=pltpu.CompilerParams(
            dimension_semantics=("arbitrary",), vmem_limit_bytes=VMEM_LIMIT),
        name=name,
    )(x, mod, w["w_in"], w["w_s"], w["b_s_t"], w["w_pool"], w["pool_scale"], w["w_out"],
      w["ln_g"], w["ln_b"])


def _conv3_block(z_ref, r0, lanes, cw, cb, seq_len):
    nv = FFN_RB // SUBLANES
    z3 = z_ref[r0:r0 + FFN_RB, lanes].reshape(nv, SUBLANES, LANES)
    sub = lax.broadcasted_iota(jnp.int32, (nv, SUBLANES, LANES), 1)
    down = pltpu.roll(z3, 1, 1)
    up = pltpu.roll(z3, SUBLANES - 1, 1)
    if r0 % seq_len == 0:
        prev = jnp.zeros((1, SUBLANES, LANES), F32)
    else:
        prev = pltpu.roll(z_ref[r0 - SUBLANES:r0, lanes], 1, 0)[None]
    if (r0 + FFN_RB) % seq_len == 0:
        nxt = jnp.zeros((1, SUBLANES, LANES), F32)
    else:
        nxt = pltpu.roll(z_ref[r0 + FFN_RB:r0 + FFN_RB + SUBLANES, lanes], SUBLANES - 1, 0)[None]
    zm = jnp.where(sub == 0, jnp.concatenate([prev, down[:-1]], axis=0), down)
    zp = jnp.where(sub == SUBLANES - 1, jnp.concatenate([up[1:], nxt], axis=0), up)
    y = cw[0][:, lanes] * zm + cw[1][:, lanes] * z3 + cw[2][:, lanes] * zp + cb[:, lanes]
    return y.reshape(FFN_RB, LANES)


def _ffn_kernel(x_ref, mod_ref, wup_ref, cw_ref, cb_ref, wd_ref, lng_ref, lnb_ref, o_ref,
                hb_ref, zg0_ref, zv0_ref, zg1_ref, zv1_ref, a_ref, *, layer, seq_len, row0, row_step):
    T = x_ref.shape[0]
    n_chunks = wd_ref.shape[0]
    shift, scale, gate = _mod_rows(mod_ref, row0, row_step)
    z_bufs = ((zg0_ref, zv0_ref), (zg1_ref, zv1_ref))

    def row_block(rb):
        return pl.ds(pl.multiple_of(rb * ROW_BLOCK, ROW_BLOCK), ROW_BLOCK)

    def modulate(rb, carry):
        rows = row_block(rb)
        hb_ref[rows, :] = (x_ref[rows, :] * (1.0 + scale) + shift).astype(BF16)
        return carry

    lax.fori_loop(0, T // ROW_BLOCK, modulate, 0)

    n_parts = T // FFN_DOT_ROWS
    blocks_per_part = FFN_DOT_ROWS // FFN_RB

    def up_proj(c, bufs, q):
        rows = slice(q * FFN_DOT_ROWS, (q + 1) * FFN_DOT_ROWS)
        bufs[0][rows, :] = _dot(hb_ref[rows, :], wup_ref[c])
        bufs[1][rows, :] = _dot(hb_ref[rows, :], wup_ref[c + n_chunks])

    def gate_blocks(c, bufs, q):
        cwg = [cw_ref[k, pl.ds(c, 1), :] for k in range(3)]
        cwv = [cw_ref[k, pl.ds(c + n_chunks, 1), :] for k in range(3)]
        cbg, cbv = cb_ref[pl.ds(c, 1), :], cb_ref[pl.ds(c + n_chunks, 1), :]
        for rb in range(q * blocks_per_part, (q + 1) * blocks_per_part):
            r0 = rb * FFN_RB
            for lt in range(FF_CHUNK // LANES):
                lanes = slice(lt * LANES, (lt + 1) * LANES)
                g = _conv3_block(bufs[0], r0, lanes, cwg, cbg, seq_len)
                v = _conv3_block(bufs[1], r0, lanes, cwv, cbv, seq_len)
                a_ref[c, r0:r0 + FFN_RB, lanes] = (g * jax.nn.sigmoid(g) * v).astype(BF16)

    for q in range(n_parts):
        up_proj(0, z_bufs[0], q)

    def chunk_pair(k, carry):
        c = 2 * k
        for q in range(n_parts):
            up_proj(c + 1, z_bufs[1], q)
            gate_blocks(c, z_bufs[0], q)
        for q in range(n_parts):
            up_proj(c + 2, z_bufs[0], q)
            gate_blocks(c + 1, z_bufs[1], q)
        return carry

    assert n_chunks % 2 == 1
    lax.fori_loop(0, n_chunks // 2, chunk_pair, 0)
    for q in range(n_parts):
        gate_blocks(n_chunks - 1, z_bufs[0], q)

    def out(rb, carry):
        rows = row_block(rb)
        acc = _dot(a_ref[0, rows, :], wd_ref[0])
        for c in range(1, n_chunks):
            acc = acc + _dot(a_ref[c, rows, :], wd_ref[c])
        r = ALPHA * x_ref[rows, :] + gate * acc
        o_ref[rows, :] = _layernorm(r, lng_ref[layer:layer + 1, :], lnb_ref[layer:layer + 1, :])
        return carry

    lax.fori_loop(0, T // ROW_BLOCK, out, 0)


def _ffn_layer(x, mod, w, l, seq_len, row0, row_step, name):
    n_rows = x.shape[0]
    n_chunks = D_FF // FF_CHUNK
    kern = functools.partial(_ffn_kernel, layer=l, seq_len=seq_len, row0=row0, row_step=row_step)
    params = [w["w_up"], w["conv_w"], w["conv_b"], w["w_down"]]
    return pl.pallas_call(
        kern,
        grid=(n_rows // TILE_ROWS,),
        in_specs=[pl.BlockSpec((TILE_ROWS, D), lambda i: (i, 0)), _mod_spec(l, 1)]
        + [_layer_spec(a.shape, l) for a in params]
        + [_const_spec(w["ln_g"].shape), _const_spec(w["ln_b"].shape)],
        out_specs=pl.BlockSpec((TILE_ROWS, D), lambda i: (i, 0)),
        out_shape=jax.ShapeDtypeStruct((n_rows, D), F32),
        scratch_shapes=[
            pltpu.VMEM((TILE_ROWS, D), BF16),
            pltpu.VMEM((TILE_ROWS, FF_CHUNK), F32),
            pltpu.VMEM((TILE_ROWS, FF_CHUNK), F32),
            pltpu.VMEM((TILE_ROWS, FF_CHUNK), F32),
            pltpu.VMEM((TILE_ROWS, FF_CHUNK), F32),
            pltpu.VMEM((n_chunks, TILE_ROWS, FF_CHUNK), BF16),
        ],
        compiler_params=pltpu.CompilerParams(
            dimension_semantics=("arbitrary",), vmem_limit_bytes=VMEM_LIMIT),
        name=name,
    )(x, mod, *params, w["ln_g"], w["ln_b"])


def _prep_up_kernel(w_ref, o_ref):
    for k in range(o_ref.shape[0]):
        o_ref[k] = w_ref[:, k * FF_CHUNK:(k + 1) * FF_CHUNK].astype(BF16)


def _prep_down_kernel(w_ref, o_ref):
    o_ref[...] = w_ref[...].astype(BF16)


def _prep_ffn_weights(w_up, w_down):
    n_ch = 2 * D_FF // FF_CHUNK
    up = pl.pallas_call(
        _prep_up_kernel,
        grid=(DEPTH, 2),
        in_specs=[pl.BlockSpec((None, D, D_FF), lambda l, j: (l, 0, j))],
        out_specs=pl.BlockSpec((None, n_ch // 2, D, FF_CHUNK), lambda l, j: (l, j, 0, 0)),
        out_shape=jax.ShapeDtypeStruct((DEPTH, n_ch, D, FF_CHUNK), BF16),
        compiler_params=pltpu.CompilerParams(
            dimension_semantics=("arbitrary", "arbitrary"), vmem_limit_bytes=VMEM_LIMIT),
        name="ffn_up_bf16",
    )(w_up)
    half = D_FF // 2
    down = pl.pallas_call(
        _prep_down_kernel,
        grid=(DEPTH, 2),
        in_specs=[pl.BlockSpec((None, half, D), lambda l, j: (l, j, 0))],
        out_specs=pl.BlockSpec((None, half, D), lambda l, j: (l, j, 0)),
        out_shape=jax.ShapeDtypeStruct((DEPTH, D_FF, D), BF16),
        compiler_params=pltpu.CompilerParams(
            dimension_semantics=("arbitrary", "arbitrary"), vmem_limit_bytes=VMEM_LIMIT),
        name="ffn_down_bf16",
    )(w_down)
    return up, down.reshape(DEPTH, n_ch // 2, FF_CHUNK, D)


def _conv4_block(xr_ref, cbk, r0, cw, cb, seq_len):
    nv = ODD_RB // SUBLANES
    z3 = xr_ref[cbk, r0:r0 + ODD_RB, :].reshape(nv, SUBLANES, LANES)
    sub = lax.broadcasted_iota(jnp.int32, (nv, SUBLANES, LANES), 1)
    down = pltpu.roll(z3, 1, 1)
    up1 = pltpu.roll(z3, SUBLANES - 1, 1)
    up2 = pltpu.roll(z3, SUBLANES - 2, 1)
    zero = jnp.zeros((1, SUBLANES, LANES), F32)
    if r0 % seq_len == 0:
        prev = zero
    else:
        prev = pltpu.roll(xr_ref[cbk, r0 - SUBLANES:r0, :], 1, 0)[None]
    if (r0 + ODD_RB) % seq_len == 0:
        nxt1 = nxt2 = zero
    else:
        nx = xr_ref[cbk, r0 + ODD_RB:r0 + ODD_RB + SUBLANES, :]
        nxt1 = pltpu.roll(nx, SUBLANES - 1, 0)[None]
        nxt2 = pltpu.roll(nx, SUBLANES - 2, 0)[None]
    zm = jnp.where(sub == 0, jnp.concatenate([prev, down[:-1]], axis=0), down)
    zp1 = jnp.where(sub == SUBLANES - 1, jnp.concatenate([up1[1:], nxt1], axis=0), up1)
    zp2 = jnp.where(sub >= SUBLANES - 2, jnp.concatenate([up2[1:], nxt2], axis=0), up2)
    y = cw[0:1, :] * zm + cw[1:2, :] * z3 + cw[2:3, :] * zp1 + cw[3:4, :] * zp2 + cb
    return y.reshape(ODD_RB, LANES)


def _scan_slabs(a, b, reverse):
    nv = a.shape[0] // SUBLANES
    a = a.reshape(nv, SUBLANES, LANES)
    b = b.reshape(nv, SUBLANES, LANES)
    sub = lax.broadcasted_iota(jnp.int32, (nv, SUBLANES, LANES), 1)
    for d in (1, 2, 4):
        if reverse:
            valid, sh = sub < SUBLANES - d, SUBLANES - d
        else:
            valid, sh = sub >= d, d
        a_prev = jnp.where(valid, pltpu.roll(a, sh, 1), 1.0)
        b_prev = jnp.where(valid, pltpu.roll(b, sh, 1), 0.0)
        b = b + a * b_prev
        a = a * a_prev
    return a.reshape(nv * SUBLANES, LANES), b.reshape(nv * SUBLANES, LANES)


def _scan_ends(a, b, posg, n_groups, reverse):
    n = a.shape[0]
    d = 1
    while d < n_groups:
        if reverse:
            valid, sh = posg < n_groups - d, n - d
        else:
            valid, sh = posg >= d, d
        a_prev = jnp.where(valid, pltpu.roll(a, sh, 0), 1.0)
        b_prev = jnp.where(valid, pltpu.roll(b, sh, 0), 0.0)
        b = b + a * b_prev
        a = a * a_prev
        d *= 2
    return a, b


def _rope_pairs(x, cos, sin_signed, even_lane):
    n = x.shape[1]
    swapped = jnp.where(even_lane, pltpu.roll(x, n - 1, 1), pltpu.roll(x, 1, 1))
    return x * cos + swapped * sin_signed


def _rmsnorm(x, g):
    return x * lax.rsqrt(jnp.mean(x * x, axis=-1, keepdims=True) + RMS_EPS) * g


def _odd_kernel(*refs, layer, seq_len, latent, past_len, row0, row_step):
    n_in = 21 if latent else 16
    n_out = 1 if latent else 4
    (x_ref, mod_ref, win_ref, cw_ref, cb_ref, wgate_ref, bgate_ref, lam_ref, qg_ref, kvg_ref,
     wuq_ref, wuk_ref, wuv_ref, wout_ref, lng_ref, lnb_ref) = refs[:16]
    if latent:
        cckv_ref, ckr_ref, st_ref, cos_ref, sin_ref = refs[16:21]
        (o_ref,) = refs[n_in:n_in + n_out]
    else:
        o_ref, ckv_out_ref, kr_out_ref, st_out_ref = refs[n_in:n_in + n_out]
    (xr_ref, gr_ref, xc_ref, pre_ref, a_ref, b_ref, hrep_ref, q_ref, k_ref, v_ref, yd_ref) = refs[n_in + n_out:]

    T = x_ref.shape[0]
    n_blocks = T // ROW_BLOCK
    n_tiles = W_C // LANES
    half_w = W_C // 2
    shift, scale, gate = _mod_rows(mod_ref, row0, row_step)
    o_q = 2 * W_C
    o_kv = o_q + Q_LORA
    o_kr = o_kv + KV_LORA
    pair_w = 2 * HEAD_PAD
    even_lane = (lax.broadcasted_iota(jnp.int32, (ROW_BLOCK, HEAD_PAD), 1) & 1) == 0
    softplus = []
    for direction in range(2):
        neg = -lam_ref[direction:direction + 1, :]
        softplus.append(jnp.maximum(neg, 0.0) + jnp.log1p(jnp.exp(-jnp.abs(neg))))

    def stage_a(rb):
        rows = slice(rb * ROW_BLOCK, (rb + 1) * ROW_BLOCK)
        k_rows = slice(past_len + rb * ROW_BLOCK, past_len + (rb + 1) * ROW_BLOCK)
        st = {}

        def t_xr():
            st["hb"] = (x_ref[rows, :] * (1.0 + scale) + shift).astype(BF16)
            z = _dot(st["hb"], win_ref[:, 0:W_C])
            for cbk in range(n_tiles):
                xr_ref[cbk, rows, :] = z[:, cbk * LANES:(cbk + 1) * LANES]

        def t_gr():
            gr_ref[rows, :] = _dot(st["hb"], win_ref[:, W_C:o_q])

        def t_lat():
            hb = st["hb"]
            st["qn"] = _rmsnorm(_dot(hb, win_ref[:, o_q:o_kv]), qg_ref[...]).astype(BF16)
            ckv = _rmsnorm(_dot(hb, win_ref[:, o_kv:o_kr]), kvg_ref[...])
            kr = _dot(hb, win_ref[:, o_kr:o_kr + HEAD_PAD])
            if latent:
                st["cos"] = cos_ref[rows, :]
                st["sin"] = sin_ref[rows, :]
                kr = _rope_pairs(kr, st["cos"], st["sin"], even_lane)
            else:
                ckv_out_ref[rows, :] = ckv
                kr_out_ref[rows, :] = kr[:, KR_LANE:KR_LANE + QK_ROPE]
            st["kr"] = kr
            st["ckvb"] = ckv.astype(BF16)

        def t_v():
            v_ref[k_rows, :] = _dot(st["ckvb"], wuv_ref[...]).astype(BF16)

        def t_pair(pair):
            def run():
                pcols = slice(pair * pair_w, (pair + 1) * pair_w)
                q2 = _dot(st["qn"], wuq_ref[:, pcols])
                k2 = _dot(st["ckvb"], wuk_ref[:, pcols])
                for hh in range(2):
                    cols = slice(pair * pair_w + hh * HEAD_PAD, pair * pair_w + (hh + 1) * HEAD_PAD)
                    qh = q2[:, hh * HEAD_PAD:(hh + 1) * HEAD_PAD]
                    if latent:
                        qh = _rope_pairs(qh, st["cos"], st["sin"], even_lane)
                    q_ref[rows, cols] = qh.astype(BF16)
                    k_ref[k_rows, cols] = (k2[:, hh * HEAD_PAD:(hh + 1) * HEAD_PAD] + st["kr"]).astype(BF16)
            return run

        return [t_xr, t_gr, t_lat, t_v] + [t_pair(p) for p in range(D_HEADS // 2)]

    def conv_thunks(rb):
        par = rb % xc_ref.shape[0]

        def t_conv(cbk):
            def run():
                cw = cw_ref[:, cbk * LANES:(cbk + 1) * LANES]
                cb = cb_ref[:, cbk * LANES:(cbk + 1) * LANES]
                for rs in range(ROW_BLOCK // ODD_RB):
                    r0 = rb * ROW_BLOCK + rs * ODD_RB
                    xc_ref[par, rs * ODD_RB:(rs + 1) * ODD_RB, cbk * LANES:(cbk + 1) * LANES] = _conv4_block(
                        xr_ref, cbk, r0, cw, cb, seq_len)
            return run

        return [t_conv(cbk) for cbk in range(n_tiles)]

    def gate_thunks(rb):
        par = rb % xc_ref.shape[0]
        thunks = []
        for direction in range(2):
            for j in range(2):
                slot = (2 * direction + j) % pre_ref.shape[0]

                def t_dot(direction=direction, j=j, slot=slot):
                    xcb = xc_ref[par, :, j * half_w:(j + 1) * half_w].astype(BF16)
                    pre_ref[slot, 0] = _dot(xcb, wgate_ref[direction, 0, j])
                    pre_ref[slot, 1] = _dot(xcb, wgate_ref[direction, 1, j])

                def t_post(direction=direction, j=j, slot=slot):
                    for rs in range(ROW_BLOCK // ODD_RB):
                        lrows = slice(rs * ODD_RB, (rs + 1) * ODD_RB)
                        grows = slice(rb * ROW_BLOCK + rs * ODD_RB, rb * ROW_BLOCK + (rs + 1) * ODD_RB)
                        for lt in range(half_w // LANES):
                            cbk = j * (half_w // LANES) + lt
                            lanes = slice(lt * LANES, (lt + 1) * LANES)
                            glanes = slice(cbk * LANES, (cbk + 1) * LANES)
                            xcv = xc_ref[par, lrows, glanes]
                            r = jax.nn.sigmoid(pre_ref[slot, 0, lrows, lanes] + bgate_ref[direction, 0:1, glanes])
                            i = jax.nn.sigmoid(pre_ref[slot, 1, lrows, lanes] + bgate_ref[direction, 1:2, glanes])
                            log_a = -RG_C * r * softplus[direction][:, glanes]
                            a = jnp.exp(log_a)
                            b = jnp.sqrt(jnp.tanh(-log_a) * (a * a + 1.0)) * i * xcv
                            a, b = _scan_slabs(a, b, direction == 1)
                            a_ref[direction, cbk, grows, :] = a
                            b_ref[direction, cbk, grows, :] = b

                thunks += [t_dot, t_post]
        return thunks

    def combine(direction, cbk, t0, n_rows, h0):
        reverse = direction == 1
        n_ends = n_rows // SUBLANES
        end_row = 0 if reverse else SUBLANES - 1
        posg = lax.broadcasted_iota(jnp.int32, (n_ends, LANES), 0)
        a_end = a_ref[direction, cbk, pl.ds(t0 + end_row, n_ends, stride=SUBLANES), :]
        b_end = b_ref[direction, cbk, pl.ds(t0 + end_row, n_ends, stride=SUBLANES), :]
        a_tot, b_tot = _scan_ends(a_end, b_end, posg, n_ends, reverse)
        ends = b_tot if h0 is None else a_tot * h0 + b_tot
        first = jnp.zeros((1, LANES), F32) if h0 is None else h0
        if reverse:
            carry_in = jnp.where(posg < n_ends - 1, pltpu.roll(ends, n_ends - 1, 0), first)
        else:
            carry_in = jnp.where(posg >= 1, pltpu.roll(ends, 1, 0), first)
        for rr in range(SUBLANES):
            hrep_ref[pl.ds(t0 + rr, n_ends, stride=SUBLANES), :] = carry_in
        for rs in range(n_rows // ODD_RB):
            rows = slice(t0 + rs * ODD_RB, t0 + (rs + 1) * ODD_RB)
            h = a_ref[direction, cbk, rows, :] * hrep_ref[rows, :] + b_ref[direction, cbk, rows, :]
            if direction == 0:
                xr_ref[cbk, rows, :] = h
            else:
                xr_ref[cbk, rows, :] = xr_ref[cbk, rows, :] + h
        return ends

    def attend_pair(q_rows, k_rows, pair):
        o_pair = None
        for h in (2 * pair, 2 * pair + 1):
            cols = slice(h * HEAD_PAD, (h + 1) * HEAD_PAD)
            s = lax.dot_general(q_ref[q_rows, cols], k_ref[k_rows, cols],
                                (((1,), (1,)), ((), ())), preferred_element_type=F32) * ATTN_SCALE
            m = jnp.max(s, axis=-1, keepdims=True)
            p = jnp.exp(s - m)
            l = jnp.sum(p, axis=-1, keepdims=True)
            o = _dot(p.astype(BF16), v_ref[k_rows, cols]) / l
            o_pair = o if o_pair is None else o_pair + o
        yd_ref[q_rows, pair * HEAD_PAD:(pair + 1) * HEAD_PAD] = o_pair.astype(BF16)

    def out_block(rows):
        yc = jnp.concatenate(
            [xr_ref[cbk, rows, :] * _gelu(gr_ref[rows, cbk * LANES:(cbk + 1) * LANES]) for cbk in range(n_tiles)],
            axis=1).astype(BF16)
        d = _dot(yc, wout_ref[0:W_C, :]) + _dot(yd_ref[rows, :], wout_ref[W_C:, :])
        r = ALPHA * x_ref[rows, :] + gate * d
        o_ref[rows, :] = _layernorm(r, lng_ref[layer:layer + 1, :], lnb_ref[layer:layer + 1, :])

    if not latent:
        def stage_b(s):
            t0 = s * seq_len
            rows = slice(t0, t0 + seq_len)
            thunks = conv_thunks(s) + gate_thunks(s)
            for direction in range(2):
                for cbk in range(n_tiles):
                    def t_comb(direction=direction, cbk=cbk):
                        ends = combine(direction, cbk, t0, seq_len, None)
                        last = 0 if direction == 1 else seq_len // SUBLANES - 1
                        st_out_ref[s, direction:direction + 1, cbk * LANES:(cbk + 1) * LANES] = ends[last:last + 1, :]
                    thunks.append(t_comb)
            return thunks

        assert seq_len == ROW_BLOCK
        for t in stage_a(0):
            t()
        for s in range(n_blocks):
            rows = slice(s * seq_len, (s + 1) * seq_len)
            attn = [functools.partial(attend_pair, rows, rows, pair) for pair in range(D_HEADS // 2)]
            nxt = stage_a(s + 1) if s + 1 < n_blocks else []
            mxu = []
            _interleave([functools.partial(mxu.append, t) for t in nxt] or [lambda: None],
                        [functools.partial(mxu.append, t) for t in attn])
            _interleave(stage_b(s), mxu)
            out_block(rows)
        return

    assert seq_len == T
    for t in stage_a(0):
        t()
    for rb in range(n_blocks):
        nxt = stage_a(rb + 1) if rb + 1 < n_blocks else []
        if nxt:
            nxt[0]()
        _interleave(conv_thunks(rb) + gate_thunks(rb), nxt[1:])

    def cached(cbk, carry):
        rows = pl.ds(pl.multiple_of(cbk * ROW_BLOCK, ROW_BLOCK), ROW_BLOCK)
        cckv = cckv_ref[rows, :].astype(BF16)
        ckr = ckr_ref[rows, :]
        v_ref[rows, :] = _dot(cckv, wuv_ref[...]).astype(BF16)
        for pair in range(D_HEADS // 2):
            pcols = slice(pair * pair_w, (pair + 1) * pair_w)
            k2 = _dot(cckv, wuk_ref[:, pcols])
            for hh in range(2):
                cols = slice(pair * pair_w + hh * HEAD_PAD, pair * pair_w + (hh + 1) * HEAD_PAD)
                k_ref[rows, cols] = (k2[:, hh * HEAD_PAD:(hh + 1) * HEAD_PAD] + ckr).astype(BF16)
        return carry

    lax.fori_loop(0, past_len // ROW_BLOCK, cached, 0)

    n_keys = past_len + seq_len

    def tail(qb, carry):
        q_rows = pl.ds(pl.multiple_of(qb * ROW_BLOCK, ROW_BLOCK), ROW_BLOCK)
        k_rows = slice(0, n_keys)
        rg = []
        for direction in range(2):
            def t_comb(direction=direction):
                h0 = st_ref[direction, pl.ds(qb, 1), :]
                combine(direction, qb, 0, T, h0)
            rg.append(t_comb)
        _interleave([functools.partial(attend_pair, q_rows, k_rows, pair) for pair in range(D_HEADS // 2)], rg)
        return carry

    assert n_tiles == n_blocks
    lax.fori_loop(0, n_blocks, tail, 0)

    def out(rb, carry):
        out_block(pl.ds(pl.multiple_of(rb * ROW_BLOCK, ROW_BLOCK), ROW_BLOCK))
        return carry

    lax.fori_loop(0, n_blocks, out, 0)


def _odd_layer(x, mod, w, l, seq_len, row0, row_step, name, latent_inputs=None):
    n_rows = x.shape[0]
    n_tiles = n_rows // TILE_ROWS
    latent = latent_inputs is not None
    past_len = latent_inputs["cache_ckv"].shape[1] if latent else 0
    n_seq = TILE_ROWS // seq_len
    kern = functools.partial(_odd_kernel, layer=l, seq_len=seq_len, latent=latent, past_len=past_len,
                             row0=row0, row_step=row_step)
    weights = [w["w_in"], w["conv_w"], w["conv_b"], w["w_gate"], w["b_gate"], w["lam"], w["q_g"], w["kv_g"],
               w["w_uq"], w["w_uk"], w["w_uv"], w["w_out"]]
    out_mode = pl.Buffered(1) if latent else pl.Buffered(2)
    in_specs = [pl.BlockSpec((TILE_ROWS, D), lambda i: (i, 0)),
                _mod_spec(l, 0)]
    in_specs += [_const_spec(a.shape) for a in weights]
    in_specs += [_const_spec(w["ln_g"].shape), _const_spec(w["ln_b"].shape)]
    args = [x, mod] + weights + [w["ln_g"], w["ln_b"]]
    out_specs = [pl.BlockSpec((TILE_ROWS, D), lambda i: (i, 0), pipeline_mode=out_mode)]
    out_shape = [jax.ShapeDtypeStruct((n_rows, D), F32)]
    if latent:
        li = latent_inputs
        in_specs += [
            pl.BlockSpec((None, past_len, KV_LORA), lambda i: (i, 0, 0)),
            pl.BlockSpec((None, past_len, HEAD_PAD), lambda i: (i, 0, 0)),
            pl.BlockSpec((None, 2, W_C // LANES, LANES), lambda i: (i, 0, 0, 0)),
            _const_spec(li["cos"].shape),
            _const_spec(li["sin"].shape),
        ]
        args += [li["cache_ckv"], li["cache_kr"], li["state"], li["cos"], li["sin"]]
    else:
        out_specs += [
            pl.BlockSpec((TILE_ROWS, KV_LORA), lambda i: (i, 0)),
            pl.BlockSpec((TILE_ROWS, QK_ROPE), lambda i: (i, 0)),
            pl.BlockSpec((n_seq, 2, W_C), lambda i: (i, 0, 0)),
        ]
        out_shape += [
            jax.ShapeDtypeStruct((n_rows, KV_LORA), F32),
            jax.ShapeDtypeStruct((n_rows, QK_ROPE), F32),
            jax.ShapeDtypeStruct((n_rows // seq_len, 2, W_C), F32),
        ]
    n_keys_buf = past_len + TILE_ROWS
    n_slots = 1 if latent else 2
    scratch = [
        pltpu.VMEM((W_C // LANES, TILE_ROWS, LANES), F32),
        pltpu.VMEM((TILE_ROWS, W_C), F32),
        pltpu.VMEM((n_slots, ROW_BLOCK, W_C), F32),
        pltpu.VMEM((n_slots, 2, ROW_BLOCK, W_C // 2), F32),
        pltpu.VMEM((2, W_C // LANES, TILE_ROWS, LANES), F32),
        pltpu.VMEM((2, W_C // LANES, TILE_ROWS, LANES), F32),
        pltpu.VMEM((TILE_ROWS, LANES), F32),
        pltpu.VMEM((TILE_ROWS, D_HEADS * HEAD_PAD), BF16),
        pltpu.VMEM((n_keys_buf, D_HEADS * HEAD_PAD), BF16),
        pltpu.VMEM((n_keys_buf, D_HEADS * HEAD_PAD), BF16),
        pltpu.VMEM((TILE_ROWS, W_C), BF16),
    ]
    return pl.pallas_call(
        kern,
        grid=(n_tiles,),
        in_specs=in_specs,
        out_specs=out_specs,
        out_shape=out_shape,
        scratch_shapes=scratch,
        compiler_params=pltpu.CompilerParams(
            dimension_semantics=("arbitrary",), vmem_limit_bytes=VMEM_LIMIT),
        name=name,
    )(*args)


def _axial_rope_tables(rows):
    half = QK_ROPE // 2
    inv = (ROPE_BASE ** (-np.arange(0, half, 2, dtype=np.float32) / half)).astype(np.float32)
    r = np.repeat(np.arange(rows, dtype=np.float32), GRID_W)
    col = np.tile(np.arange(GRID_W, dtype=np.float32), rows)
    ang = np.concatenate([r[:, None] * inv, col[:, None] * inv], axis=-1).astype(np.float32)
    cos = np.repeat(np.cos(ang), 2, axis=-1)
    sin = np.repeat(np.sin(ang), 2, axis=-1) * np.tile(np.array([-1.0, 1.0], np.float32), half)
    n = ang.shape[0]
    cos_t = np.ones((n, HEAD_PAD), np.float32)
    sin_t = np.zeros((n, HEAD_PAD), np.float32)
    cos_t[:, KR_LANE:KR_LANE + QK_ROPE] = cos
    sin_t[:, KR_LANE:KR_LANE + QK_ROPE] = sin
    return jnp.asarray(cos_t), jnp.asarray(sin_t)


def _odd_prep_kernel(win_ref, wa_ref, wx_ref, wuq_ref, wuk_ref, wuv_ref, wout_ref,
                     win_o, gate_o, wuq_o, wuk_o, wuv_o, wout_o):
    n_main = 2 * W_C + Q_LORA + KV_LORA
    win_o[:, 0:n_main] = win_ref[:, 0:n_main].astype(BF16)
    win_o[:, n_main:] = jnp.zeros((D, HEAD_PAD), BF16)
    win_o[:, n_main + KR_LANE:n_main + KR_LANE + QK_ROPE] = win_ref[:, n_main:].astype(BF16)
    wout_o[...] = wout_ref[...].astype(BF16)
    wuq_o[...] = jnp.zeros_like(wuq_o)
    wuk_o[...] = jnp.zeros_like(wuk_o)
    wuv_o[...] = jnp.zeros_like(wuv_o)
    for h in range(D_HEADS):
        wuq_o[:, h * HEAD_PAD:h * HEAD_PAD + QK_NOPE + QK_ROPE] = wuq_ref[:, h, :].astype(BF16)
        wuk_o[:, h * HEAD_PAD:h * HEAD_PAD + QK_NOPE] = wuk_ref[:, h, :].astype(BF16)
        v0 = h * HEAD_PAD + (h % 2) * V_DIM
        wuv_o[:, v0:v0 + V_DIM] = wuv_ref[:, h, :].astype(BF16)
    gate_o[...] = jnp.zeros_like(gate_o)
    per = (W_C // 2) // C_BLOCK
    for direction in range(2):
        for g, w_ref in enumerate((wa_ref, wx_ref)):
            for h in range(C_HEADS):
                j, hh = divmod(h, per)
                blk = slice(hh * C_BLOCK, (hh + 1) * C_BLOCK)
                gate_o[direction, g, j, blk, blk] = w_ref[direction, h].astype(BF16)


def _prep_odd_weights(w_in, w_a, w_x, w_uq, w_uk, w_uv, w_out):
    ins = [w_in, w_a, w_x, w_uq, w_uk, w_uv, w_out]
    outs = [
        jax.ShapeDtypeStruct((D, ODD_IN_PAD), BF16),
        jax.ShapeDtypeStruct((2, 2, 2, W_C // 2, W_C // 2), BF16),
        jax.ShapeDtypeStruct((Q_LORA, D_HEADS * HEAD_PAD), BF16),
        jax.ShapeDtypeStruct((KV_LORA, D_HEADS * HEAD_PAD), BF16),
        jax.ShapeDtypeStruct((KV_LORA, D_HEADS * HEAD_PAD), BF16),
        jax.ShapeDtypeStruct((D, D), BF16),
    ]
    return pl.pallas_call(
        _odd_prep_kernel,
        in_specs=[pl.BlockSpec(a.shape, lambda n=a.ndim: (0,) * n) for a in ins],
        out_specs=[pl.BlockSpec(o.shape, lambda n=len(o.shape): (0,) * n) for o in outs],
        out_shape=outs,
        compiler_params=pltpu.CompilerParams(vmem_limit_bytes=VMEM_LIMIT),
        name="odd_weights_bf16",
    )(*ins)


def kernel(x_prompt, x_sample, cache_mla_ckv, cache_mla_krope, state_rglru, c, c_ctx, w_mod, b_mod, ln1_g, ln1_b, ln2_g, ln2_b, even_w_in, even_w_s, even_b_s, even_w_pool, even_pool_scale, even_w_out, odd_w_in, rg_conv_w, rg_conv_b, rg_w_a, rg_b_a, rg_w_x, rg_b_x, rg_lam, mla_q_g, mla_kv_g, mla_w_uq, mla_w_uk, mla_w_uv, odd_w_out, ffn_w_up, ffn_conv_w, ffn_conv_b, ffn_w_down):
    batch, seq, _ = x_prompt.shape
    dec_batch, dec_seq, _ = x_sample.shape
    assert TILE_ROWS % seq == 0 and dec_seq == TILE_ROWS and dec_batch + 1 <= SUBLANES

    mod = _modulation(c_ctx, c, w_mod, b_mod)

    wup_bf, wdown_bf = _prep_ffn_weights(ffn_w_up, ffn_w_down)
    n_ch = 2 * D_FF // FF_CHUNK
    wf = {"ln_g": ln2_g, "ln_b": ln2_b,
          "w_up": wup_bf, "conv_w": ffn_conv_w.reshape(DEPTH, 3, n_ch, FF_CHUNK),
          "conv_b": ffn_conv_b.reshape(DEPTH, n_ch, FF_CHUNK), "w_down": wdown_bf}
    ln1 = {"ln_g": ln1_g, "ln_b": ln1_b}

    xp = x_prompt.reshape(batch * seq, D)
    xs = x_sample.reshape(dec_batch * dec_seq, D)
    new_ckv = new_kr = new_state = None
    for l in range(DEPTH):
        if l % 2 == 0:
            e = l // 2
            w = dict(ln1, w_in=even_w_in, w_s=even_w_s, b_s_t=even_b_s[e].T, w_pool=even_w_pool,
                     pool_scale=even_pool_scale[e][None, :], w_out=even_w_out)
            xp = _even_layer(xp, mod, w, l, seq, 0, 0, f"even{l}_ctx")
            xs = _even_layer(xs, mod, w, l, dec_seq, 1, 1, f"even{l}_lat")
        else:
            o = l // 2
            w_in_bf, w_gate_bf, w_uq_bf, w_uk_bf, w_uv_bf, w_out_bf = _prep_odd_weights(
                odd_w_in[o], rg_w_a[o], rg_w_x[o], mla_w_uq[o], mla_w_uk[o], mla_w_uv[o], odd_w_out[o])
            b_gate = jnp.stack([rg_b_a[o], rg_b_x[o]], axis=1)
            w = dict(ln1,
                     w_in=w_in_bf, conv_w=rg_conv_w[o], conv_b=rg_conv_b[o][None, :],
                     w_gate=w_gate_bf, b_gate=b_gate, lam=rg_lam[o],
                     q_g=mla_q_g[o][None, :], kv_g=mla_kv_g[o][None, :],
                     w_uq=w_uq_bf, w_uk=w_uk_bf, w_uv=w_uv_bf, w_out=w_out_bf)
            xp, ckv, kr, st = _odd_layer(xp, mod, w, l, seq, 0, 0, f"odd{l}_ctx")
            new_ckv = ckv.reshape(batch, 1, seq, KV_LORA)
            new_kr = kr.reshape(batch, 1, seq, QK_ROPE)
            new_state = st.reshape(batch, 1, 2, W_C)
            cos_t, sin_t = _axial_rope_tables(dec_seq // GRID_W)
            latent_inputs = {
                "cache_ckv": cache_mla_ckv[:, o],
                "cache_kr": jnp.pad(cache_mla_krope[:, o],
                                    ((0, 0), (0, 0), (KR_LANE, HEAD_PAD - KR_LANE - QK_ROPE))),
                "state": state_rglru[:, o].reshape(dec_batch, 2, W_C // LANES, LANES),
                "cos": cos_t, "sin": sin_t,
            }
            (xs,) = _odd_layer(xs, mod, w, l, dec_seq, 1, 1, f"odd{l}_lat", latent_inputs)
        xp = _ffn_layer(xp, mod, wf, l, seq, 0, 0, f"ffn{l}_ctx")
        xs = _ffn_layer(xs, mod, wf, l, dec_seq, 1, 1, f"ffn{l}_lat")
    return (xp.reshape(batch, seq, D), xs.reshape(dec_batch, dec_seq, D), new_ckv, new_kr, new_state)
```
